```python
import math
import jax, jax.numpy as jnp
from jax import lax
import numpy as np

D_MODEL = 2048
BATCH = 2
SEQ = 4096
DEPTH = 4
DEC_BATCH = 8
DEC_SEQ = 4
PAST_LEN = 16384
PAGE_SIZE = 128

H_A = 16
HD_A = 128
KV_A = 2
HPG_A = H_A // KV_A
CMP_BLK = 32
SLC_BLK = 64
N_SEL = 16
WINDOW = 512
Q_BLK = 64
FORCE_BONUS = 1.0e4
NEG_INF = -1.0e30
N_BUCKETS = 32
MAX_DIST = 128
D_INNER_B = 2048
HD_B = 64
H_B = D_INNER_B // HD_B
G_B = 4
N_B = 128
CONV_B = 4
SSD_CHUNK = 128
XBC_DIM = D_INNER_B + 2 * G_B * N_B
D_C = 2048
G_C = 8
CHUNK_C = 128
D_FF = 5632
CONV_F = 3
NORM_EPS = 1e-6
SPLIT_SIZES = (H_A * HD_A, 2 * KV_A * HD_A, 2 * KV_A * HD_A, 2 * KV_A * HD_A, 3 * H_A,
               D_INNER_B, XBC_DIM, H_B, D_C, D_C, 3 * D_MODEL)
IN_DIM = sum(SPLIT_SIZES)

kernel_name = 'nsa_ssd_chunkmlp_gated_hybrid_step'


def rms_norm(x, g):
    xf = x.astype(jnp.float32)
    y = xf * lax.rsqrt(jnp.mean(xf * xf, axis=-1, keepdims=True) + NORM_EPS)
    return (y * g).astype(x.dtype)


def group_rms_norm(x, g, groups):
    n, t, c = x.shape
    xf = x.astype(jnp.float32).reshape(n, t, groups, c // groups)
    xf = xf * lax.rsqrt(jnp.mean(xf * xf, axis=-1, keepdims=True) + NORM_EPS)
    return (xf.reshape(n, t, c) * g).astype(x.dtype)


def layer_norm(x, g, b):
    xf = x.astype(jnp.float32)
    mu = jnp.mean(xf, axis=-1, keepdims=True)
    xc = xf - mu
    var = jnp.mean(xc * xc, axis=-1, keepdims=True)
    return (xc * lax.rsqrt(var + NORM_EPS) * g + b).astype(x.dtype)


def t5_bucket(dist):
    n = jnp.maximum(dist, 0)
    max_exact = N_BUCKETS // 2
    nf = jnp.maximum(n, 1).astype(jnp.float32)
    large = max_exact + (jnp.log(nf / max_exact) / math.log(MAX_DIST / max_exact)
                         * (N_BUCKETS - max_exact)).astype(jnp.int32)
    return jnp.where(n < max_exact, n, jnp.minimum(large, N_BUCKETS - 1))


def causal_dwconv(x, prev, w, b):
    k = w.shape[0]
    t = x.shape[1]
    xp = jnp.concatenate([prev.astype(x.dtype), x], axis=1)
    y = b
    for i in range(k):
        y = y + xp[:, i:i + t] * w[i]
    return y, xp[:, t:]


def gather_pages(cache_l, page_table):
    g = cache_l[page_table]
    return g.reshape((page_table.shape[0], page_table.shape[1] * cache_l.shape[1]) + cache_l.shape[2:])


def nsa_attention(q, gates, kv_cmp, kv_slc, kv_win, q_pos0, cmp_pe, cmp_w1, cmp_w2, rel_bias):
    n, tq = q.shape[:2]
    t = kv_cmp.shape[1]
    scale = HD_A ** -0.5
    qg = q.reshape(n, tq, KV_A, HPG_A, HD_A)
    qpos = q_pos0 + jnp.arange(tq)
    rb = rel_bias.astype(jnp.float32).reshape(N_BUCKETS, KV_A, HPG_A)

    nb = t // CMP_BLK
    blk = kv_cmp[:, :nb * CMP_BLK].reshape(n, nb, CMP_BLK, 2, KV_A, HD_A) + cmp_pe[None, None, :, :, None, :]
    hid = jax.nn.gelu(jnp.einsum('nbscgd,csde->nbcge', blk, cmp_w1))
    kvc = jnp.einsum('nbcge,cef->nbcgf', hid, cmp_w2)
    kc, vc = kvc[:, :, 0], kvc[:, :, 1]
    blk_end = jnp.arange(nb) * CMP_BLK + (CMP_BLK - 1)
    vis = blk_end[None, :] <= qpos[:, None]
    bias_c = rb[t5_bucket(qpos[:, None] - blk_end[None, :])]
    lc = jnp.einsum('nqghd,nbgd->nqghb', qg, kc).astype(jnp.float32) * scale + bias_c.transpose(0, 2, 3, 1)[None]
    lc = jnp.where(vis[None, :, None, None, :], lc, NEG_INF)
    pc = jax.nn.softmax(lc, axis=-1) * jnp.any(vis, axis=-1)[None, :, None, None, None].astype(jnp.float32)
    o_cmp = jnp.einsum('nqghb,nbgd->nqghd', pc.astype(vc.dtype), vc)

    ratio = SLC_BLK // CMP_BLK
    ns = -(-t // SLC_BLK)
    imp = jnp.sum(pc, axis=3)
    imp = jnp.pad(imp, ((0, 0), (0, 0), (0, 0), (0, ns * ratio - nb))).reshape(n, tq, KV_A, ns, ratio).sum(-1)
    j = jnp.arange(ns)
    cur = qpos // SLC_BLK
    forced = (j[None, :] == 0) | (j[None, :] == cur[:, None]) | (j[None, :] == cur[:, None] - 1)
    valid = j[None, :] <= cur[:, None]
    score = jnp.where(valid[None, :, None, :],
                      imp + jnp.where(forced, FORCE_BONUS, 0.0)[None, :, None, :], -1.0)
    n_top = min(N_SEL, ns)
    _, sel = lax.top_k(score, n_top)

    pad = ns * SLC_BLK - t
    kvs_b = jnp.pad(kv_slc, ((0, 0), (0, pad), (0, 0), (0, 0), (0, 0)))
    kvs_b = kvs_b.reshape(n, ns, SLC_BLK, 2, KV_A, HD_A).transpose(0, 4, 1, 2, 3, 5)

    qb_size = min(Q_BLK, tq)
    n_qb = tq // qb_size
    offs_s = jnp.arange(SLC_BLK)
    offs_w = jnp.arange(WINDOW + qb_size)
    garange = jnp.arange(KV_A)
    narange = jnp.arange(n)

    def one_block(i):
        q0 = i * qb_size
        qb = lax.dynamic_slice_in_dim(qg, q0, qb_size, axis=1)
        sb = lax.dynamic_slice_in_dim(sel, q0, qb_size, axis=1)
        pos = q_pos0 + q0 + jnp.arange(qb_size)
        ii = sb.transpose(0, 2, 1, 3).reshape(n, KV_A, qb_size * n_top)
        g = kvs_b[narange[:, None, None], garange[None, :, None], ii]
        g = g.reshape(n, KV_A, qb_size, n_top, SLC_BLK, 2, HD_A)
        kpos = sb[..., None] * SLC_BLK + offs_s
        dist = pos[None, :, None, None, None] - kpos
        bias_s = rb[t5_bucket(dist), garange[None, None, :, None, None]]
        ls = jnp.einsum('nqghd,ngqksd->nqghks', qb, g[..., 0, :]).astype(jnp.float32) * scale \
            + bias_s.transpose(0, 1, 2, 5, 3, 4)
        ls = jnp.where((dist >= 0)[:, :, :, None], ls, NEG_INF)
        ps = jax.nn.softmax(ls, axis=(-2, -1))
        o_s = jnp.einsum('nqghks,ngqksd->nqghd', ps.astype(g.dtype), g[..., 1, :])
        w = lax.dynamic_slice_in_dim(kv_win, q0, WINDOW + qb_size, axis=1)
        wpos = q_pos0 - WINDOW + q0 + offs_w
        dw = pos[:, None] - wpos[None, :]
        okw = (dw >= 0) & (dw < WINDOW) & (wpos >= 0)[None, :]
        bias_w = rb[t5_bucket(dw)]
        lw = jnp.einsum('nqghd,nkgd->nqghk', qb, w[:, :, 0]).astype(jnp.float32) * scale \
            + bias_w.transpose(0, 2, 3, 1)[None]
        lw = jnp.where(okw[None, :, None, None, :], lw, NEG_INF)
        pw = jax.nn.softmax(lw, axis=-1)
        o_w = jnp.einsum('nqghk,nkgd->nqghd', pw.astype(w.dtype), w[:, :, 1])
        return o_s, o_w

    o_s, o_w = lax.map(one_block, jnp.arange(n_qb))
    o_s = jnp.moveaxis(o_s, 0, 1).reshape(n, tq, KV_A, HPG_A, HD_A)
    o_w = jnp.moveaxis(o_w, 0, 1).reshape(n, tq, KV_A, HPG_A, HD_A)
    gt = gates.reshape(n, tq, KV_A, HPG_A, 3)
    o = gt[..., 0:1] * o_cmp + gt[..., 1:2] * o_s + gt[..., 2:3] * o_w
    return o.reshape(n, tq, H_A * HD_A).astype(q.dtype)


def ssd_scan(xh, dt, a, bm, cm, h0):
    n, t = xh.shape[:2]
    q = min(SSD_CHUNK, t)
    nc = t // q
    hpg = H_B // G_B

    def to_chunks(v):
        return jnp.moveaxis(v.reshape((n, nc, q) + v.shape[2:]), 1, 0)

    xc = to_chunks(xh.astype(jnp.float32).reshape(n, t, G_B, hpg, HD_B))
    dc = to_chunks(dt.reshape(n, t, G_B, hpg))
    bc = to_chunks(bm.astype(jnp.float32))
    cc = to_chunks(cm.astype(jnp.float32))
    ag = a.reshape(G_B, hpg)
    tril = jnp.tril(jnp.ones((q, q), bool))

    def step(h, inp):
        x, d, b, c = inp
        cs = jnp.cumsum(d * ag, axis=1)
        seg = cs[:, :, None] - cs[:, None]
        decay = jnp.exp(jnp.where(tril[None, :, :, None, None], seg, -jnp.inf))
        cb = jnp.einsum('nigs,njgs->nijg', c, b)
        y = jnp.einsum('nijgh,njghp->nighp', cb[..., None] * decay * d[:, None], x)
        y = y + jnp.einsum('nigs,nghps->nighp', c, h) * jnp.exp(cs)[..., None]
        w_end = jnp.exp(cs[:, -1:] - cs) * d
        h = h * jnp.exp(cs[:, -1])[..., None, None] + jnp.einsum('njgs,njghp->nghps', b, x * w_end[..., None])
        return h, y

    h, ys = lax.scan(step, h0.astype(jnp.float32).reshape(n, G_B, hpg, HD_B, N_B), (xc, dc, bc, cc))
    y = jnp.moveaxis(ys, 0, 1).reshape(n, t, H_B, HD_B)
    return y, h.reshape(n, H_B, HD_B, N_B)


def chunk_mlp(u, v, ln_g, ln_b, w_s, b_s):
    n, t, c = v.shape
    ln = min(CHUNK_C, t)
    nc = t // ln
    vn = layer_norm(v, ln_g, ln_b)
    vr = vn.reshape(n, nc, ln, G_C, c // G_C)
    ws = jnp.where(jnp.tril(jnp.ones((ln, ln), bool)), w_s[:, :ln, :ln], 0.0)
    s = jnp.einsum('gij,ncjgd->ncigd', ws, vr) + b_s[:, :ln].T[None, None, :, :, None]
    return u * s.reshape(n, t, c).astype(u.dtype), vn


def trunk_layer(x, pos0, kv_cmp_past, kv_slc_past, kv_win_prev, n_win_keep, h0, conv_b_prev, conv_f_prev, p, rel_bias):
    n, t, _ = x.shape
    xn = rms_norm(x, p['norm1'])
    proj = jnp.einsum('ntd,de->nte', xn, p['w_in'])
    split_at = np.cumsum(SPLIT_SIZES)[:-1].tolist()
    q, kvc, kvs, kvw, g_nsa, z, xbc, dt_raw, u, v, g_mrg = jnp.split(proj, split_at, axis=-1)
    kv_shape = (n, t, 2, KV_A, HD_A)
    kvc = kvc.reshape(kv_shape)
    kvs = kvs.reshape(kv_shape)
    kvw = kvw.reshape(kv_shape)
    kvw_all = jnp.concatenate([kv_win_prev.astype(kvw.dtype), kvw], axis=1)
    o_a = nsa_attention(q.reshape(n, t, H_A, HD_A), jax.nn.sigmoid(g_nsa).reshape(n, t, H_A, 3),
                        jnp.concatenate([kv_cmp_past.astype(kvc.dtype), kvc], axis=1),
                        jnp.concatenate([kv_slc_past.astype(kvs.dtype), kvs], axis=1),
                        kvw_all, pos0, p['cmp_pe'], p['cmp_w1'], p['cmp_w2'], rel_bias)
    xbc_c, conv_b_new = causal_dwconv(xbc, conv_b_prev, p['conv_b_w'], p['conv_b_bias'])
    xbc_c = jax.nn.silu(xbc_c)
    xs_b, bm, cm = jnp.split(xbc_c, [D_INNER_B, D_INNER_B + G_B * N_B], axis=-1)
    dt = jax.nn.softplus(dt_raw.astype(jnp.float32) + p['dt_bias'].astype(jnp.float32))
    a = -jnp.exp(p['a_log'].astype(jnp.float32))
    xh = xs_b.reshape(n, t, H_B, HD_B)
    y_ssm, h_new = ssd_scan(xh, dt, a, bm.reshape(n, t, G_B, N_B), cm.reshape(n, t, G_B, N_B), h0)
    y_ssm = (y_ssm + p['d_skip'].astype(jnp.float32)[:, None] * xh.astype(jnp.float32)).reshape(n, t, D_INNER_B)
    o_b = group_rms_norm(y_ssm.astype(x.dtype) * jax.nn.silu(z), p['ssm_norm'], G_B)
    o_c, v_rows = chunk_mlp(jax.nn.gelu(u), jax.nn.gelu(v), p['v_ln_g'], p['v_ln_b'], p['w_spatial'], p['b_spatial'])
    gm = jax.nn.sigmoid(g_mrg).reshape(n, t, 3, D_MODEL)
    merged = gm[:, :, 0] * (o_a @ p['w_br_a']) + gm[:, :, 1] * (o_b @ p['w_br_b']) + gm[:, :, 2] * (o_c @ p['w_br_c'])
    x = x + merged @ p['w_out']
    xn2 = rms_norm(x, p['norm2'])
    a_ff, b_ff = jnp.split(xn2 @ p['w_ff_in'], 2, axis=-1)
    a_ff, conv_f_new = causal_dwconv(a_ff, conv_f_prev, p['ff_conv_w'], p['ff_conv_b'])
    x = x + (jax.nn.gelu(a_ff) * b_ff) @ p['w_ff_out']
    return x, kvc, kvs, kvw_all[:, -n_win_keep:], h_new, conv_b_new, conv_f_new, v_rows


def setup_inputs(seed: int = 0) -> dict:
    key = jax.random.key(seed)
    ks = iter(jax.random.split(key, 48))
    f32 = jnp.float32

    def nrm(shape, scale):
        return jax.random.normal(next(ks), shape, f32) * scale

    n_pages = PAST_LEN // PAGE_SIZE
    n_used = DEC_BATCH * n_pages
    n_pool = n_used + n_used // 4
    w_buf = min(WINDOW, PAST_LEN)
    x_prompt = nrm((BATCH, SEQ, D_MODEL), 1.0)
    x_sample = nrm((DEC_BATCH, DEC_SEQ, D_MODEL), 1.0)
    cache_cmp_kv = nrm((DEPTH, n_pool, PAGE_SIZE, 2, KV_A, HD_A), 1.0)
    cache_slc_kv = nrm((DEPTH, n_pool, PAGE_SIZE, 2, KV_A, HD_A), 1.0)
    cache_win_kv = nrm((DEPTH, DEC_BATCH, w_buf, 2, KV_A, HD_A), 1.0)
    state_ssm = nrm((DEPTH, DEC_BATCH, H_B, HD_B, N_B), 0.1)
    state_conv_ssm = nrm((DEPTH, DEC_BATCH, CONV_B - 1, XBC_DIM), 1.0)
    state_conv_ffn = nrm((DEPTH, DEC_BATCH, CONV_F - 1, D_FF), 1.0)
    page_table = jax.random.permutation(next(ks), n_pool)[:n_used].reshape(DEC_BATCH, n_pages).astype(jnp.int32)
    rel_bias = nrm((N_BUCKETS, H_A), 0.5)
    norm1 = 1.0 + nrm((DEPTH, D_MODEL), 0.02)
    w_in = nrm((DEPTH, D_MODEL, IN_DIM), D_MODEL ** -0.5)
    cmp_pe = nrm((DEPTH, CMP_BLK, 2, HD_A), 0.1)
    cmp_w1 = nrm((DEPTH, 2, CMP_BLK, HD_A, HD_A), (CMP_BLK * HD_A) ** -0.5)
    cmp_w2 = nrm((DEPTH, 2, HD_A, HD_A), HD_A ** -0.5)
    conv_b_w = nrm((DEPTH, CONV_B, XBC_DIM), CONV_B ** -0.5)
    conv_b_bias = nrm((DEPTH, XBC_DIM), 0.02)
    dt0 = jnp.exp(jax.random.uniform(next(ks), (DEPTH, H_B), f32, math.log(1e-3), math.log(1e-1)))
    dt_bias = dt0 + jnp.log(-jnp.expm1(-dt0))
    a_log = jnp.log(jax.random.uniform(next(ks), (DEPTH, H_B), f32, 1.0, 16.0))
    d_skip = 1.0 + nrm((DEPTH, H_B), 0.02)
    ssm_norm = 1.0 + nrm((DEPTH, D_INNER_B), 0.02)
    v_ln_g = 1.0 + nrm((DEPTH, D_C), 0.02)
    v_ln_b = nrm((DEPTH, D_C), 0.02)
    w_spatial = nrm((DEPTH, G_C, CHUNK_C, CHUNK_C), 0.5 * CHUNK_C ** -0.5)
    b_spatial = 1.0 + nrm((DEPTH, G_C, CHUNK_C), 0.02)
    w_br_a = nrm((DEPTH, H_A * HD_A, D_MODEL), (H_A * HD_A) ** -0.5)
    w_br_b = nrm((DEPTH, D_INNER_B, D_MODEL), D_INNER_B ** -0.5)
    w_br_c = nrm((DEPTH, D_C, D_MODEL), D_C ** -0.5)
    w_out = nrm((DEPTH, D_MODEL, D_MODEL), D_MODEL ** -0.5)
    norm2 = 1.0 + nrm((DEPTH, D_MODEL), 0.02)
    w_ff_in = nrm((DEPTH, D_MODEL, 2 * D_FF), D_MODEL ** -0.5)
    ff_conv_w = nrm((DEPTH, CONV_F, D_FF), CONV_F ** -0.5)
    ff_conv_b = nrm((DEPTH, D_FF), 0.02)
    w_ff_out = nrm((DEPTH, D_FF, D_MODEL), D_FF ** -0.5)
    final_norm = 1.0 + nrm((D_MODEL,), 0.02)
    return {'x_prompt': x_prompt, 'x_sample': x_sample, 'cache_cmp_kv': cache_cmp_kv, 'cache_slc_kv': cache_slc_kv,
            'cache_win_kv': cache_win_kv, 'state_ssm': state_ssm, 'state_conv_ssm': state_conv_ssm,
            'state_conv_ffn': state_conv_ffn, 'page_table': page_table, 'rel_bias': rel_bias, 'norm1': norm1,
            'w_in': w_in, 'cmp_pe': cmp_pe, 'cmp_w1': cmp_w1, 'cmp_w2': cmp_w2, 'conv_b_w': conv_b_w,
            'conv_b_bias': conv_b_bias, 'dt_bias': dt_bias, 'a_log': a_log, 'd_skip': d_skip, 'ssm_norm': ssm_norm,
            'v_ln_g': v_ln_g, 'v_ln_b': v_ln_b, 'w_spatial': w_spatial, 'b_spatial': b_spatial, 'w_br_a': w_br_a,
            'w_br_b': w_br_b, 'w_br_c': w_br_c, 'w_out': w_out, 'norm2': norm2, 'w_ff_in': w_ff_in,
            'ff_conv_w': ff_conv_w, 'ff_conv_b': ff_conv_b, 'w_ff_out': w_ff_out, 'final_norm': final_norm}


def reference(x_prompt, x_sample, cache_cmp_kv, cache_slc_kv, cache_win_kv, state_ssm, state_conv_ssm, state_conv_ffn,
              page_table, rel_bias, norm1, w_in, cmp_pe, cmp_w1, cmp_w2, conv_b_w, conv_b_bias, dt_bias, a_log, d_skip,
              ssm_norm, v_ln_g, v_ln_b, w_spatial, b_spatial, w_br_a, w_br_b, w_br_c, w_out, norm2, w_ff_in,
              ff_conv_w, ff_conv_b, w_ff_out, final_norm):
    bp, seq = x_prompt.shape[:2]
    past = page_table.shape[1] * cache_cmp_kv.shape[2]
    w_buf = cache_win_kv.shape[2]
    dtp = x_prompt.dtype
    zero_kv = jnp.zeros((bp, 0, 2, KV_A, HD_A), dtp)
    zero_win = jnp.zeros((bp, WINDOW, 2, KV_A, HD_A), dtp)
    zero_h = jnp.zeros((bp, H_B, HD_B, N_B), jnp.float32)
    zero_cb = jnp.zeros((bp, CONV_B - 1, XBC_DIM), dtp)
    zero_cf = jnp.zeros((bp, CONV_F - 1, D_FF), dtp)
    xp, xs = x_prompt, x_sample
    p_cmp, p_slc, p_win, p_ssm, p_cvb, p_cvf = [], [], [], [], [], []
    s_cmp, s_slc, s_win, s_ssm, s_cvb, s_cvf, s_v = [], [], [], [], [], [], []
    for l in range(DEPTH):
        p = {'norm1': norm1[l], 'w_in': w_in[l], 'cmp_pe': cmp_pe[l], 'cmp_w1': cmp_w1[l], 'cmp_w2': cmp_w2[l],
             'conv_b_w': conv_b_w[l], 'conv_b_bias': conv_b_bias[l], 'dt_bias': dt_bias[l], 'a_log': a_log[l],
             'd_skip': d_skip[l], 'ssm_norm': ssm_norm[l], 'v_ln_g': v_ln_g[l], 'v_ln_b': v_ln_b[l],
             'w_spatial': w_spatial[l], 'b_spatial': b_spatial[l], 'w_br_a': w_br_a[l], 'w_br_b': w_br_b[l],
             'w_br_c': w_br_c[l], 'w_out': w_out[l], 'norm2': norm2[l], 'w_ff_in': w_ff_in[l],
             'ff_conv_w': ff_conv_w[l], 'ff_conv_b': ff_conv_b[l], 'w_ff_out': w_ff_out[l]}
        xp, kc, kslc, kw, hn, cb, cf, _ = trunk_layer(xp, 0, zero_kv, zero_kv, zero_win, min(WINDOW, seq),
                                                      zero_h, zero_cb, zero_cf, p, rel_bias)
        p_cmp.append(kc); p_slc.append(kslc); p_win.append(kw); p_ssm.append(hn); p_cvb.append(cb); p_cvf.append(cf)
        past_c = gather_pages(cache_cmp_kv[l], page_table)
        past_s = gather_pages(cache_slc_kv[l], page_table)
        win_prev = jnp.pad(cache_win_kv[l], ((0, 0), (WINDOW - w_buf, 0), (0, 0), (0, 0), (0, 0)))
        xs, kc, kslc, kw, hn, cb, cf, vr = trunk_layer(xs, past, past_c, past_s, win_prev, w_buf,
                                                       state_ssm[l], state_conv_ssm[l], state_conv_ffn[l], p, rel_bias)
        s_cmp.append(kc); s_slc.append(kslc); s_win.append(kw); s_ssm.append(hn); s_cvb.append(cb); s_cvf.append(cf)
        s_v.append(vr)
    y_prompt = rms_norm(xp, final_norm)
    y_sample = rms_norm(xs, final_norm)
    return (y_prompt, y_sample,
            jnp.stack(p_cmp), jnp.stack(p_slc), jnp.stack(p_win), jnp.stack(p_ssm), jnp.stack(p_cvb), jnp.stack(p_cvf),
            jnp.stack(s_cmp), jnp.stack(s_slc), jnp.stack(s_win), jnp.stack(s_ssm), jnp.stack(s_cvb), jnp.stack(s_cvf),
            jnp.stack(s_v))
```

```python
import functools
import math

import numpy as np
import jax
import jax.numpy as jnp
from jax import lax
from jax.experimental import pallas as pl
from jax.experimental.pallas import tpu as pltpu

F32 = jnp.float32
BF16 = jnp.bfloat16

D_MODEL = 2048
DEPTH = 4
PAGE_SIZE = 128
H_A = 16
HD_A = 128
KV_A = 2
HPG_A = H_A // KV_A
CMP_BLK = 32
SLC_BLK = 64
N_SEL = 16
WINDOW = 512
FORCE_BONUS = 1.0e4
NEG_INF = -1.0e30
N_BUCKETS = 32
MAX_DIST = 128
D_INNER_B = 2048
HD_B = 64
H_B = D_INNER_B // HD_B
G_B = 4
N_B = 128
CONV_B = 4
SSD_CHUNK = 128
XBC_DIM = D_INNER_B + 2 * G_B * N_B
D_C = 2048
G_C = 8
CHUNK_C = 128
D_FF = 5632
CONV_F = 3
NORM_EPS = 1e-6

C_Q = 0
C_Z = 2048
C_U = 4096
C_V = 6144
C_GM = 8192
C_XBC = 14336
C_KVC = 17408
C_KVS = 17920
C_KVW = 18432
C_GN = 18944
C_DT = 19072
IN_PAD = 19200
_SEGS = ((H_A * HD_A, C_Q), (512, C_KVC), (512, C_KVS), (512, C_KVW), (3 * H_A, C_GN), (D_INNER_B, C_Z),
         (XBC_DIM, C_XBC), (H_B, C_DT), (D_C, C_U), (D_C, C_V), (3 * D_MODEL, C_GM))

VMEM_LIMIT = 56 * 1024 * 1024
SAMPLE_T = 16


def _cp(*sem):
    return pltpu.CompilerParams(dimension_semantics=sem, vmem_limit_bytes=VMEM_LIMIT)


def _gelu(x):
    return jax.nn.gelu(x)


def _t5_thresholds():
    n = np.arange(0, 4 * MAX_DIST)
    max_exact = N_BUCKETS // 2
    nf = np.maximum(n, 1).astype(np.float64)
    large = max_exact + (np.log(nf / max_exact) / math.log(MAX_DIST / max_exact) * (N_BUCKETS - max_exact)).astype(np.int64)
    b = np.where(n < max_exact, n, np.minimum(large, N_BUCKETS - 1))
    return [int(np.argmax(b >= k)) for k in range(1, N_BUCKETS)]


_T5_THR = _t5_thresholds()


def _t5_bias(dist, rb_ref, head):
    out = jnp.full(dist.shape, rb_ref[0, head], F32)
    for k, thr in enumerate(_T5_THR):
        out = jnp.where(dist >= thr, rb_ref[k + 1, head], out)
    return out


def _rmsmm_kernel(x_ref, g_ref, w_ref, o_ref, xn_ref):
    @pl.when(pl.program_id(1) == 0)
    def _():
        x = x_ref[...]
        y = x * lax.rsqrt(jnp.mean(x * x, axis=-1, keepdims=True) + NORM_EPS)
        xn_ref[...] = (y * g_ref[...]).astype(BF16)

    o_ref[...] = jnp.dot(xn_ref[...], w_ref[...], preferred_element_type=F32)


def rms_matmul(x, g, w, tm, tn):
    r, d = x.shape
    n = w.shape[1]
    return pl.pallas_call(
        _rmsmm_kernel,
        grid=(r // tm, n // tn),
        in_specs=[pl.BlockSpec((tm, d), lambda i, j: (i, 0)),
                  pl.BlockSpec((1, d), lambda i, j: (0, 0)),
                  pl.BlockSpec((d, tn), lambda i, j: (0, j))],
        out_specs=pl.BlockSpec((tm, tn), lambda i, j: (i, j)),
        out_shape=jax.ShapeDtypeStruct((r, n), F32),
        scratch_shapes=[pltpu.VMEM((tm, d), BF16)],
        compiler_params=_cp("arbitrary", "arbitrary"),
        name="rms_matmul",
    )(x, g.reshape(1, d), w)


def _merge_kernel(oa_ref, ob_ref, oc_ref, wa_ref, wb_ref, wc_ref, ga_ref, gb_ref, gc_ref, o_ref):
    acc = jax.nn.sigmoid(ga_ref[...]) * jnp.dot(oa_ref[...], wa_ref[...], preferred_element_type=F32)
    acc = acc + jax.nn.sigmoid(gb_ref[...]) * jnp.dot(ob_ref[...], wb_ref[...], preferred_element_type=F32)
    acc = acc + jax.nn.sigmoid(gc_ref[...]) * jnp.dot(oc_ref[...], wc_ref[...], preferred_element_type=F32)
    o_ref[...] = acc.astype(BF16)


def gated_merge(o_a, o_b, o_c, w_a, w_b, w_c, proj, tm, tn):
    r, d = o_a.shape
    n = w_a.shape[1]
    gblk = C_GM // tn
    per = D_MODEL // tn
    lhs = pl.BlockSpec((tm, d), lambda i, j: (i, 0))
    wsp = pl.BlockSpec((d, tn), lambda i, j: (0, j))

    def gate(b):
        return pl.BlockSpec((tm, tn), lambda i, j: (i, gblk + b * per + j))

    return pl.pallas_call(
        _merge_kernel,
        grid=(r // tm, n // tn),
        in_specs=[lhs, lhs, lhs, wsp, wsp, wsp, gate(0), gate(1), gate(2)],
        out_specs=pl.BlockSpec((tm, tn), lambda i, j: (i, j)),
        out_shape=jax.ShapeDtypeStruct((r, n), BF16),
        compiler_params=_cp("arbitrary", "arbitrary"),
        name="gated_merge",
    )(o_a, o_b, o_c, w_a, w_b, w_c, proj, proj, proj)


def _mmres_kernel(a_ref, w_ref, r_ref, o_ref):
    o_ref[...] = r_ref[...] + jnp.dot(a_ref[...], w_ref[...], preferred_element_type=F32)


def matmul_residual(a, w, res, tm, tn):
    r, k = a.shape
    n = w.shape[1]
    return pl.pallas_call(
        _mmres_kernel,
        grid=(r // tm, n // tn),
        in_specs=[pl.BlockSpec((tm, k), lambda i, j: (i, 0)),
                  pl.BlockSpec((k, tn), lambda i, j: (0, j)),
                  pl.BlockSpec((tm, tn), lambda i, j: (i, j))],
        out_specs=pl.BlockSpec((tm, tn), lambda i, j: (i, j)),
        out_shape=jax.ShapeDtypeStruct((r, n), F32),
        compiler_params=_cp("arbitrary", "arbitrary"),
        name="matmul_residual",
    )(a, w, res)


def _ffn_out_kernel(tiles_per_seq, a_ref, b_ref, prev_ref, cw_ref, cb_ref, w_ref, r_ref, o_ref,
                    ext_ref, carry_ref, acc_ref):
    i = pl.program_id(0)
    k = pl.program_id(1)
    tm = a_ref.shape[0]

    @pl.when(i % tiles_per_seq == 0)
    def _():
        ext_ref[0:8, :] = prev_ref[0]

    @pl.when(i % tiles_per_seq != 0)
    def _():
        ext_ref[0:8, :] = carry_ref[k]

    a = a_ref[...]
    ext_ref[8:8 + tm, :] = a
    carry_ref[k] = a_ref[tm - 8:tm, :]
    cw = cw_ref[...]
    y = cb_ref[...] + ext_ref[6:6 + tm, :] * cw[0:1]
    y = y + ext_ref[7:7 + tm, :] * cw[1:2]
    y = y + a * cw[2:3]
    h = (_gelu(y) * b_ref[...]).astype(BF16)
    part = jnp.dot(h, w_ref[...], preferred_element_type=F32)

    @pl.when(k == 0)
    def _():
        acc_ref[...] = part

    @pl.when(k != 0)
    def _():
        acc_ref[...] += part

    @pl.when(k == pl.num_programs(1) - 1)
    def _():
        o_ref[...] = r_ref[...] + acc_ref[...]


def ffn_out(h, prev, conv_w, conv_b, w, res, seq_len, tm, tk):
    r = h.shape[0]
    n = w.shape[1]
    nk = D_FF // tk
    kernel = functools.partial(_ffn_out_kernel, seq_len // tm)
    return pl.pallas_call(
        kernel,
        grid=(r // tm, nk),
        in_specs=[pl.BlockSpec((tm, tk), lambda i, k: (i, k)),
                  pl.BlockSpec((tm, tk), lambda i, k: (i, nk + k)),
                  pl.BlockSpec((1, 8, tk), lambda i, k: (i * tm // seq_len, 0, k)),
                  pl.BlockSpec((8, tk), lambda i, k: (0, k)),
                  pl.BlockSpec((1, tk), lambda i, k: (0, k)),
                  pl.BlockSpec((tk, n), lambda i, k: (k, 0)),
                  pl.BlockSpec((tm, n), lambda i, k: (i, 0))],
        out_specs=pl.BlockSpec((tm, n), lambda i, k: (i, 0)),
        out_shape=jax.ShapeDtypeStruct((r, n), F32),
        scratch_shapes=[pltpu.VMEM((tm + 8, tk), F32), pltpu.VMEM((nk, 8, tk), F32), pltpu.VMEM((tm, n), F32)],
        compiler_params=_cp("arbitrary", "arbitrary"),
        name="ffn_out",
    )(h, h, prev, conv_w, conv_b.reshape(1, D_FF), w, res)


def _rms_kernel(x_ref, g_ref, o_ref):
    x = x_ref[...]
    o_ref[...] = x * lax.rsqrt(jnp.mean(x * x, axis=-1, keepdims=True) + NORM_EPS) * g_ref[...]


def rms_norm_rows(x, g, tm):
    r, d = x.shape
    return pl.pallas_call(
        _rms_kernel,
        grid=(r // tm,),
        in_specs=[pl.BlockSpec((tm, d), lambda i: (i, 0)), pl.BlockSpec((1, d), lambda i: (0, 0))],
        out_specs=pl.BlockSpec((tm, d), lambda i: (i, 0)),
        out_shape=jax.ShapeDtypeStruct((r, d), F32),
        compiler_params=_cp("arbitrary"),
        name="final_rms_norm",
    )(x, g.reshape(1, d))


def _ssd_kernel(rows_in, valid_len, x_ref, b_ref, c_ref, z_ref, dt_ref, prev_ref, cw_ref, cb_ref, dtb_ref, alog_ref,
                dskip_ref, norm_ref, h0_ref, tril_ref, o_ref, ht_ref, ext_ref, st_ref, tt_ref, y_ref, xw_ref):
    L = SSD_CHUNK
    c = pl.program_id(1)

    @pl.when(c == 0)
    def _():
        ext_ref[0:8, :] = prev_ref[0]
        ht_ref[...] = h0_ref[...]

    @pl.when(c != 0)
    def _():
        ext_ref[0:8, :] = ext_ref[L:L + 8, :]

    if rows_in < L:
        ext_ref[8 + rows_in:8 + L, :] = jnp.zeros((L - rows_in, XBC_DIM), F32)
    ext_ref[8:8 + rows_in, 0:D_INNER_B] = x_ref[...]
    ext_ref[8:8 + rows_in, D_INNER_B:D_INNER_B + G_B * N_B] = b_ref[...]
    ext_ref[8:8 + rows_in, D_INNER_B + G_B * N_B:XBC_DIM] = c_ref[...]

    cw = cw_ref[...]
    y = cb_ref[...] + ext_ref[5:5 + L, :] * cw[0:1]
    y = y + ext_ref[6:6 + L, :] * cw[1:2]
    y = y + ext_ref[7:7 + L, :] * cw[2:3]
    y = y + ext_ref[8:8 + L, :] * cw[3:4]
    xbc = y * jax.nn.sigmoid(y)

    if rows_in < L:
        dt_raw = jnp.concatenate([dt_ref[...], jnp.zeros((L - rows_in, 128), F32)], axis=0)
    else:
        dt_raw = dt_ref[...]
    d_all = jax.nn.softplus(dt_raw + dtb_ref[...])
    if valid_len < L:
        row = lax.broadcasted_iota(jnp.int32, (L, 128), 0)
        d_all = jnp.where(row < valid_len, d_all, 0.0)
    a_row = -jnp.exp(alog_ref[...])
    tril = tril_ref[...]
    cs = jnp.dot(tril, d_all * a_row, preferred_element_type=F32, precision=lax.Precision.HIGHEST)
    st_ref[0] = cs
    st_ref[1] = jnp.exp(cs)
    st_ref[2] = jnp.exp(cs[L - 1:L, :] - cs) * d_all
    tt_ref[0] = cs.T
    tt_ref[1] = d_all.T
    lower = tril > 0.5

    for g in range(G_B):
        b_g = xbc[:, D_INNER_B + g * N_B:D_INNER_B + (g + 1) * N_B]
        c_g = xbc[:, D_INNER_B + G_B * N_B + g * N_B:D_INNER_B + G_B * N_B + (g + 1) * N_B]
        cgb = c_g.astype(BF16)
        cb = lax.dot_general(cgb, b_g.astype(BF16), (((1,), (1,)), ((), ())), preferred_element_type=F32)
        bt = b_g.T.astype(BF16)
        gs = slice(g * 512, (g + 1) * 512)
        hg = ht_ref[0, :, gs]
        yoff = jnp.dot(cgb, hg.astype(BF16), preferred_element_type=F32)
        for hh in range(8):
            h = g * 8 + hh
            hs = slice(h * HD_B, (h + 1) * HD_B)
            x_h = xbc[:, hs]
            seg = st_ref[0, :, h:h + 1] - tt_ref[0, h:h + 1, :]
            decay = jnp.exp(jnp.where(lower, seg, NEG_INF))
            m = (cb * decay * tt_ref[1, h:h + 1, :]).astype(BF16)
            y_h = jnp.dot(m, x_h.astype(BF16), preferred_element_type=F32)
            y_h = y_h + yoff[:, hh * HD_B:(hh + 1) * HD_B] * st_ref[1, :, h:h + 1]
            y_ref[:, hs] = y_h + dskip_ref[:, hs] * x_h
            xw_ref[:, hh * HD_B:(hh + 1) * HD_B] = (x_h * st_ref[2, :, h:h + 1]).astype(BF16)
        upd = jnp.dot(bt, xw_ref[...], preferred_element_type=F32)
        for hh in range(8):
            h = g * 8 + hh
            hs = slice(h * HD_B, (h + 1) * HD_B)
            ht_ref[0, :, hs] = ht_ref[0, :, hs] * st_ref[1, L - 1:L, h:h + 1] + upd[:, hh * HD_B:(hh + 1) * HD_B]

    if rows_in < L:
        z = jnp.concatenate([z_ref[...], jnp.zeros((L - rows_in, D_INNER_B), F32)], axis=0)
    else:
        z = z_ref[...]
    yz = y_ref[...] * (z * jax.nn.sigmoid(z))
    for g in range(G_B):
        gs = slice(g * 512, (g + 1) * 512)
        v = yz[:, gs]
        out = v * lax.rsqrt(jnp.mean(v * v, axis=-1, keepdims=True) + NORM_EPS) * norm_ref[:, gs]
        o_ref[:, gs] = out[0:rows_in].astype(BF16)


def ssd_mixer(proj, prev, conv_w, conv_b, dt_bias, a_log, d_skip, ssm_norm, h0t, nseq, seq_rows, valid_len):
    rows_in = min(seq_rows, SSD_CHUNK)
    nchunk = seq_rows // rows_in
    r = nseq * seq_rows
    tril = jnp.asarray(np.tril(np.ones((SSD_CHUNK, SSD_CHUNK), np.float32)))
    pad128 = lambda v: jnp.pad(v.astype(F32), (0, 128 - v.shape[0])).reshape(1, 128)
    kernel = functools.partial(_ssd_kernel, rows_in, min(valid_len, SSD_CHUNK))
    rowblk = lambda n, c: n * nchunk + c
    const = lambda n, c: (0, 0)
    return pl.pallas_call(
        kernel,
        grid=(nseq, nchunk),
        in_specs=[pl.BlockSpec((rows_in, D_INNER_B), lambda n, c: (rowblk(n, c), C_XBC // 2048)),
                  pl.BlockSpec((rows_in, 512), lambda n, c: (rowblk(n, c), (C_XBC + 2048) // 512)),
                  pl.BlockSpec((rows_in, 512), lambda n, c: (rowblk(n, c), (C_XBC + 2560) // 512)),
                  pl.BlockSpec((rows_in, D_INNER_B), lambda n, c: (rowblk(n, c), C_Z // 2048)),
                  pl.BlockSpec((rows_in, 128), lambda n, c: (rowblk(n, c), C_DT // 128)),
                  pl.BlockSpec((1, 8, XBC_DIM), lambda n, c: (n, 0, 0)),
                  pl.BlockSpec((8, XBC_DIM), const),
                  pl.BlockSpec((1, XBC_DIM), const),
                  pl.BlockSpec((1, 128), const),
                  pl.BlockSpec((1, 128), const),
                  pl.BlockSpec((1, D_INNER_B), const),
                  pl.BlockSpec((1, D_INNER_B), const),
                  pl.BlockSpec((1, N_B, D_INNER_B), lambda n, c: (n, 0, 0)),
                  pl.BlockSpec((SSD_CHUNK, SSD_CHUNK), const)],
        out_specs=[pl.BlockSpec((rows_in, D_INNER_B), lambda n, c: (rowblk(n, c), 0)),
                   pl.BlockSpec((1, N_B, D_INNER_B), lambda n, c: (n, 0, 0))],
        out_shape=[jax.ShapeDtypeStruct((r, D_INNER_B), BF16),
                   jax.ShapeDtypeStruct((nseq, N_B, D_INNER_B), F32)],
        scratch_shapes=[pltpu.VMEM((SSD_CHUNK + 8, XBC_DIM), F32),
                        pltpu.VMEM((3, SSD_CHUNK, 128), F32),
                        pltpu.VMEM((2, 128, SSD_CHUNK), F32),
                        pltpu.VMEM((SSD_CHUNK, D_INNER_B), F32),
                        pltpu.VMEM((SSD_CHUNK, 512), BF16)],
        compiler_params=_cp("arbitrary", "arbitrary"),
        name="ssd_mixer",
    )(proj, proj, proj, proj, proj, prev, conv_w, conv_b.reshape(1, XBC_DIM), pad128(dt_bias), pad128(a_log),
      jnp.repeat(d_skip.astype(F32), HD_B).reshape(1, D_INNER_B), ssm_norm.reshape(1, D_INNER_B), h0t, tril)


def _cmlp_kernel(rows_in, u_ref, v_ref, lg_ref, lb_ref, ws_ref, bst_ref, o_ref, vn_ref):
    L = CHUNK_C
    v = _gelu(v_ref[...])
    mu = jnp.mean(v, axis=-1, keepdims=True)
    vc = v - mu
    var = jnp.mean(vc * vc, axis=-1, keepdims=True)
    vn = vc * lax.rsqrt(var + NORM_EPS) * lg_ref[...] + lb_ref[...]
    vn_ref[...] = vn
    u = _gelu(u_ref[...])
    if rows_in < L:
        vn = jnp.concatenate([vn, jnp.zeros((L - rows_in, D_C), F32)], axis=0)
    vnb = vn.astype(BF16)
    row = lax.broadcasted_iota(jnp.int32, (L, L), 0)
    col = lax.broadcasted_iota(jnp.int32, (L, L), 1)
    gw = D_C // G_C
    for g in range(G_C):
        w = jnp.where(col <= row, ws_ref[g], 0.0).astype(BF16)
        s = jnp.dot(w, vnb[:, g * gw:(g + 1) * gw], preferred_element_type=F32) + bst_ref[:, g:g + 1]
        o_ref[:, g * gw:(g + 1) * gw] = (u[:, g * gw:(g + 1) * gw] * s[0:rows_in]).astype(BF16)


def chunk_mlp_mixer(proj, ln_g, ln_b, w_s, b_s, seq_rows):
    r = proj.shape[0]
    rows_in = min(seq_rows, CHUNK_C)
    kernel = functools.partial(_cmlp_kernel, rows_in)
    bst = jnp.pad(b_s.T.astype(F32), ((0, 0), (0, 128 - G_C)))
    return pl.pallas_call(
        kernel,
        grid=(r // rows_in,),
        in_specs=[pl.BlockSpec((rows_in, D_C), lambda i: (i, C_U // 2048)),
                  pl.BlockSpec((rows_in, D_C), lambda i: (i, C_V // 2048)),
                  pl.BlockSpec((1, D_C), lambda i: (0, 0)),
                  pl.BlockSpec((1, D_C), lambda i: (0, 0)),
                  pl.BlockSpec((G_C, CHUNK_C, CHUNK_C), lambda i: (0, 0, 0)),
                  pl.BlockSpec((CHUNK_C, 128), lambda i: (0, 0))],
        out_specs=[pl.BlockSpec((rows_in, D_C), lambda i: (i, 0)),
                   pl.BlockSpec((rows_in, D_C), lambda i: (i, 0))],
        out_shape=[jax.ShapeDtypeStruct((r, D_C), BF16), jax.ShapeDtypeStruct((r, D_C), F32)],
        compiler_params=_cp("arbitrary"),
        name="chunk_mlp_mixer",
    )(proj, proj, ln_g.reshape(1, D_C), ln_b.reshape(1, D_C), w_s, bst)


ROW_W = 2 * KV_A * HD_A
SCALE = HD_A ** -0.5
MASK_BIG = 1.0e30
M_INIT = -2.0e30


def _gather_kernel(pt_ref, cmp_ref, slc_ref, oc_ref, os_ref, sem):
    l = pl.program_id(0)
    n = pl.program_id(1)
    npages = pt_ref.shape[1]

    def copies(p):
        page = pt_ref[n, p]
        return (pltpu.make_async_copy(cmp_ref.at[l, page], oc_ref.at[l, n, p], sem.at[0]),
                pltpu.make_async_copy(slc_ref.at[l, page], os_ref.at[l, n, p], sem.at[1]))

    def start(p, carry):
        for cp in copies(p):
            cp.start()
        return carry

    def wait(p, carry):
        for cp in copies(p):
            cp.wait()
        return carry

    lax.fori_loop(0, npages, start, 0)
    lax.fori_loop(0, npages, wait, 0)


def gather_pages(page_table, cache_cmp, cache_slc):
    depth = cache_cmp.shape[0]
    nseq, npages = page_table.shape
    shape = jax.ShapeDtypeStruct((depth, nseq, npages, PAGE_SIZE, ROW_W), F32)
    anyspec = pl.BlockSpec(memory_space=pl.ANY)
    return pl.pallas_call(
        _gather_kernel,
        grid_spec=pltpu.PrefetchScalarGridSpec(
            num_scalar_prefetch=1, grid=(depth, nseq),
            in_specs=[anyspec, anyspec], out_specs=[anyspec, anyspec],
            scratch_shapes=[pltpu.SemaphoreType.DMA((2,))]),
        out_shape=[shape, shape],
        compiler_params=_cp("arbitrary", "arbitrary"),
        name="gather_pages",
    )(page_table, cache_cmp, cache_slc)


CMP_ROWS = 4096


def _cmpmlp_kernel(x00_ref, x01_ref, x10_ref, x11_ref, pe_ref, w1_ref, w2_ref, kc_ref, vc_ref):
    x_refs = ((x00_ref, x01_ref), (x10_ref, x11_ref))
    nblk = x00_ref.shape[1] // CMP_BLK
    for c in range(2):
        def body(s, acc):
            xs = [x_refs[c][g][0, pl.ds(s, nblk, stride=CMP_BLK), :] for g in range(KV_A)]
            x = jnp.concatenate(xs, axis=0) + pe_ref[s * 2 + c]
            return acc + jnp.dot(x.astype(BF16), w1_ref[c, s], preferred_element_type=F32)

        acc = lax.fori_loop(0, CMP_BLK, body, jnp.zeros((KV_A * nblk, HD_A), F32))
        out = jnp.dot(_gelu(acc).astype(BF16), w2_ref[c], preferred_element_type=F32)
        dst = kc_ref if c == 0 else vc_ref
        for g in range(KV_A):
            dst[0, g] = out[g * nblk:(g + 1) * nblk].astype(BF16)


def compress_kv(src, lead, col_blk, nseq, seq_rows, pe, w1, w2):
    rows = min(CMP_ROWS, seq_rows)
    nchunk = seq_rows // rows
    nb = seq_rows // CMP_BLK
    shape = jax.ShapeDtypeStruct((nseq, KV_A, nb, HD_A), BF16)
    ospec = pl.BlockSpec((1, KV_A, rows // CMP_BLK, HD_A), lambda n, c: (n, 0, c, 0))

    def xspec(part):
        return pl.BlockSpec((1, rows, HD_A), lambda n, c: (lead, n * nchunk + c, col_blk * (ROW_W // HD_A) + part))

    return pl.pallas_call(
        _cmpmlp_kernel,
        grid=(nseq, nchunk),
        in_specs=[xspec(0), xspec(1), xspec(2), xspec(3),
                  pl.BlockSpec((CMP_BLK * 2, 1, HD_A), lambda n, c: (0, 0, 0)),
                  pl.BlockSpec((2, CMP_BLK, HD_A, HD_A), lambda n, c: (0, 0, 0, 0)),
                  pl.BlockSpec((2, HD_A, HD_A), lambda n, c: (0, 0, 0))],
        out_specs=[ospec, ospec],
        out_shape=[shape, shape],
        compiler_params=_cp("arbitrary", "arbitrary"),
        name="compress_kv",
    )(src, src, src, src, pe.reshape(CMP_BLK * 2, 1, HD_A), w1, w2)


def _cmpattn_kernel(q_pos0, ns, rb_ref, q_ref, kc_ref, vc_ref, pool_ref, o_ref, sel_ref):
    g = pl.program_id(1)
    qt = pl.program_id(2)
    tq = q_ref.shape[0]
    nb = kc_ref.shape[2]
    nsp = pool_ref.shape[0]
    q0 = q_pos0 + qt * tq
    qpos = q0 + lax.broadcasted_iota(jnp.int32, (tq, nb), 0)
    dist = qpos - (lax.broadcasted_iota(jnp.int32, (tq, nb), 1) * CMP_BLK + (CMP_BLK - 1))
    vis = dist >= 0
    anyvis = ((q0 + lax.broadcasted_iota(jnp.int32, (tq, 1), 0)) >= CMP_BLK - 1).astype(F32)
    kc = kc_ref[0, 0]
    vc = vc_ref[0, 0]
    psum = jnp.zeros((tq, nb), F32)
    for h in range(HPG_A):
        hs = slice(h * HD_A, (h + 1) * HD_A)
        lc = lax.dot_general(q_ref[:, hs].astype(BF16), kc, (((1,), (1,)), ((), ())), preferred_element_type=F32)
        lc = lc * SCALE + _t5_bias(dist, rb_ref, g * HPG_A + h)
        lc = jnp.where(vis, lc, NEG_INF)
        e = jnp.exp(lc - jnp.max(lc, axis=-1, keepdims=True))
        p = e / jnp.sum(e, axis=-1, keepdims=True) * anyvis
        o_ref[:, hs] = jnp.dot(p.astype(BF16), vc, preferred_element_type=F32)
        psum = psum + p

    tqp = max(tq, 128)
    if tq < tqp:
        psum = jnp.concatenate([psum, jnp.zeros((tqp - tq, nb), F32)], axis=0)
    imp = lax.dot_general(pool_ref[...], psum, (((1,), (1,)), ((), ())), preferred_element_type=F32,
                          precision=lax.Precision.HIGHEST)
    j = lax.broadcasted_iota(jnp.int32, (nsp, tqp), 0)
    cur = lax.shift_right_logical(q0 + lax.broadcasted_iota(jnp.int32, (nsp, tqp), 1), 6)
    forced = (j == 0) | (j == cur) | (j == cur - 1)
    score = jnp.where(j <= cur, imp + jnp.where(forced, FORCE_BONUS, 0.0), -1.0)
    score = jnp.where(j < ns, score, -2.0)
    sel = jnp.zeros((nsp, tqp), F32)
    for _ in range(N_SEL):
        top = jnp.max(score, axis=0, keepdims=True)
        idx = jnp.min(jnp.where(score == top, j, 1 << 30), axis=0, keepdims=True)
        hit = j == idx
        sel = jnp.where(hit, 1.0, sel)
        score = jnp.where(hit, -3.0, score)
    sel_ref[0] = sel.T[0:tq].astype(BF16)


def compressed_attention(proj, kc, vc, rel_bias, nseq, seq_rows, tq, q_pos0, ns):
    r = nseq * seq_rows
    nb = kc.shape[2]
    nsp = -(-ns // 128) * 128
    nqt = seq_rows // tq
    pool = np.zeros((nsp, nb), np.float32)
    pool[np.arange(nb) // (SLC_BLK // CMP_BLK), np.arange(nb)] = 1.0
    kernel = functools.partial(_cmpattn_kernel, q_pos0, ns)
    kvspec = pl.BlockSpec((1, 1, nb, HD_A), lambda n, g, t: (n, g, 0, 0))
    return pl.pallas_call(
        kernel,
        grid=(nseq, KV_A, nqt),
        in_specs=[pl.BlockSpec(memory_space=pltpu.SMEM),
                  pl.BlockSpec((tq, HPG_A * HD_A), lambda n, g, t: (n * nqt + t, g)),
                  kvspec, kvspec,
                  pl.BlockSpec((nsp, nb), lambda n, g, t: (0, 0))],
        out_specs=[pl.BlockSpec((tq, HPG_A * HD_A), lambda n, g, t: (n * nqt + t, g)),
                   pl.BlockSpec((1, tq, nsp), lambda n, g, t: (g, n * nqt + t, 0))],
        out_shape=[jax.ShapeDtypeStruct((r, H_A * HD_A), F32), jax.ShapeDtypeStruct((KV_A, r, nsp), BF16)],
        compiler_params=_cp("arbitrary", "arbitrary", "arbitrary"),
        name="compressed_attention",
    )(rel_bias, proj, kc, vc, jnp.asarray(pool))


def _softmax_step(s, v, m_ref, l_ref, acc_ref):
    m_prev = m_ref[...]
    m_new = jnp.maximum(m_prev, jnp.max(s, axis=-1, keepdims=True))
    alpha = jnp.exp(m_prev - m_new)
    p = jnp.exp(s - m_new)
    l_ref[...] = alpha * l_ref[...] + jnp.sum(p, axis=-1, keepdims=True)
    acc_ref[...] = alpha * acc_ref[...] + jnp.dot(p.astype(BF16), v, preferred_element_type=F32)
    m_ref[...] = m_new


def _softmax_init(m_ref, l_ref, acc_ref):
    m_ref[...] = jnp.full(m_ref.shape, M_INIT, F32)
    l_ref[...] = jnp.zeros(l_ref.shape, F32)
    acc_ref[...] = jnp.zeros(acc_ref.shape, F32)


def _qk(q, k):
    return lax.dot_general(q, k, (((1,), (1,)), ((), ())), preferred_element_type=F32) * SCALE


def _gate_cols(gates, g, h):
    cols = []
    for b in range(3):
        lo = 3 * h + b
        hi = 3 * (HPG_A + h) + b
        cols.append(jnp.where(g == 0, gates[:, lo:lo + 1], gates[:, hi:hi + 1]))
    return cols


PTQ = 128


def _prompt_attn_kernel(seq, rb_ref, q_ref, sel_ref, ks_ref, vs_ref, kw_ref, vw_ref, oc_ref, gn_ref, o_ref,
                        kaug_ref, vsb_ref, kwb_ref, vwb_ref, btab_ref, m_ref, l_ref, acc_ref, os_ref):
    g = pl.program_id(1)
    qt = pl.program_id(2)
    tq = PTQ

    @pl.when(qt == 0)
    def _():
        kaug_ref[:, 0:HD_A] = ks_ref[...].astype(BF16)
        row = lax.broadcasted_iota(jnp.int32, (seq, 128), 0)
        lane = lax.broadcasted_iota(jnp.int32, (seq, 128), 1)
        kaug_ref[:, HD_A:HD_A + 128] = jnp.where(lax.shift_right_logical(row, 6) == lane, 1.0, 0.0).astype(BF16)
        vsb_ref[...] = vs_ref[...].astype(BF16)
        kwb_ref[...] = kw_ref[...].astype(BF16)
        vwb_ref[...] = vw_ref[...].astype(BF16)
        qi = lax.broadcasted_iota(jnp.int32, (tq, tq), 0)
        kj = lax.broadcasted_iota(jnp.int32, (tq, tq), 1)
        for h in range(HPG_A):
            head = g * HPG_A + h
            rows = slice(h * tq, (h + 1) * tq)
            far = jnp.full((tq, tq), rb_ref[N_BUCKETS - 1, head], F32)
            btab_ref[0, rows, :] = _t5_bias(qi - kj + tq, rb_ref, head)
            btab_ref[1, rows, :] = jnp.where(kj <= qi, _t5_bias(qi - kj, rb_ref, head), NEG_INF)
            btab_ref[2, rows, :] = jnp.where(kj > qi, far, NEG_INF)
            btab_ref[3, rows, :] = far

    selterm = ((sel_ref[0].astype(F32) - 1.0) * MASK_BIG).astype(BF16)
    qh = [q_ref[:, h * HD_A:(h + 1) * HD_A].astype(BF16) for h in range(HPG_A)]
    q_aug = jnp.concatenate([jnp.concatenate([x, selterm], axis=1) for x in qh], axis=0)
    q_all = jnp.concatenate(qh, axis=0)

    def slc_chunk(kc, tab):
        k0 = pl.multiple_of(kc * tq, tq)
        s = _qk(q_aug, kaug_ref[pl.ds(k0, tq), :]) + btab_ref[tab]
        _softmax_step(s, vsb_ref[pl.ds(k0, tq), :], m_ref, l_ref, acc_ref)

    def win_chunk(kc, tab):
        k0 = pl.multiple_of(kc * tq, tq)
        s = _qk(q_all, kwb_ref[pl.ds(k0, tq), :]) + btab_ref[tab]
        _softmax_step(s, vwb_ref[pl.ds(k0, tq), :], m_ref, l_ref, acc_ref)

    _softmax_init(m_ref, l_ref, acc_ref)

    def far_body(kc, carry):
        slc_chunk(kc, 3)
        return carry

    lax.fori_loop(0, jnp.maximum(qt - 1, 0), far_body, 0)

    @pl.when(qt >= 1)
    def _():
        slc_chunk(qt - 1, 0)

    slc_chunk(qt, 1)
    os_ref[...] = acc_ref[...] / l_ref[...]

    _softmax_init(m_ref, l_ref, acc_ref)
    back = WINDOW // tq
    for step in range(back):
        tab = 2 if step == 0 else (0 if step == back - 1 else 3)

        @pl.when(qt >= back - step)
        def _():
            win_chunk(qt - (back - step), tab)

    win_chunk(qt, 1)
    ow = acc_ref[...] / l_ref[...]

    gates = jax.nn.sigmoid(gn_ref[...])
    for h in range(HPG_A):
        g0, g1, g2 = _gate_cols(gates, g, h)
        rows = slice(h * tq, (h + 1) * tq)
        hs = slice(h * HD_A, (h + 1) * HD_A)
        o_ref[:, hs] = (g0 * oc_ref[:, hs] + g1 * os_ref[rows, :] + g2 * ow[rows, :]).astype(BF16)


def prompt_attention(proj, o_cmp, sel, rel_bias, nseq, seq):
    r = nseq * seq
    nqt = seq // PTQ
    nsp = sel.shape[2]
    assert nsp == 128 and seq % PTQ == 0
    kernel = functools.partial(_prompt_attn_kernel, seq)

    def kv(col):
        return pl.BlockSpec((seq, HD_A), lambda n, g, t: (n, col // HD_A + g))

    qspec = pl.BlockSpec((PTQ, HPG_A * HD_A), lambda n, g, t: (n * nqt + t, g))
    rows8 = HPG_A * PTQ
    return pl.pallas_call(
        kernel,
        grid=(nseq, KV_A, nqt),
        in_specs=[pl.BlockSpec(memory_space=pltpu.SMEM),
                  qspec,
                  pl.BlockSpec((1, PTQ, nsp), lambda n, g, t: (g, n * nqt + t, 0)),
                  kv(C_KVS), kv(C_KVS + KV_A * HD_A), kv(C_KVW), kv(C_KVW + KV_A * HD_A),
                  qspec,
                  pl.BlockSpec((PTQ, 128), lambda n, g, t: (n * nqt + t, C_GN // 128))],
        out_specs=qspec,
        out_shape=jax.ShapeDtypeStruct((r, H_A * HD_A), BF16),
        scratch_shapes=[pltpu.VMEM((seq, 2 * HD_A), BF16), pltpu.VMEM((seq, HD_A), BF16),
                        pltpu.VMEM((seq, HD_A), BF16), pltpu.VMEM((seq, HD_A), BF16),
                        pltpu.VMEM((4, rows8, PTQ), F32),
                        pltpu.VMEM((rows8, 1), F32), pltpu.VMEM((rows8, 1), F32), pltpu.VMEM((rows8, HD_A), F32),
                        pltpu.VMEM((rows8, HD_A), F32)],
        compiler_params=_cp("arbitrary", "arbitrary", "arbitrary"),
        name="prompt_attention",
    )(rel_bias, proj, sel, proj, proj, proj, proj, o_cmp, proj)


SUB = 512


def _sample_attn_kernel(past, rb_ref, q_ref, sel_ref, kp_ref, vp_ref, kn_ref, vn_ref, kwc_ref, vwc_ref, kwn_ref,
                        vwn_ref, oc_ref, gn_ref, o_ref, m_ref, l_ref, acc_ref):
    g = pl.program_id(1)
    kc = pl.program_id(2)
    nkc = pl.num_programs(2)
    tq = q_ref.shape[0]
    rows8 = HPG_A * tq
    nsp = sel_ref.shape[2]
    chunk = kp_ref.shape[1]
    ns_last = past // SLC_BLK

    selterm = ((sel_ref[0].astype(F32) - 1.0) * MASK_BIG).astype(BF16)
    selrows = jnp.concatenate([selterm] * HPG_A, axis=0)
    q_all = jnp.concatenate([q_ref[:, h * HD_A:(h + 1) * HD_A].astype(BF16) for h in range(HPG_A)], axis=0)
    q_aug = jnp.concatenate([q_all, selrows], axis=1)
    rowh = lax.shift_right_logical(lax.broadcasted_iota(jnp.int32, (rows8, 1), 0), int(math.log2(tq)))
    far = jnp.zeros((rows8, 1), F32)
    for h in range(HPG_A):
        far = jnp.where(rowh == h, rb_ref[N_BUCKETS - 1, g * HPG_A + h], far)

    def head_bias(dist):
        return jnp.concatenate([_t5_bias(dist[h * tq:(h + 1) * tq], rb_ref, g * HPG_A + h) for h in range(HPG_A)],
                               axis=0)

    @pl.when(kc == 0)
    def _():
        _softmax_init(m_ref, l_ref, acc_ref)

    qi_s = lax.broadcasted_iota(jnp.int32, (rows8, SUB), 0) & (tq - 1)
    kj_s = lax.broadcasted_iota(jnp.int32, (rows8, SUB), 1)
    for sub in range(chunk // SUB):
        k = kp_ref[0, sub * SUB:(sub + 1) * SUB, :].astype(BF16)
        v = vp_ref[0, sub * SUB:(sub + 1) * SUB, :].astype(BF16)
        blk = lax.shift_right_logical(kc * chunk + sub * SUB + lax.broadcasted_iota(jnp.int32, (SUB, nsp), 0), 6)
        onehot = jnp.where(blk == lax.broadcasted_iota(jnp.int32, (SUB, nsp), 1), 1.0, 0.0).astype(BF16)
        s = _qk(q_aug, jnp.concatenate([k, onehot], axis=1))
        if sub < chunk // SUB - 1:
            _softmax_step(s + far, v, m_ref, l_ref, acc_ref)
        else:
            @pl.when(kc < nkc - 1)
            def _():
                _softmax_step(s + far, v, m_ref, l_ref, acc_ref)

            @pl.when(kc == nkc - 1)
            def _():
                _softmax_step(s + head_bias(qi_s + SUB - kj_s), v, m_ref, l_ref, acc_ref)

    @pl.when(kc == nkc - 1)
    def _():
        qi = lax.broadcasted_iota(jnp.int32, (rows8, 128), 0) & (tq - 1)
        kj = lax.broadcasted_iota(jnp.int32, (rows8, 128), 1)
        pad = jnp.zeros((128 - tq, HD_A), BF16)
        causal = kj <= qi
        b_new = jnp.where(causal, head_bias(qi - kj), NEG_INF)
        k_new = jnp.concatenate([kn_ref[...].astype(BF16), pad], axis=0)
        v_new = jnp.concatenate([vn_ref[...].astype(BF16), pad], axis=0)
        s = _qk(q_all, k_new) + b_new + selrows[:, ns_last:ns_last + 1].astype(F32) * SCALE
        _softmax_step(s, v_new, m_ref, l_ref, acc_ref)
        o_s = acc_ref[...] / l_ref[...]
        _softmax_init(m_ref, l_ref, acc_ref)
        wrows = kwc_ref.shape[2]
        qi_w = lax.broadcasted_iota(jnp.int32, (rows8, wrows), 0) & (tq - 1)
        kj_w = lax.broadcasted_iota(jnp.int32, (rows8, wrows), 1)
        dw = qi_w + wrows - kj_w
        s = _qk(q_all, kwc_ref[0, 0].astype(BF16)) + jnp.where(dw < WINDOW, head_bias(dw), NEG_INF)
        _softmax_step(s, vwc_ref[0, 0].astype(BF16), m_ref, l_ref, acc_ref)
        k_new = jnp.concatenate([kwn_ref[...].astype(BF16), pad], axis=0)
        v_new = jnp.concatenate([vwn_ref[...].astype(BF16), pad], axis=0)
        _softmax_step(_qk(q_all, k_new) + b_new, v_new, m_ref, l_ref, acc_ref)
        o_w = acc_ref[...] / l_ref[...]
        gates = jax.nn.sigmoid(gn_ref[...])
        for h in range(HPG_A):
            g0, g1, g2 = _gate_cols(gates, g, h)
            rows = slice(h * tq, (h + 1) * tq)
            hs = slice(h * HD_A, (h + 1) * HD_A)
            o_ref[:, hs] = (g0 * oc_ref[:, hs] + g1 * o_s[rows, :] + g2 * o_w[rows, :]).astype(BF16)


def sample_attention(proj, o_cmp, sel, past_kv, layer, cache_win, rel_bias, nseq, tq, past):
    r = nseq * tq
    nsp = sel.shape[2]
    chunk = CMP_ROWS
    nkc = past // chunk
    kernel = functools.partial(_sample_attn_kernel, past)
    wrows = cache_win.shape[2]
    assert wrows == WINDOW

    def new_rows(col):
        return pl.BlockSpec((tq, HD_A), lambda n, g, k: (n, col // HD_A + g))

    def past_rows(c):
        return pl.BlockSpec((1, chunk, HD_A), lambda n, g, k: (layer, n * nkc + k, c * KV_A + g))

    def win_rows(c):
        return pl.BlockSpec((1, 1, wrows, HD_A), lambda n, g, k: (layer, n, 0, c * KV_A + g))

    qspec = pl.BlockSpec((tq, HPG_A * HD_A), lambda n, g, k: (n, g))
    rows8 = HPG_A * tq
    return pl.pallas_call(
        kernel,
        grid=(nseq, KV_A, nkc),
        in_specs=[pl.BlockSpec(memory_space=pltpu.SMEM),
                  qspec,
                  pl.BlockSpec((1, tq, nsp), lambda n, g, k: (g, n, 0)),
                  past_rows(0), past_rows(1),
                  new_rows(C_KVS), new_rows(C_KVS + KV_A * HD_A),
                  win_rows(0), win_rows(1),
                  new_rows(C_KVW), new_rows(C_KVW + KV_A * HD_A),
                  qspec,
                  pl.BlockSpec((tq, 128), lambda n, g, k: (n, C_GN // 128))],
        out_specs=qspec,
        out_shape=jax.ShapeDtypeStruct((r, H_A * HD_A), BF16),
        scratch_shapes=[pltpu.VMEM((rows8, 1), F32), pltpu.VMEM((rows8, 1), F32), pltpu.VMEM((rows8, HD_A), F32)],
        compiler_params=_cp("arbitrary", "arbitrary", "arbitrary"),
        name="sample_attention",
    )(rel_bias, proj, sel, past_kv, past_kv, proj, proj, cache_win, cache_win, proj, proj, o_cmp, proj)


def _prep_w_in(w):
    src = 0
    pieces = []
    for width, dst in _SEGS:
        pieces.append((dst, w[:, src:src + width]))
        src += width
    pieces.sort(key=lambda p: p[0])
    cols, pos = [], 0
    for dst, piece in pieces:
        assert dst >= pos
        if dst > pos:
            cols.append(jnp.zeros((w.shape[0], dst - pos), w.dtype))
        cols.append(piece)
        pos = dst + piece.shape[1]
    if pos < IN_PAD:
        cols.append(jnp.zeros((w.shape[0], IN_PAD - pos), w.dtype))
    return jnp.concatenate(cols, axis=1).astype(BF16)


def _state_rows(state, nseq, width):
    k = state.shape[1]
    return jnp.concatenate([jnp.zeros((nseq, 8 - k, width), F32), state.astype(F32)], axis=1)


def _trunk_layer(x, p, rel_bias, nseq, seq_rows, valid, tiles, attn_fn, conv_b_prev, conv_f_prev, h0t):
    tm, tm_merge, tm_ffn = tiles
    proj = rms_matmul(x, p['norm1'], p['w_in'], tm, 768)
    o_a = attn_fn(proj)
    o_b, h_t = ssd_mixer(proj, conv_b_prev, p['conv_b_w'], p['conv_b_bias'], p['dt_bias'], p['a_log'], p['d_skip'],
                         p['ssm_norm'], h0t, nseq, seq_rows, valid)
    o_c, v_rows = chunk_mlp_mixer(proj, p['v_ln_g'], p['v_ln_b'], p['w_spatial'], p['b_spatial'], seq_rows)
    merged = gated_merge(o_a, o_b, o_c, p['w_br_a'], p['w_br_b'], p['w_br_c'], proj, tm_merge, 512)
    x = matmul_residual(merged, p['w_out'], x, tm, 1024)
    h = rms_matmul(x, p['norm2'], p['w_ff_in'], tm, 1024)
    x = ffn_out(h, conv_f_prev, p['ff_conv_w'], p['ff_conv_b'], p['w_ff_out'], x, seq_rows, tm_ffn, 512)
    return x, proj, h, h_t, v_rows


def kernel(x_prompt, x_sample, cache_cmp_kv, cache_slc_kv, cache_win_kv, state_ssm, state_conv_ssm, state_conv_ffn,
           page_table, rel_bias, norm1, w_in, cmp_pe, cmp_w1, cmp_w2, conv_b_w, conv_b_bias, dt_bias, a_log, d_skip,
           ssm_norm, v_ln_g, v_ln_b, w_spatial, b_spatial, w_br_a, w_br_b, w_br_c, w_out, norm2, w_ff_in,
           ff_conv_w, ff_conv_b, w_ff_out, final_norm):
    bp, seq = x_prompt.shape[:2]
    bs, dec = x_sample.shape[:2]
    depth = w_in.shape[0]
    n_pool = cache_cmp_kv.shape[1]
    past = page_table.shape[1] * PAGE_SIZE
    st = SAMPLE_T
    kv_shape = (2, KV_A, HD_A)

    pages_c, pages_s = gather_pages(page_table.astype(jnp.int32),
                                    cache_cmp_kv.reshape(depth, n_pool, PAGE_SIZE, ROW_W),
                                    cache_slc_kv.reshape(depth, n_pool, PAGE_SIZE, ROW_W))
    past_c = pages_c.reshape(depth, bs * past, ROW_W)
    past_s = pages_s.reshape(depth, bs * past, ROW_W)
    cache_win = cache_win_kv.reshape(depth, bs, WINDOW, ROW_W)

    xp = x_prompt.reshape(bp * seq, D_MODEL)
    xs = jnp.pad(x_sample, ((0, 0), (0, st - dec), (0, 0))).reshape(bs * st, D_MODEL)
    zeros_b = jnp.zeros((bp, 8, XBC_DIM), F32)
    zeros_f = jnp.zeros((bp, 8, D_FF), F32)
    zeros_h = jnp.zeros((bp, N_B, D_INNER_B), F32)
    ns_p = -(-seq // SLC_BLK)
    ns_s = -(-(past + dec) // SLC_BLK)

    outs = [[] for _ in range(13)]
    for l in range(depth):
        p = {'norm1': norm1[l], 'w_in': _prep_w_in(w_in[l]), 'conv_b_w': jnp.pad(conv_b_w[l], ((0, 8 - CONV_B), (0, 0))),
             'conv_b_bias': conv_b_bias[l], 'dt_bias': dt_bias[l], 'a_log': a_log[l], 'd_skip': d_skip[l],
             'ssm_norm': ssm_norm[l], 'v_ln_g': v_ln_g[l], 'v_ln_b': v_ln_b[l], 'w_spatial': w_spatial[l],
             'b_spatial': b_spatial[l], 'w_br_a': w_br_a[l].astype(BF16), 'w_br_b': w_br_b[l].astype(BF16),
             'w_br_c': w_br_c[l].astype(BF16), 'w_out': w_out[l].astype(BF16), 'norm2': norm2[l],
             'w_ff_in': w_ff_in[l].astype(BF16), 'ff_conv_w': jnp.pad(ff_conv_w[l], ((0, 8 - CONV_F), (0, 0))),
             'ff_conv_b': ff_conv_b[l], 'w_ff_out': w_ff_out[l].astype(BF16)}
        pe, w1, w2 = cmp_pe[l], cmp_w1[l].astype(BF16), cmp_w2[l].astype(BF16)

        def attn_prompt(proj):
            kc, vc = compress_kv(proj.reshape(1, bp * seq, IN_PAD), 0, C_KVC // ROW_W, bp, seq, pe, w1, w2)
            o_cmp, sel = compressed_attention(proj, kc, vc, rel_bias, bp, seq, PTQ, 0, ns_p)
            return prompt_attention(proj, o_cmp, sel, rel_bias, bp, seq)

        def attn_sample(proj):
            kc, vc = compress_kv(past_c, l, 0, bs, past, pe, w1, w2)
            o_cmp, sel = compressed_attention(proj, kc, vc, rel_bias, bs, st, st, past, ns_s)
            return sample_attention(proj, o_cmp, sel, past_s, l, cache_win, rel_bias, bs, st, past)

        xp, proj, h, h_t, _ = _trunk_layer(xp, p, rel_bias, bp, seq, seq, (1024, 512, 512), attn_prompt,
                                           zeros_b, zeros_f, zeros_h)
        pr = proj.reshape(bp, seq, IN_PAD)
        outs[0].append(pr[:, :, C_KVC:C_KVC + ROW_W].reshape((bp, seq) + kv_shape))
        outs[1].append(pr[:, :, C_KVS:C_KVS + ROW_W].reshape((bp, seq) + kv_shape))
        outs[2].append(pr[:, seq - WINDOW:, C_KVW:C_KVW + ROW_W].reshape((bp, WINDOW) + kv_shape))
        outs[3].append(h_t.reshape(bp, N_B, H_B, HD_B).transpose(0, 2, 3, 1))
        outs[4].append(pr[:, seq - (CONV_B - 1):, C_XBC:C_XBC + XBC_DIM])
        outs[5].append(h.reshape(bp, seq, 2 * D_FF)[:, seq - (CONV_F - 1):, :D_FF])

        h0t = state_ssm[l].astype(F32).transpose(0, 3, 1, 2).reshape(bs, N_B, D_INNER_B)
        xs, proj, h, h_t, v_rows = _trunk_layer(xs, p, rel_bias, bs, st, dec, (bs * st, bs * st, st), attn_sample,
                                                _state_rows(state_conv_ssm[l], bs, XBC_DIM),
                                                _state_rows(state_conv_ffn[l], bs, D_FF), h0t)
        pr = proj.reshape(bs, st, IN_PAD)
        kvw_new = pr[:, :dec, C_KVW:C_KVW + ROW_W].reshape((bs, dec) + kv_shape)
        outs[6].append(pr[:, :dec, C_KVC:C_KVC + ROW_W].reshape((bs, dec) + kv_shape))
        outs[7].append(pr[:, :dec, C_KVS:C_KVS + ROW_W].reshape((bs, dec) + kv_shape))
        outs[8].append(jnp.concatenate([cache_win_kv[l][:, dec:], kvw_new], axis=1))
        outs[9].append(h_t.reshape(bs, N_B, H_B, HD_B).transpose(0, 2, 3, 1))
        xbc_all = jnp.concatenate([state_conv_ssm[l], pr[:, :dec, C_XBC:C_XBC + XBC_DIM]], axis=1)
        outs[10].append(xbc_all[:, dec:])
        a_all = jnp.concatenate([state_conv_ffn[l], h.reshape(bs, st, 2 * D_FF)[:, :dec, :D_FF]], axis=1)
        outs[11].append(a_all[:, dec:])
        outs[12].append(v_rows.reshape(bs, st, D_C)[:, :dec])

    y_prompt = rms_norm_rows(xp, final_norm, 1024).reshape(bp, seq, D_MODEL)
    y_sample = rms_norm_rows(xs, final_norm, bs * st).reshape(bs, st, D_MODEL)[:, :dec]
    return (y_prompt, y_sample) + tuple(jnp.stack(o) for o in outs)
```

```python
import functools
import math

import numpy as np
import jax
import jax.numpy as jnp
from jax import lax
from jax.experimental import pallas as pl
from jax.experimental.pallas import tpu as pltpu

F32 = jnp.float32
BF16 = jnp.bfloat16

D_MODEL = 2048
DEPTH = 4
PAGE_SIZE = 128
H_A = 16
HD_A = 128
KV_A = 2
HPG_A = H_A // KV_A
CMP_BLK = 32
SLC_BLK = 64
N_SEL = 16
WINDOW = 512
FORCE_BONUS = 1.0e4
NEG_INF = -1.0e30
N_BUCKETS = 32
MAX_DIST = 128
D_INNER_B = 2048
HD_B = 64
H_B = D_INNER_B // HD_B
G_B = 4
N_B = 128
CONV_B = 4
SSD_CHUNK = 128
XBC_DIM = D_INNER_B + 2 * G_B * N_B
D_C = 2048
G_C = 8
CHUNK_C = 128
D_FF = 5632
CONV_F = 3
NORM_EPS = 1e-6

C_Q = 0
C_Z = 2048
C_U = 4096
C_V = 6144
C_GM = 8192
C_XBC = 14336
C_KVC = 17408
C_KVS = 17920
C_KVW = 18432
C_GN = 18944
C_DT = 19072
IN_PAD = 19200
_SEGS = ((H_A * HD_A, C_Q), (512, C_KVC), (512, C_KVS), (512, C_KVW), (3 * H_A, C_GN), (D_INNER_B, C_Z),
         (XBC_DIM, C_XBC), (H_B, C_DT), (D_C, C_U), (D_C, C_V), (3 * D_MODEL, C_GM))

VMEM_LIMIT = 56 * 1024 * 1024
SAMPLE_T = 16


def _cp(*sem):
    return pltpu.CompilerParams(dimension_semantics=sem, vmem_limit_bytes=VMEM_LIMIT)


def _gelu(x):
    return jax.nn.gelu(x)


def _t5_thresholds():
    n = np.arange(0, 4 * MAX_DIST)
    max_exact = N_BUCKETS // 2
    nf = np.maximum(n, 1).astype(np.float64)
    large = max_exact + (np.log(nf / max_exact) / math.log(MAX_DIST / max_exact) * (N_BUCKETS - max_exact)).astype(np.int64)
    b = np.where(n < max_exact, n, np.minimum(large, N_BUCKETS - 1))
    return [int(np.argmax(b >= k)) for k in range(1, N_BUCKETS)]


_T5_THR = _t5_thresholds()


def _t5_bias(dist, rb_ref, head):
    out = jnp.full(dist.shape, rb_ref[0, head], F32)
    for k, thr in enumerate(_T5_THR):
        out = jnp.where(dist >= thr, rb_ref[k + 1, head], out)
    return out


def _rmsmm_kernel(x_ref, g_ref, w_ref, o_ref, xn_ref):
    @pl.when(pl.program_id(1) == 0)
    def _():
        x = x_ref[...]
        y = x * lax.rsqrt(jnp.mean(x * x, axis=-1, keepdims=True) + NORM_EPS)
        xn_ref[...] = (y * g_ref[...]).astype(BF16)

    o_ref[...] = jnp.dot(xn_ref[...], w_ref[...], preferred_element_type=F32)


def rms_matmul(x, g, w, tm, tn):
    r, d = x.shape
    n = w.shape[1]
    return pl.pallas_call(
        _rmsmm_kernel,
        grid=(r // tm, n // tn),
        in_specs=[pl.BlockSpec((tm, d), lambda i, j: (i, 0)),
                  pl.BlockSpec((1, d), lambda i, j: (0, 0)),
                  pl.BlockSpec((d, tn), lambda i, j: (0, j))],
        out_specs=pl.BlockSpec((tm, tn), lambda i, j: (i, j)),
        out_shape=jax.ShapeDtypeStruct((r, n), F32),
        scratch_shapes=[pltpu.VMEM((tm, d), BF16)],
        compiler_params=_cp("arbitrary", "arbitrary"),
        name="rms_matmul",
    )(x, g.reshape(1, d), w)


def _merge_kernel(oa_ref, ob_ref, oc_ref, wa_ref, wb_ref, wc_ref, ga_ref, gb_ref, gc_ref, o_ref):
    acc = jax.nn.sigmoid(ga_ref[...]) * jnp.dot(oa_ref[...], wa_ref[...], preferred_element_type=F32)
    acc = acc + jax.nn.sigmoid(gb_ref[...]) * jnp.dot(ob_ref[...], wb_ref[...], preferred_element_type=F32)
    acc = acc + jax.nn.sigmoid(gc_ref[...]) * jnp.dot(oc_ref[...], wc_ref[...], preferred_element_type=F32)
    o_ref[...] = acc.astype(BF16)


def gated_merge(o_a, o_b, o_c, w_a, w_b, w_c, proj, tm, tn):
    r, d = o_a.shape
    n = w_a.shape[1]
    gblk = C_GM // tn
    per = D_MODEL // tn
    lhs = pl.BlockSpec((tm, d), lambda i, j: (i, 0))
    wsp = pl.BlockSpec((d, tn), lambda i, j: (0, j))

    def gate(b):
        return pl.BlockSpec((tm, tn), lambda i, j: (i, gblk + b * per + j))

    return pl.pallas_call(
        _merge_kernel,
        grid=(r // tm, n // tn),
        in_specs=[lhs, lhs, lhs, wsp, wsp, wsp, gate(0), gate(1), gate(2)],
        out_specs=pl.BlockSpec((tm, tn), lambda i, j: (i, j)),
        out_shape=jax.ShapeDtypeStruct((r, n), BF16),
        compiler_params=_cp("arbitrary", "arbitrary"),
        name="gated_merge",
    )(o_a, o_b, o_c, w_a, w_b, w_c, proj, proj, proj)


def _mmres_kernel(a_ref, w_ref, r_ref, o_ref):
    o_ref[...] = r_ref[...] + jnp.dot(a_ref[...], w_ref[...], preferred_element_type=F32)


def matmul_residual(a, w, res, tm, tn):
    r, k = a.shape
    n = w.shape[1]
    return pl.pallas_call(
        _mmres_kernel,
        grid=(r // tm, n // tn),
        in_specs=[pl.BlockSpec((tm, k), lambda i, j: (i, 0)),
                  pl.BlockSpec((k, tn), lambda i, j: (0, j)),
                  pl.BlockSpec((tm, tn), lambda i, j: (i, j))],
        out_specs=pl.BlockSpec((tm, tn), lambda i, j: (i, j)),
        out_shape=jax.ShapeDtypeStruct((r, n), F32),
        compiler_params=_cp("arbitrary", "arbitrary"),
        name="matmul_residual",
    )(a, w, res)


def _ffn_out_kernel(tiles_per_seq, a_ref, b_ref, prev_ref, cw_ref, cb_ref, w_ref, r_ref, o_ref,
                    ext_ref, carry_ref, acc_ref):
    i = pl.program_id(0)
    k = pl.program_id(1)
    tm = a_ref.shape[0]

    @pl.when(i % tiles_per_seq == 0)
    def _():
        ext_ref[0:8, :] = prev_ref[0]

    @pl.when(i % tiles_per_seq != 0)
    def _():
        ext_ref[0:8, :] = carry_ref[k]

    a = a_ref[...]
    ext_ref[8:8 + tm, :] = a
    carry_ref[k] = a_ref[tm - 8:tm, :]
    cw = cw_ref[...]
    y = cb_ref[...] + ext_ref[6:6 + tm, :] * cw[0:1]
    y = y + ext_ref[7:7 + tm, :] * cw[1:2]
    y = y + a * cw[2:3]
    h = (_gelu(y) * b_ref[...]).astype(BF16)
    part = jnp.dot(h, w_ref[...], preferred_element_type=F32)

    @pl.when(k == 0)
    def _():
        acc_ref[...] = part

    @pl.when(k != 0)
    def _():
        acc_ref[...] += part

    @pl.when(k == pl.num_programs(1) - 1)
    def _():
        o_ref[...] = r_ref[...] + acc_ref[...]


def ffn_out(h, prev, conv_w, conv_b, w, res, seq_len, tm, tk):
    r = h.shape[0]
    n = w.shape[1]
    nk = D_FF // tk
    kernel = functools.partial(_ffn_out_kernel, seq_len // tm)
    return pl.pallas_call(
        kernel,
        grid=(r // tm, nk),
        in_specs=[pl.BlockSpec((tm, tk), lambda i, k: (i, k)),
                  pl.BlockSpec((tm, tk), lambda i, k: (i, nk + k)),
                  pl.BlockSpec((1, 8, tk), lambda i, k: (i * tm // seq_len, 0, k)),
                  pl.BlockSpec((8, tk), lambda i, k: (0, k)),
                  pl.BlockSpec((1, tk), lambda i, k: (0, k)),
                  pl.BlockSpec((tk, n), lambda i, k: (k, 0)),
                  pl.BlockSpec((tm, n), lambda i, k: (i, 0))],
        out_specs=pl.BlockSpec((tm, n), lambda i, k: (i, 0)),
        out_shape=jax.ShapeDtypeStruct((r, n), F32),
        scratch_shapes=[pltpu.VMEM((tm + 8, tk), F32), pltpu.VMEM((nk, 8, tk), F32), pltpu.VMEM((tm, n), F32)],
        compiler_params=_cp("arbitrary", "arbitrary"),
        name="ffn_out",
    )(h, h, prev, conv_w, conv_b.reshape(1, D_FF), w, res)


def _rms_kernel(x_ref, g_ref, o_ref):
    x = x_ref[...]
    o_ref[...] = x * lax.rsqrt(jnp.mean(x * x, axis=-1, keepdims=True) + NORM_EPS) * g_ref[...]


def rms_norm_rows(x, g, tm):
    r, d = x.shape
    return pl.pallas_call(
        _rms_kernel,
        grid=(r // tm,),
        in_specs=[pl.BlockSpec((tm, d), lambda i: (i, 0)), pl.BlockSpec((1, d), lambda i: (0, 0))],
        out_specs=pl.BlockSpec((tm, d), lambda i: (i, 0)),
        out_shape=jax.ShapeDtypeStruct((r, d), F32),
        compiler_params=_cp("arbitrary"),
        name="final_rms_norm",
    )(x, g.reshape(1, d))


def _ssd_kernel(rows_in, valid_len, x_ref, b_ref, c_ref, z_ref, dt_ref, prev_ref, cw_ref, cb_ref, dtb_ref, alog_ref,
                dskip_ref, norm_ref, h0_ref, tril_ref, o_ref, ht_ref, ext_ref, st_ref, tt_ref, y_ref, xw_ref):
    L = SSD_CHUNK
    c = pl.program_id(1)

    @pl.when(c == 0)
    def _():
        ext_ref[0:8, :] = prev_ref[0]
        ht_ref[...] = h0_ref[...]

    @pl.when(c != 0)
    def _():
        ext_ref[0:8, :] = ext_ref[L:L + 8, :]

    if rows_in < L:
        ext_ref[8 + rows_in:8 + L, :] = jnp.zeros((L - rows_in, XBC_DIM), F32)
    ext_ref[8:8 + rows_in, 0:D_INNER_B] = x_ref[...]
    ext_ref[8:8 + rows_in, D_INNER_B:D_INNER_B + G_B * N_B] = b_ref[...]
    ext_ref[8:8 + rows_in, D_INNER_B + G_B * N_B:XBC_DIM] = c_ref[...]

    cw = cw_ref[...]
    y = cb_ref[...] + ext_ref[5:5 + L, :] * cw[0:1]
    y = y + ext_ref[6:6 + L, :] * cw[1:2]
    y = y + ext_ref[7:7 + L, :] * cw[2:3]
    y = y + ext_ref[8:8 + L, :] * cw[3:4]
    xbc = y * jax.nn.sigmoid(y)

    if rows_in < L:
        dt_raw = jnp.concatenate([dt_ref[...], jnp.zeros((L - rows_in, 128), F32)], axis=0)
    else:
        dt_raw = dt_ref[...]
    d_all = jax.nn.softplus(dt_raw + dtb_ref[...])
    if valid_len < L:
        row = lax.broadcasted_iota(jnp.int32, (L, 128), 0)
        d_all = jnp.where(row < valid_len, d_all, 0.0)
    a_row = -jnp.exp(alog_ref[...])
    tril = tril_ref[...]
    cs = jnp.dot(tril, d_all * a_row, preferred_element_type=F32, precision=lax.Precision.HIGHEST)
    st_ref[0] = cs
    st_ref[1] = jnp.exp(cs)
    st_ref[2] = jnp.exp(cs[L - 1:L, :] - cs) * d_all
    tt_ref[0] = cs.T
    tt_ref[1] = d_all.T
    lower = tril > 0.5

    for g in range(G_B):
        b_g = xbc[:, D_INNER_B + g * N_B:D_INNER_B + (g + 1) * N_B]
        c_g = xbc[:, D_INNER_B + G_B * N_B + g * N_B:D_INNER_B + G_B * N_B + (g + 1) * N_B]
        cgb = c_g.astype(BF16)
        cb = lax.dot_general(cgb, b_g.astype(BF16), (((1,), (1,)), ((), ())), preferred_element_type=F32)
        bt = b_g.T.astype(BF16)
        gs = slice(g * 512, (g + 1) * 512)
        hg = ht_ref[0, :, gs]
        yoff = jnp.dot(cgb, hg.astype(BF16), preferred_element_type=F32)
        for hh in range(8):
            h = g * 8 + hh
            hs = slice(h * HD_B, (h + 1) * HD_B)
            x_h = xbc[:, hs]
            seg = st_ref[0, :, h:h + 1] - tt_ref[0, h:h + 1, :]
            decay = jnp.exp(jnp.where(lower, seg, NEG_INF))
            m = (cb * decay * tt_ref[1, h:h + 1, :]).astype(BF16)
            y_h = jnp.dot(m, x_h.astype(BF16), preferred_element_type=F32)
            y_h = y_h + yoff[:, hh * HD_B:(hh + 1) * HD_B] * st_ref[1, :, h:h + 1]
            y_ref[:, hs] = y_h + dskip_ref[:, hs] * x_h
            xw_ref[:, hh * HD_B:(hh + 1) * HD_B] = (x_h * st_ref[2, :, h:h + 1]).astype(BF16)
        upd = jnp.dot(bt, xw_ref[...], preferred_element_type=F32)
        for hh in range(8):
            h = g * 8 + hh
            hs = slice(h * HD_B, (h + 1) * HD_B)
            ht_ref[0, :, hs] = ht_ref[0, :, hs] * st_ref[1, L - 1:L, h:h + 1] + upd[:, hh * HD_B:(hh + 1) * HD_B]

    if rows_in < L:
        z = jnp.concatenate([z_ref[...], jnp.zeros((L - rows_in, D_INNER_B), F32)], axis=0)
    else:
        z = z_ref[...]
    yz = y_ref[...] * (z * jax.nn.sigmoid(z))
    for g in range(G_B):
        gs = slice(g * 512, (g + 1) * 512)
        v = yz[:, gs]
        out = v * lax.rsqrt(jnp.mean(v * v, axis=-1, keepdims=True) + NORM_EPS) * norm_ref[:, gs]
        o_ref[:, gs] = out[0:rows_in].astype(BF16)


def ssd_mixer(proj, prev, conv_w, conv_b, dt_bias, a_log, d_skip, ssm_norm, h0t, nseq, seq_rows, valid_len):
    rows_in = min(seq_rows, SSD_CHUNK)
    nchunk = seq_rows // rows_in
    r = nseq * seq_rows
    tril = jnp.asarray(np.tril(np.ones((SSD_CHUNK, SSD_CHUNK), np.float32)))
    pad128 = lambda v: jnp.pad(v.astype(F32), (0, 128 - v.shape[0])).reshape(1, 128)
    kernel = functools.partial(_ssd_kernel, rows_in, min(valid_len, SSD_CHUNK))
    rowblk = lambda n, c: n * nchunk + c
    const = lambda n, c: (0, 0)
    return pl.pallas_call(
        kernel,
        grid=(nseq, nchunk),
        in_specs=[pl.BlockSpec((rows_in, D_INNER_B), lambda n, c: (rowblk(n, c), C_XBC // 2048)),
                  pl.BlockSpec((rows_in, 512), lambda n, c: (rowblk(n, c), (C_XBC + 2048) // 512)),
                  pl.BlockSpec((rows_in, 512), lambda n, c: (rowblk(n, c), (C_XBC + 2560) // 512)),
                  pl.BlockSpec((rows_in, D_INNER_B), lambda n, c: (rowblk(n, c), C_Z // 2048)),
                  pl.BlockSpec((rows_in, 128), lambda n, c: (rowblk(n, c), C_DT // 128)),
                  pl.BlockSpec((1, 8, XBC_DIM), lambda n, c: (n, 0, 0)),
                  pl.BlockSpec((8, XBC_DIM), const),
                  pl.BlockSpec((1, XBC_DIM), const),
                  pl.BlockSpec((1, 128), const),
                  pl.BlockSpec((1, 128), const),
                  pl.BlockSpec((1, D_INNER_B), const),
                  pl.BlockSpec((1, D_INNER_B), const),
                  pl.BlockSpec((1, N_B, D_INNER_B), lambda n, c: (n, 0, 0)),
                  pl.BlockSpec((SSD_CHUNK, SSD_CHUNK), const)],
        out_specs=[pl.BlockSpec((rows_in, D_INNER_B), lambda n, c: (rowblk(n, c), 0)),
                   pl.BlockSpec((1, N_B, D_INNER_B), lambda n, c: (n, 0, 0))],
        out_shape=[jax.ShapeDtypeStruct((r, D_INNER_B), BF16),
                   jax.ShapeDtypeStruct((nseq, N_B, D_INNER_B), F32)],
        scratch_shapes=[pltpu.VMEM((SSD_CHUNK + 8, XBC_DIM), F32),
                        pltpu.VMEM((3, SSD_CHUNK, 128), F32),
                        pltpu.VMEM((2, 128, SSD_CHUNK), F32),
                        pltpu.VMEM((SSD_CHUNK, D_INNER_B), F32),
                        pltpu.VMEM((SSD_CHUNK, 512), BF16)],
        compiler_params=_cp("arbitrary", "arbitrary"),
        name="ssd_mixer",
    )(proj, proj, proj, proj, proj, prev, conv_w, conv_b.reshape(1, XBC_DIM), pad128(dt_bias), pad128(a_log),
      jnp.repeat(d_skip.astype(F32), HD_B).reshape(1, D_INNER_B), ssm_norm.reshape(1, D_INNER_B), h0t, tril)


def _cmlp_kernel(rows_in, u_ref, v_ref, lg_ref, lb_ref, ws_ref, bst_ref, o_ref, vn_ref):
    L = CHUNK_C
    v = _gelu(v_ref[...])
    mu = jnp.mean(v, axis=-1, keepdims=True)
    vc = v - mu
    var = jnp.mean(vc * vc, axis=-1, keepdims=True)
    vn = vc * lax.rsqrt(var + NORM_EPS) * lg_ref[...] + lb_ref[...]
    vn_ref[...] = vn
    u = _gelu(u_ref[...])
    if rows_in < L:
        vn = jnp.concatenate([vn, jnp.zeros((L - rows_in, D_C), F32)], axis=0)
    vnb = vn.astype(BF16)
    row = lax.broadcasted_iota(jnp.int32, (L, L), 0)
    col = lax.broadcasted_iota(jnp.int32, (L, L), 1)
    gw = D_C // G_C
    for g in range(G_C):
        w = jnp.where(col <= row, ws_ref[g], 0.0).astype(BF16)
        s = jnp.dot(w, vnb[:, g * gw:(g + 1) * gw], preferred_element_type=F32) + bst_ref[:, g:g + 1]
        o_ref[:, g * gw:(g + 1) * gw] = (u[:, g * gw:(g + 1) * gw] * s[0:rows_in]).astype(BF16)


def chunk_mlp_mixer(proj, ln_g, ln_b, w_s, b_s, seq_rows):
    r = proj.shape[0]
    rows_in = min(seq_rows, CHUNK_C)
    kernel = functools.partial(_cmlp_kernel, rows_in)
    bst = jnp.pad(b_s.T.astype(F32), ((0, 0), (0, 128 - G_C)))
    return pl.pallas_call(
        kernel,
        grid=(r // rows_in,),
        in_specs=[pl.BlockSpec((rows_in, D_C), lambda i: (i, C_U // 2048)),
                  pl.BlockSpec((rows_in, D_C), lambda i: (i, C_V // 2048)),
                  pl.BlockSpec((1, D_C), lambda i: (0, 0)),
                  pl.BlockSpec((1, D_C), lambda i: (0, 0)),
                  pl.BlockSpec((G_C, CHUNK_C, CHUNK_C), lambda i: (0, 0, 0)),
                  pl.BlockSpec((CHUNK_C, 128), lambda i: (0, 0))],
        out_specs=[pl.BlockSpec((rows_in, D_C), lambda i: (i, 0)),
                   pl.BlockSpec((rows_in, D_C), lambda i: (i, 0))],
        out_shape=[jax.ShapeDtypeStruct((r, D_C), BF16), jax.ShapeDtypeStruct((r, D_C), F32)],
        compiler_params=_cp("arbitrary"),
        name="chunk_mlp_mixer",
    )(proj, proj, ln_g.reshape(1, D_C), ln_b.reshape(1, D_C), w_s, bst)


ROW_W = 2 * KV_A * HD_A
SCALE = HD_A ** -0.5
MASK_BIG = 1.0e30
M_INIT = -2.0e30


CMP_ROWS = 4096
CHUNK_PAGES = CMP_ROWS // PAGE_SIZE
BLK_PER_PAGE = PAGE_SIZE // CMP_BLK


def _paged_chunk_pipeline(pt_ref, copies_for, n_chunks_per_seq):
    n = pl.program_id(0)
    k = pl.program_id(1)
    step = n * n_chunks_per_seq + k
    total = pl.num_programs(0) * n_chunks_per_seq
    slot = step % 2

    @pl.when(step == 0)
    def _():
        for cp in copies_for(n, k, slot):
            cp.start()

    @pl.when(step + 1 < total)
    def _():
        wrap = k + 1 == n_chunks_per_seq
        for cp in copies_for(jnp.where(wrap, n + 1, n), jnp.where(wrap, 0, k + 1), 1 - slot):
            cp.start()

    for cp in copies_for(n, k, slot):
        cp.wait()
    return slot


def _cmpmlp_paged_kernel(layer, pt_ref, cache_ref, pe_ref, w1_ref, w2_ref, kc_ref, vc_ref, buf_ref, sem):
    nkc = pl.num_programs(1)

    def copies_for(n, k, slot):
        return [pltpu.make_async_copy(cache_ref.at[layer, pt_ref[n, k * CHUNK_PAGES + p]],
                                      buf_ref.at[slot, :, :, pl.ds(p * 8, 8), :], sem.at[slot])
                for p in range(CHUNK_PAGES)]

    slot = _paged_chunk_pipeline(pt_ref, copies_for, nkc)
    rows = CHUNK_PAGES * KV_A * BLK_PER_PAGE
    for c in range(2):
        def body(s, acc):
            x = buf_ref[slot, c, s] + pe_ref[s * 2 + c]
            return acc + jnp.dot(x.astype(BF16), w1_ref[c, s], preferred_element_type=F32)

        acc = lax.fori_loop(0, CMP_BLK, body, jnp.zeros((rows, HD_A), F32))
        out = jnp.dot(_gelu(acc).astype(BF16), w2_ref[c], preferred_element_type=F32)
        dst = kc_ref if c == 0 else vc_ref
        dst[0] = out.reshape(CHUNK_PAGES, KV_A * BLK_PER_PAGE, HD_A)


def compress_kv_paged(page_table, cache_t, layer, pe, w1, w2):
    nseq, npages = page_table.shape
    nkc = npages // CHUNK_PAGES
    rows8 = KV_A * BLK_PER_PAGE
    shape = jax.ShapeDtypeStruct((nseq, npages, rows8, HD_A), F32)
    ospec = pl.BlockSpec((1, CHUNK_PAGES, rows8, HD_A), lambda n, k, pt: (n, k, 0, 0))
    kernel = functools.partial(_cmpmlp_paged_kernel, layer)
    kc, vc = pl.pallas_call(
        kernel,
        grid_spec=pltpu.PrefetchScalarGridSpec(
            num_scalar_prefetch=1, grid=(nseq, nkc),
            in_specs=[pl.BlockSpec(memory_space=pl.ANY),
                      pl.BlockSpec((CMP_BLK * 2, 1, HD_A), lambda n, k, pt: (0, 0, 0)),
                      pl.BlockSpec((2, CMP_BLK, HD_A, HD_A), lambda n, k, pt: (0, 0, 0, 0)),
                      pl.BlockSpec((2, HD_A, HD_A), lambda n, k, pt: (0, 0, 0))],
            out_specs=[ospec, ospec],
            scratch_shapes=[pltpu.VMEM((2, 2, CMP_BLK, CHUNK_PAGES * rows8, HD_A), F32),
                            pltpu.SemaphoreType.DMA((2,))]),
        out_shape=[shape, shape],
        compiler_params=_cp("arbitrary", "arbitrary"),
        name="compress_kv_paged",
    )(page_table, cache_t, pe.reshape(CMP_BLK * 2, 1, HD_A), w1, w2)

    def regroup(x):
        x = x.reshape(nseq, npages, KV_A, BLK_PER_PAGE, HD_A).transpose(0, 2, 1, 3, 4)
        return x.reshape(nseq, KV_A, npages * BLK_PER_PAGE, HD_A).astype(BF16)

    return regroup(kc), regroup(vc)


def _cmpmlp_kernel(x00_ref, x01_ref, x10_ref, x11_ref, pe_ref, w1_ref, w2_ref, kc_ref, vc_ref):
    x_refs = ((x00_ref, x01_ref), (x10_ref, x11_ref))
    nblk = x00_ref.shape[1] // CMP_BLK
    for c in range(2):
        def body(s, acc):
            xs = [x_refs[c][g][0, pl.ds(s, nblk, stride=CMP_BLK), :] for g in range(KV_A)]
            x = jnp.concatenate(xs, axis=0) + pe_ref[s * 2 + c]
            return acc + jnp.dot(x.astype(BF16), w1_ref[c, s], preferred_element_type=F32)

        acc = lax.fori_loop(0, CMP_BLK, body, jnp.zeros((KV_A * nblk, HD_A), F32))
        out = jnp.dot(_gelu(acc).astype(BF16), w2_ref[c], preferred_element_type=F32)
        dst = kc_ref if c == 0 else vc_ref
        for g in range(KV_A):
            dst[0, g] = out[g * nblk:(g + 1) * nblk].astype(BF16)


def compress_kv(src, lead, col_blk, nseq, seq_rows, pe, w1, w2):
    rows = min(CMP_ROWS, seq_rows)
    nchunk = seq_rows // rows
    nb = seq_rows // CMP_BLK
    shape = jax.ShapeDtypeStruct((nseq, KV_A, nb, HD_A), BF16)
    ospec = pl.BlockSpec((1, KV_A, rows // CMP_BLK, HD_A), lambda n, c: (n, 0, c, 0))

    def xspec(part):
        return pl.BlockSpec((1, rows, HD_A), lambda n, c: (lead, n * nchunk + c, col_blk * (ROW_W // HD_A) + part))

    return pl.pallas_call(
        _cmpmlp_kernel,
        grid=(nseq, nchunk),
        in_specs=[xspec(0), xspec(1), xspec(2), xspec(3),
                  pl.BlockSpec((CMP_BLK * 2, 1, HD_A), lambda n, c: (0, 0, 0)),
                  pl.BlockSpec((2, CMP_BLK, HD_A, HD_A), lambda n, c: (0, 0, 0, 0)),
                  pl.BlockSpec((2, HD_A, HD_A), lambda n, c: (0, 0, 0))],
        out_specs=[ospec, ospec],
        out_shape=[shape, shape],
        compiler_params=_cp("arbitrary", "arbitrary"),
        name="compress_kv",
    )(src, src, src, src, pe.reshape(CMP_BLK * 2, 1, HD_A), w1, w2)


def _cmpattn_kernel(q_pos0, ns, blocks_major, rb_ref, q_ref, kc_ref, vc_ref, pool_ref, o_ref, sel_ref):
    g = pl.program_id(1)
    qt = pl.program_id(2)
    tq = q_ref.shape[0]
    nb = kc_ref.shape[2]
    nsp = pool_ref.shape[0]
    q0 = q_pos0 + qt * tq
    qpos = q0 + lax.broadcasted_iota(jnp.int32, (tq, nb), 0)
    dist = qpos - (lax.broadcasted_iota(jnp.int32, (tq, nb), 1) * CMP_BLK + (CMP_BLK - 1))
    vis = dist >= 0
    anyvis = ((q0 + lax.broadcasted_iota(jnp.int32, (tq, 1), 0)) >= CMP_BLK - 1).astype(F32)
    kc = kc_ref[0, 0]
    vc = vc_ref[0, 0]
    psum = jnp.zeros((tq, nb), F32)
    for h in range(HPG_A):
        hs = slice(h * HD_A, (h + 1) * HD_A)
        lc = lax.dot_general(q_ref[:, hs].astype(BF16), kc, (((1,), (1,)), ((), ())), preferred_element_type=F32)
        lc = lc * SCALE + _t5_bias(dist, rb_ref, g * HPG_A + h)
        lc = jnp.where(vis, lc, NEG_INF)
        e = jnp.exp(lc - jnp.max(lc, axis=-1, keepdims=True))
        p = e / jnp.sum(e, axis=-1, keepdims=True) * anyvis
        o_ref[:, hs] = jnp.dot(p.astype(BF16), vc, preferred_element_type=F32)
        psum = psum + p

    tqp = max(tq, 128)
    if tq < tqp:
        psum = jnp.concatenate([psum, jnp.zeros((tqp - tq, nb), F32)], axis=0)
    imp = lax.dot_general(pool_ref[...], psum, (((1,), (1,)), ((), ())), preferred_element_type=F32,
                          precision=lax.Precision.HIGHEST)
    j = lax.broadcasted_iota(jnp.int32, (nsp, tqp), 0)
    cur = lax.shift_right_logical(q0 + lax.broadcasted_iota(jnp.int32, (nsp, tqp), 1), 6)
    forced = (j == 0) | (j == cur) | (j == cur - 1)
    score = jnp.where(j <= cur, imp + jnp.where(forced, FORCE_BONUS, 0.0), -1.0)
    score = jnp.where(j < ns, score, -2.0)
    sel = jnp.zeros((nsp, tqp), F32)
    for _ in range(N_SEL):
        top = jnp.max(score, axis=0, keepdims=True)
        idx = jnp.min(jnp.where(score == top, j, 1 << 30), axis=0, keepdims=True)
        hit = j == idx
        sel = jnp.where(hit, 1.0, sel)
        score = jnp.where(hit, -3.0, score)
    if blocks_major:
        sel_ref[0] = sel.astype(BF16)
    else:
        sel_ref[0] = sel.T[0:tq].astype(BF16)


def compressed_attention(proj, kc, vc, rel_bias, nseq, seq_rows, tq, q_pos0, ns, blocks_major):
    r = nseq * seq_rows
    nb = kc.shape[2]
    nsp = -(-ns // 128) * 128
    nqt = seq_rows // tq
    pool = np.zeros((nsp, nb), np.float32)
    pool[np.arange(nb) // (SLC_BLK // CMP_BLK), np.arange(nb)] = 1.0
    kernel = functools.partial(_cmpattn_kernel, q_pos0, ns, blocks_major)
    kvspec = pl.BlockSpec((1, 1, nb, HD_A), lambda n, g, t: (n, g, 0, 0))
    if blocks_major:
        assert tq % 128 == 0
        sel_spec = pl.BlockSpec((1, nsp, tq), lambda n, g, t: (g, 0, n * nqt + t))
        sel_shape = jax.ShapeDtypeStruct((KV_A, nsp, r), BF16)
    else:
        sel_spec = pl.BlockSpec((1, tq, nsp), lambda n, g, t: (g, n * nqt + t, 0))
        sel_shape = jax.ShapeDtypeStruct((KV_A, r, nsp), BF16)
    return pl.pallas_call(
        kernel,
        grid=(nseq, KV_A, nqt),
        in_specs=[pl.BlockSpec(memory_space=pltpu.SMEM),
                  pl.BlockSpec((tq, HPG_A * HD_A), lambda n, g, t: (n * nqt + t, g)),
                  kvspec, kvspec,
                  pl.BlockSpec((nsp, nb), lambda n, g, t: (0, 0))],
        out_specs=[pl.BlockSpec((tq, HPG_A * HD_A), lambda n, g, t: (n * nqt + t, g)), sel_spec],
        out_shape=[jax.ShapeDtypeStruct((r, H_A * HD_A), F32), sel_shape],
        compiler_params=_cp("arbitrary", "arbitrary", "arbitrary"),
        name="compressed_attention",
    )(rel_bias, proj, kc, vc, jnp.asarray(pool))


def _softmax_step(s, v, m_ref, l_ref, acc_ref):
    m_prev = m_ref[...]
    m_new = jnp.maximum(m_prev, jnp.max(s, axis=-1, keepdims=True))
    alpha = jnp.exp(m_prev - m_new)
    p = jnp.exp(s - m_new)
    l_ref[...] = alpha * l_ref[...] + jnp.sum(p, axis=-1, keepdims=True)
    acc_ref[...] = alpha * acc_ref[...] + jnp.dot(p.astype(BF16), v, preferred_element_type=F32)
    m_ref[...] = m_new


def _softmax_init(m_ref, l_ref, acc_ref):
    m_ref[...] = jnp.full(m_ref.shape, M_INIT, F32)
    l_ref[...] = jnp.zeros(l_ref.shape, F32)
    acc_ref[...] = jnp.zeros(acc_ref.shape, F32)


def _qk(q, k):
    return lax.dot_general(q, k, (((1,), (1,)), ((), ())), preferred_element_type=F32) * SCALE


PTQ = 128


def _softmax_step_t(s, vt, m_ref, l_ref, acc_ref):
    m_prev = m_ref[...]
    m_new = jnp.maximum(m_prev, jnp.max(s, axis=0, keepdims=True))
    alpha = jnp.exp(m_prev - m_new)
    p = jnp.exp(s - m_new)
    l_ref[...] = alpha * l_ref[...] + jnp.sum(p, axis=0, keepdims=True)
    acc_ref[...] = alpha * acc_ref[...] + jnp.dot(vt, p.astype(BF16), preferred_element_type=F32)
    m_ref[...] = m_new


def _prompt_attn_kernel(seq, rb_ref, q_ref, sel_ref, ks_ref, vs_ref, kw_ref, vw_ref, oc_ref, gn_ref, o_ref,
                        kaug_ref, vst_ref, kwb_ref, vwt_ref, btab_ref, m_ref, l_ref, acc_ref, os_ref):
    g = pl.program_id(1)
    qt = pl.program_id(2)
    tq = PTQ

    @pl.when(qt == 0)
    def _():
        kaug_ref[:, 0:HD_A] = ks_ref[...].astype(BF16)
        row = lax.broadcasted_iota(jnp.int32, (seq, 128), 0)
        lane = lax.broadcasted_iota(jnp.int32, (seq, 128), 1)
        kaug_ref[:, HD_A:HD_A + 128] = jnp.where(lax.shift_right_logical(row, 6) == lane, 1.0, 0.0).astype(BF16)
        kwb_ref[...] = kw_ref[...].astype(BF16)

        def transpose_chunk(c, carry):
            r0 = pl.multiple_of(c * tq, tq)
            vst_ref[c] = vs_ref[pl.ds(r0, tq), :].T.astype(BF16)
            vwt_ref[c] = vw_ref[pl.ds(r0, tq), :].T.astype(BF16)
            return carry

        lax.fori_loop(0, seq // tq, transpose_chunk, 0)
        kj = lax.broadcasted_iota(jnp.int32, (tq, tq), 0)
        qi = lax.broadcasted_iota(jnp.int32, (tq, tq), 1)
        for h in range(HPG_A):
            head = g * HPG_A + h
            cols = slice(h * tq, (h + 1) * tq)
            far = jnp.full((tq, tq), rb_ref[N_BUCKETS - 1, head], F32)
            btab_ref[0, :, cols] = _t5_bias(qi - kj + tq, rb_ref, head)
            btab_ref[1, :, cols] = jnp.where(kj <= qi, _t5_bias(qi - kj, rb_ref, head), NEG_INF)
            btab_ref[2, :, cols] = jnp.where(kj > qi, far, NEG_INF)
            btab_ref[3, :, cols] = far

    selterm = ((sel_ref[0].astype(F32) - 1.0) * MASK_BIG).astype(BF16)
    q_t = jnp.concatenate([q_ref[:, h * HD_A:(h + 1) * HD_A].T.astype(BF16) for h in range(HPG_A)], axis=1)
    q_aug_t = jnp.concatenate([q_t, jnp.concatenate([selterm] * HPG_A, axis=1)], axis=0)

    def slc_chunk(kc, tab):
        k0 = pl.multiple_of(kc * tq, tq)
        s = jnp.dot(kaug_ref[pl.ds(k0, tq), :], q_aug_t, preferred_element_type=F32) * SCALE + btab_ref[tab]
        _softmax_step_t(s, vst_ref[kc], m_ref, l_ref, acc_ref)

    def win_chunk(kc, tab):
        k0 = pl.multiple_of(kc * tq, tq)
        s = jnp.dot(kwb_ref[pl.ds(k0, tq), :], q_t, preferred_element_type=F32) * SCALE + btab_ref[tab]
        _softmax_step_t(s, vwt_ref[kc], m_ref, l_ref, acc_ref)

    _softmax_init(m_ref, l_ref, acc_ref)

    def far_body(kc, carry):
        slc_chunk(kc, 3)
        return carry

    lax.fori_loop(0, jnp.maximum(qt - 1, 0), far_body, 0)

    @pl.when(qt >= 1)
    def _():
        slc_chunk(qt - 1, 0)

    slc_chunk(qt, 1)
    os_ref[...] = acc_ref[...] / l_ref[...]

    _softmax_init(m_ref, l_ref, acc_ref)
    back = WINDOW // tq
    for step in range(back):
        tab = 2 if step == 0 else (0 if step == back - 1 else 3)

        @pl.when(qt >= back - step)
        def _():
            win_chunk(qt - (back - step), tab)

    win_chunk(qt, 1)
    ow = acc_ref[...] / l_ref[...]

    gates = jax.nn.sigmoid(gn_ref[...])
    gates_t = gates.T
    for h in range(HPG_A):
        lo, hi = 3 * h, 3 * (HPG_A + h)
        g0 = jnp.where(g == 0, gates[:, lo:lo + 1], gates[:, hi:hi + 1])
        g1 = jnp.where(g == 0, gates_t[lo + 1:lo + 2, :], gates_t[hi + 1:hi + 2, :])
        g2 = jnp.where(g == 0, gates_t[lo + 2:lo + 3, :], gates_t[hi + 2:hi + 3, :])
        cols = slice(h * tq, (h + 1) * tq)
        hs = slice(h * HD_A, (h + 1) * HD_A)
        sw = (g1 * os_ref[:, cols] + g2 * ow[:, cols]).T
        o_ref[:, hs] = (g0 * oc_ref[:, hs] + sw).astype(BF16)


def prompt_attention(proj, o_cmp, sel, rel_bias, nseq, seq):
    r = nseq * seq
    nqt = seq // PTQ
    nsp = sel.shape[1]
    assert nsp == 128 and seq % PTQ == 0
    kernel = functools.partial(_prompt_attn_kernel, seq)

    def kv(col):
        return pl.BlockSpec((seq, HD_A), lambda n, g, t: (n, col // HD_A + g))

    qspec = pl.BlockSpec((PTQ, HPG_A * HD_A), lambda n, g, t: (n * nqt + t, g))
    rows8 = HPG_A * PTQ
    return pl.pallas_call(
        kernel,
        grid=(nseq, KV_A, nqt),
        in_specs=[pl.BlockSpec(memory_space=pltpu.SMEM),
                  qspec,
                  pl.BlockSpec((1, nsp, PTQ), lambda n, g, t: (g, 0, n * nqt + t)),
                  kv(C_KVS), kv(C_KVS + KV_A * HD_A), kv(C_KVW), kv(C_KVW + KV_A * HD_A),
                  qspec,
                  pl.BlockSpec((PTQ, 128), lambda n, g, t: (n * nqt + t, C_GN // 128))],
        out_specs=qspec,
        out_shape=jax.ShapeDtypeStruct((r, H_A * HD_A), BF16),
        scratch_shapes=[pltpu.VMEM((seq, 2 * HD_A), BF16), pltpu.VMEM((nqt, HD_A, PTQ), BF16),
                        pltpu.VMEM((seq, HD_A), BF16), pltpu.VMEM((nqt, HD_A, PTQ), BF16),
                        pltpu.VMEM((4, PTQ, rows8), F32),
                        pltpu.VMEM((1, rows8), F32), pltpu.VMEM((1, rows8), F32), pltpu.VMEM((HD_A, rows8), F32),
                        pltpu.VMEM((HD_A, rows8), F32)],
        compiler_params=_cp("arbitrary", "arbitrary", "arbitrary"),
        name="prompt_attention",
    )(rel_bias, proj, sel, proj, proj, proj, proj, o_cmp, proj)


SUB = 512


def _sample_attn_kernel(past, layer, pt_ref, rb_ref, q_ref, sel_ref, cache_ref, kn_ref, vn_ref, kwc_ref, vwc_ref,
                        kwn_ref, vwn_ref, oc_ref, gn_ref, o_ref, buf_ref, sem, m_ref, l_ref, acc_ref):
    kc = pl.program_id(1)
    nkc = pl.num_programs(1)
    tq = q_ref.shape[0]
    rows8 = HPG_A * tq
    nsp = sel_ref.shape[2]
    chunk = CHUNK_PAGES * PAGE_SIZE
    ns_last = past // SLC_BLK

    def copies_for(n, k, slot):
        return [pltpu.make_async_copy(cache_ref.at[layer, pt_ref[n, k * CHUNK_PAGES + p]],
                                      buf_ref.at[slot, :, :, pl.ds(p * PAGE_SIZE, PAGE_SIZE), :], sem.at[slot])
                for p in range(CHUNK_PAGES)]

    slot = _paged_chunk_pipeline(pt_ref, copies_for, nkc)
    rowh = lax.shift_right_logical(lax.broadcasted_iota(jnp.int32, (rows8, 1), 0), int(math.log2(tq)))
    qi_s = lax.broadcasted_iota(jnp.int32, (rows8, SUB), 0) & (tq - 1)
    kj_s = lax.broadcasted_iota(jnp.int32, (rows8, SUB), 1)
    gates = jax.nn.sigmoid(gn_ref[...])

    for g in range(KV_A):
        stats = (m_ref.at[g], l_ref.at[g], acc_ref.at[g])
        selterm = ((sel_ref[g].astype(F32) - 1.0) * MASK_BIG).astype(BF16)
        selrows = jnp.concatenate([selterm] * HPG_A, axis=0)
        q_all = jnp.concatenate([q_ref[:, (g * HPG_A + h) * HD_A:(g * HPG_A + h + 1) * HD_A].astype(BF16)
                                 for h in range(HPG_A)], axis=0)
        q_aug = jnp.concatenate([q_all, selrows], axis=1)
        far = jnp.zeros((rows8, 1), F32)
        for h in range(HPG_A):
            far = jnp.where(rowh == h, rb_ref[N_BUCKETS - 1, g * HPG_A + h], far)

        def head_bias(dist):
            return jnp.concatenate([_t5_bias(dist[h * tq:(h + 1) * tq], rb_ref, g * HPG_A + h)
                                    for h in range(HPG_A)], axis=0)

        @pl.when(kc == 0)
        def _():
            _softmax_init(*stats)

        for sub in range(chunk // SUB):
            k = buf_ref[slot, 0, g, sub * SUB:(sub + 1) * SUB, :]
            v = buf_ref[slot, 1, g, sub * SUB:(sub + 1) * SUB, :]
            blk = lax.shift_right_logical(kc * chunk + sub * SUB + lax.broadcasted_iota(jnp.int32, (SUB, nsp), 0), 6)
            onehot = jnp.where(blk == lax.broadcasted_iota(jnp.int32, (SUB, nsp), 1), 1.0, 0.0).astype(BF16)
            s = _qk(q_aug, jnp.concatenate([k, onehot], axis=1))
            if sub < chunk // SUB - 1:
                _softmax_step(s + far, v, *stats)
            else:
                @pl.when(kc < nkc - 1)
                def _():
                    _softmax_step(s + far, v, *stats)

                @pl.when(kc == nkc - 1)
                def _():
                    _softmax_step(s + head_bias(qi_s + SUB - kj_s), v, *stats)

        @pl.when(kc == nkc - 1)
        def _():
            m_g, l_g, acc_g = stats
            gk = slice(g * HD_A, (g + 1) * HD_A)
            qi = lax.broadcasted_iota(jnp.int32, (rows8, 128), 0) & (tq - 1)
            kj = lax.broadcasted_iota(jnp.int32, (rows8, 128), 1)
            pad = jnp.zeros((128 - tq, HD_A), BF16)
            b_new = jnp.where(kj <= qi, head_bias(qi - kj), NEG_INF)
            k_new = jnp.concatenate([kn_ref[:, gk].astype(BF16), pad], axis=0)
            v_new = jnp.concatenate([vn_ref[:, gk].astype(BF16), pad], axis=0)
            s = _qk(q_all, k_new) + b_new + selrows[:, ns_last:ns_last + 1].astype(F32) * SCALE
            _softmax_step(s, v_new, *stats)
            o_s = acc_g[...] / l_g[...]
            _softmax_init(*stats)
            wrows = kwc_ref.shape[2]
            qi_w = lax.broadcasted_iota(jnp.int32, (rows8, wrows), 0) & (tq - 1)
            kj_w = lax.broadcasted_iota(jnp.int32, (rows8, wrows), 1)
            dw = qi_w + wrows - kj_w
            s = _qk(q_all, kwc_ref[0, 0, :, gk].astype(BF16)) + jnp.where(dw < WINDOW, head_bias(dw), NEG_INF)
            _softmax_step(s, vwc_ref[0, 0, :, gk].astype(BF16), *stats)
            k_new = jnp.concatenate([kwn_ref[:, gk].astype(BF16), pad], axis=0)
            v_new = jnp.concatenate([vwn_ref[:, gk].astype(BF16), pad], axis=0)
            _softmax_step(_qk(q_all, k_new) + b_new, v_new, *stats)
            o_w = acc_g[...] / l_g[...]
            for h in range(HPG_A):
                lane = 3 * (g * HPG_A + h)
                rows = slice(h * tq, (h + 1) * tq)
                hs = slice((g * HPG_A + h) * HD_A, (g * HPG_A + h + 1) * HD_A)
                o_ref[:, hs] = (gates[:, lane:lane + 1] * oc_ref[:, hs] + gates[:, lane + 1:lane + 2] * o_s[rows, :]
                                + gates[:, lane + 2:lane + 3] * o_w[rows, :]).astype(BF16)


def sample_attention(proj, o_cmp, sel, page_table, cache_t, layer, cache_win, rel_bias, nseq, tq, past):
    r = nseq * tq
    nsp = sel.shape[2]
    chunk = CHUNK_PAGES * PAGE_SIZE
    nkc = past // chunk
    kernel = functools.partial(_sample_attn_kernel, past, layer)
    wrows = cache_win.shape[2]
    assert wrows == WINDOW
    gw = KV_A * HD_A

    def new_rows(col):
        return pl.BlockSpec((tq, gw), lambda n, k, pt: (n, col // gw))

    def win_rows(c):
        return pl.BlockSpec((1, 1, wrows, gw), lambda n, k, pt: (layer, n, 0, c))

    qspec = pl.BlockSpec((tq, H_A * HD_A), lambda n, k, pt: (n, 0))
    rows8 = HPG_A * tq
    return pl.pallas_call(
        kernel,
        grid_spec=pltpu.PrefetchScalarGridSpec(
            num_scalar_prefetch=1, grid=(nseq, nkc),
            in_specs=[pl.BlockSpec(memory_space=pltpu.SMEM),
                      qspec,
                      pl.BlockSpec((KV_A, tq, nsp), lambda n, k, pt: (0, n, 0)),
                      pl.BlockSpec(memory_space=pl.ANY),
                      new_rows(C_KVS), new_rows(C_KVS + gw),
                      win_rows(0), win_rows(1),
                      new_rows(C_KVW), new_rows(C_KVW + gw),
                      qspec,
                      pl.BlockSpec((tq, 128), lambda n, k, pt: (n, C_GN // 128))],
            out_specs=qspec,
            scratch_shapes=[pltpu.VMEM((2, 2, KV_A, chunk, HD_A), BF16), pltpu.SemaphoreType.DMA((2,)),
                            pltpu.VMEM((KV_A, rows8, 1), F32), pltpu.VMEM((KV_A, rows8, 1), F32),
                            pltpu.VMEM((KV_A, rows8, HD_A), F32)]),
        out_shape=jax.ShapeDtypeStruct((r, H_A * HD_A), BF16),
        compiler_params=_cp("arbitrary", "arbitrary"),
        name="sample_attention",
    )(page_table, rel_bias, proj, sel, cache_t, proj, proj, cache_win, cache_win, proj, proj, o_cmp, proj)


def _prep_w_in(w):
    src = 0
    pieces = []
    for width, dst in _SEGS:
        pieces.append((dst, w[:, src:src + width]))
        src += width
    pieces.sort(key=lambda p: p[0])
    cols, pos = [], 0
    for dst, piece in pieces:
        assert dst >= pos
        if dst > pos:
            cols.append(jnp.zeros((w.shape[0], dst - pos), w.dtype))
        cols.append(piece)
        pos = dst + piece.shape[1]
    if pos < IN_PAD:
        cols.append(jnp.zeros((w.shape[0], IN_PAD - pos), w.dtype))
    return jnp.concatenate(cols, axis=1).astype(BF16)


def _state_rows(state, nseq, width):
    k = state.shape[1]
    return jnp.concatenate([jnp.zeros((nseq, 8 - k, width), F32), state.astype(F32)], axis=1)


def _trunk_layer(x, p, rel_bias, nseq, seq_rows, valid, tiles, attn_fn, conv_b_prev, conv_f_prev, h0t):
    tm, tm_merge, tm_ffn = tiles
    proj = rms_matmul(x, p['norm1'], p['w_in'], tm, 768)
    o_a = attn_fn(proj)
    o_b, h_t = ssd_mixer(proj, conv_b_prev, p['conv_b_w'], p['conv_b_bias'], p['dt_bias'], p['a_log'], p['d_skip'],
                         p['ssm_norm'], h0t, nseq, seq_rows, valid)
    o_c, v_rows = chunk_mlp_mixer(proj, p['v_ln_g'], p['v_ln_b'], p['w_spatial'], p['b_spatial'], seq_rows)
    merged = gated_merge(o_a, o_b, o_c, p['w_br_a'], p['w_br_b'], p['w_br_c'], proj, tm_merge, 512)
    x = matmul_residual(merged, p['w_out'], x, tm, 1024)
    h = rms_matmul(x, p['norm2'], p['w_ff_in'], tm, 1024)
    x = ffn_out(h, conv_f_prev, p['ff_conv_w'], p['ff_conv_b'], p['w_ff_out'], x, seq_rows, tm_ffn, 512)
    return x, proj, h, h_t, v_rows


def kernel(x_prompt, x_sample, cache_cmp_kv, cache_slc_kv, cache_win_kv, state_ssm, state_conv_ssm, state_conv_ffn,
           page_table, rel_bias, norm1, w_in, cmp_pe, cmp_w1, cmp_w2, conv_b_w, conv_b_bias, dt_bias, a_log, d_skip,
           ssm_norm, v_ln_g, v_ln_b, w_spatial, b_spatial, w_br_a, w_br_b, w_br_c, w_out, norm2, w_ff_in,
           ff_conv_w, ff_conv_b, w_ff_out, final_norm):
    bp, seq = x_prompt.shape[:2]
    bs, dec = x_sample.shape[:2]
    depth = w_in.shape[0]
    n_pool = cache_cmp_kv.shape[1]
    past = page_table.shape[1] * PAGE_SIZE
    st = SAMPLE_T
    kv_shape = (2, KV_A, HD_A)

    page_table = page_table.astype(jnp.int32)
    cmp_t = cache_cmp_kv.reshape(depth, n_pool, BLK_PER_PAGE, CMP_BLK, 2, KV_A, HD_A).transpose(0, 1, 4, 3, 5, 2, 6)
    cmp_t = cmp_t.reshape(depth, n_pool, 2, CMP_BLK, KV_A * BLK_PER_PAGE, HD_A)
    slc_t = cache_slc_kv.transpose(0, 1, 3, 4, 2, 5).astype(BF16)
    cache_win = cache_win_kv.reshape(depth, bs, WINDOW, ROW_W)

    xp = x_prompt.reshape(bp * seq, D_MODEL)
    xs = jnp.pad(x_sample, ((0, 0), (0, st - dec), (0, 0))).reshape(bs * st, D_MODEL)
    zeros_b = jnp.zeros((bp, 8, XBC_DIM), F32)
    zeros_f = jnp.zeros((bp, 8, D_FF), F32)
    zeros_h = jnp.zeros((bp, N_B, D_INNER_B), F32)
    ns_p = -(-seq // SLC_BLK)
    ns_s = -(-(past + dec) // SLC_BLK)

    outs = [[] for _ in range(13)]
    for l in range(depth):
        p = {'norm1': norm1[l], 'w_in': _prep_w_in(w_in[l]), 'conv_b_w': jnp.pad(conv_b_w[l], ((0, 8 - CONV_B), (0, 0))),
             'conv_b_bias': conv_b_bias[l], 'dt_bias': dt_bias[l], 'a_log': a_log[l], 'd_skip': d_skip[l],
             'ssm_norm': ssm_norm[l], 'v_ln_g': v_ln_g[l], 'v_ln_b': v_ln_b[l], 'w_spatial': w_spatial[l],
             'b_spatial': b_spatial[l], 'w_br_a': w_br_a[l].astype(BF16), 'w_br_b': w_br_b[l].astype(BF16),
             'w_br_c': w_br_c[l].astype(BF16), 'w_out': w_out[l].astype(BF16), 'norm2': norm2[l],
             'w_ff_in': w_ff_in[l].astype(BF16), 'ff_conv_w': jnp.pad(ff_conv_w[l], ((0, 8 - CONV_F), (0, 0))),
             'ff_conv_b': ff_conv_b[l], 'w_ff_out': w_ff_out[l].astype(BF16)}
        pe, w1, w2 = cmp_pe[l], cmp_w1[l].astype(BF16), cmp_w2[l].astype(BF16)

        def attn_prompt(proj):
            kc, vc = compress_kv(proj.reshape(1, bp * seq, IN_PAD), 0, C_KVC // ROW_W, bp, seq, pe, w1, w2)
            o_cmp, sel = compressed_attention(proj, kc, vc, rel_bias, bp, seq, PTQ, 0, ns_p, True)
            return prompt_attention(proj, o_cmp, sel, rel_bias, bp, seq)

        def attn_sample(proj):
            kc, vc = compress_kv_paged(page_table, cmp_t, l, pe, w1, w2)
            o_cmp, sel = compressed_attention(proj, kc, vc, rel_bias, bs, st, st, past, ns_s, False)
            return sample_attention(proj, o_cmp, sel, page_table, slc_t, l, cache_win, rel_bias, bs, st, past)

        xp, proj, h, h_t, _ = _trunk_layer(xp, p, rel_bias, bp, seq, seq, (1024, 512, 512), attn_prompt,
                                           zeros_b, zeros_f, zeros_h)
        pr = proj.reshape(bp, seq, IN_PAD)
        outs[0].append(pr[:, :, C_KVC:C_KVC + ROW_W].reshape((bp, seq) + kv_shape))
        outs[1].append(pr[:, :, C_KVS:C_KVS + ROW_W].reshape((bp, seq) + kv_shape))
        outs[2].append(pr[:, seq - WINDOW:, C_KVW:C_KVW + ROW_W].reshape((bp, WINDOW) + kv_shape))
        outs[3].append(h_t.reshape(bp, N_B, H_B, HD_B).transpose(0, 2, 3, 1))
        outs[4].append(pr[:, seq - (CONV_B - 1):, C_XBC:C_XBC + XBC_DIM])
        outs[5].append(h.reshape(bp, seq, 2 * D_FF)[:, seq - (CONV_F - 1):, :D_FF])

        h0t = state_ssm[l].astype(F32).transpose(0, 3, 1, 2).reshape(bs, N_B, D_INNER_B)
        xs, proj, h, h_t, v_rows = _trunk_layer(xs, p, rel_bias, bs, st, dec, (bs * st, bs * st, st), attn_sample,
                                                _state_rows(state_conv_ssm[l], bs, XBC_DIM),
                                                _state_rows(state_conv_ffn[l], bs, D_FF), h0t)
        pr = proj.reshape(bs, st, IN_PAD)
        kvw_new = pr[:, :dec, C_KVW:C_KVW + ROW_W].reshape((bs, dec) + kv_shape)
        outs[6].append(pr[:, :dec, C_KVC:C_KVC + ROW_W].reshape((bs, dec) + kv_shape))
        outs[7].append(pr[:, :dec, C_KVS:C_KVS + ROW_W].reshape((bs, dec) + kv_shape))
        outs[8].append(jnp.concatenate([cache_win_kv[l][:, dec:], kvw_new], axis=1))
        outs[9].append(h_t.reshape(bs, N_B, H_B, HD_B).transpose(0, 2, 3, 1))
        xbc_all = jnp.concatenate([state_conv_ssm[l], pr[:, :dec, C_XBC:C_XBC + XBC_DIM]], axis=1)
        outs[10].append(xbc_all[:, dec:])
        a_all = jnp.concatenate([state_conv_ffn[l], h.reshape(bs, st, 2 * D_FF)[:, :dec, :D_FF]], axis=1)
        outs[11].append(a_all[:, dec:])
        outs[12].append(v_rows.reshape(bs, st, D_C)[:, :dec])

    y_prompt = rms_norm_rows(xp, final_norm, 1024).reshape(bp, seq, D_MODEL)
    y_sample = rms_norm_rows(xs, final_norm, bs * st).reshape(bs, st, D_MODEL)[:, :dec]
    return (y_prompt, y_sample) + tuple(jnp.stack(o) for o in outs)
```

```python
import functools
import math

import numpy as np
import jax
import jax.numpy as jnp
from jax import lax
from jax.experimental import pallas as pl
from jax.experimental.pallas import tpu as pltpu

F32 = jnp.float32
BF16 = jnp.bfloat16

D_MODEL = 2048
DEPTH = 4
PAGE_SIZE = 128
H_A = 16
HD_A = 128
KV_A = 2
HPG_A = H_A // KV_A
CMP_BLK = 32
SLC_BLK = 64
N_SEL = 16
WINDOW = 512
FORCE_BONUS = 1.0e4
NEG_INF = -1.0e30
N_BUCKETS = 32
MAX_DIST = 128
D_INNER_B = 2048
HD_B = 64
H_B = D_INNER_B // HD_B
G_B = 4
N_B = 128
CONV_B = 4
SSD_CHUNK = 128
XBC_DIM = D_INNER_B + 2 * G_B * N_B
D_C = 2048
G_C = 8
CHUNK_C = 128
D_FF = 5632
CONV_F = 3
NORM_EPS = 1e-6

C_Q = 0
C_Z = 2048
C_U = 4096
C_V = 6144
C_GM = 8192
C_XBC = 14336
C_KVC = 17408
C_KVS = 17920
C_KVW = 18432
C_GN = 18944
C_DT = 19072
IN_PAD = 19200
_SEGS = ((H_A * HD_A, C_Q), (512, C_KVC), (512, C_KVS), (512, C_KVW), (3 * H_A, C_GN), (D_INNER_B, C_Z),
         (XBC_DIM, C_XBC), (H_B, C_DT), (D_C, C_U), (D_C, C_V), (3 * D_MODEL, C_GM))

VMEM_LIMIT = 56 * 1024 * 1024
SAMPLE_T = 16


def _cp(*sem):
    return pltpu.CompilerParams(dimension_semantics=sem, vmem_limit_bytes=VMEM_LIMIT)


def _gelu(x):
    return jax.nn.gelu(x)


def _t5_thresholds():
    n = np.arange(0, 4 * MAX_DIST)
    max_exact = N_BUCKETS // 2
    nf = np.maximum(n, 1).astype(np.float64)
    large = max_exact + (np.log(nf / max_exact) / math.log(MAX_DIST / max_exact) * (N_BUCKETS - max_exact)).astype(np.int64)
    b = np.where(n < max_exact, n, np.minimum(large, N_BUCKETS - 1))
    return [int(np.argmax(b >= k)) for k in range(1, N_BUCKETS)]


_T5_THR = _t5_thresholds()


def _t5_bias(dist, rb_ref, head):
    out = jnp.full(dist.shape, rb_ref[0, head], F32)
    for k, thr in enumerate(_T5_THR):
        out = jnp.where(dist >= thr, rb_ref[k + 1, head], out)
    return out


def _rmsmm_kernel(x_ref, g_ref, w_ref, o_ref, xn_ref):
    @pl.when(pl.program_id(1) == 0)
    def _():
        x = x_ref[...]
        y = x * lax.rsqrt(jnp.mean(x * x, axis=-1, keepdims=True) + NORM_EPS)
        xn_ref[...] = (y * g_ref[...]).astype(BF16)

    o_ref[...] = jnp.dot(xn_ref[...], w_ref[...], preferred_element_type=F32)


def rms_matmul(x, g, w, tm, tn):
    r, d = x.shape
    n = w.shape[1]
    return pl.pallas_call(
        _rmsmm_kernel,
        grid=(r // tm, n // tn),
        in_specs=[pl.BlockSpec((tm, d), lambda i, j: (i, 0)),
                  pl.BlockSpec((1, d), lambda i, j: (0, 0)),
                  pl.BlockSpec((d, tn), lambda i, j: (0, j))],
        out_specs=pl.BlockSpec((tm, tn), lambda i, j: (i, j)),
        out_shape=jax.ShapeDtypeStruct((r, n), F32),
        scratch_shapes=[pltpu.VMEM((tm, d), BF16)],
        compiler_params=_cp("arbitrary", "arbitrary"),
        name="rms_matmul",
    )(x, g.reshape(1, d), w)


def _merge_kernel(oa_ref, ob_ref, oc_ref, wa_ref, wb_ref, wc_ref, ga_ref, gb_ref, gc_ref, o_ref):
    acc = jax.nn.sigmoid(ga_ref[...]) * jnp.dot(oa_ref[...], wa_ref[...], preferred_element_type=F32)
    acc = acc + jax.nn.sigmoid(gb_ref[...]) * jnp.dot(ob_ref[...], wb_ref[...], preferred_element_type=F32)
    acc = acc + jax.nn.sigmoid(gc_ref[...]) * jnp.dot(oc_ref[...], wc_ref[...], preferred_element_type=F32)
    o_ref[...] = acc.astype(BF16)


def gated_merge(o_a, o_b, o_c, w_a, w_b, w_c, proj, tm, tn):
    r, d = o_a.shape
    n = w_a.shape[1]
    gblk = C_GM // tn
    per = D_MODEL // tn
    lhs = pl.BlockSpec((tm, d), lambda i, j: (i, 0))
    wsp = pl.BlockSpec((d, tn), lambda i, j: (0, j))

    def gate(b):
        return pl.BlockSpec((tm, tn), lambda i, j: (i, gblk + b * per + j))

    return pl.pallas_call(
        _merge_kernel,
        grid=(r // tm, n // tn),
        in_specs=[lhs, lhs, lhs, wsp, wsp, wsp, gate(0), gate(1), gate(2)],
        out_specs=pl.BlockSpec((tm, tn), lambda i, j: (i, j)),
        out_shape=jax.ShapeDtypeStruct((r, n), BF16),
        compiler_params=_cp("arbitrary", "arbitrary"),
        name="gated_merge",
    )(o_a, o_b, o_c, w_a, w_b, w_c, proj, proj, proj)


def _mmres_kernel(a_ref, w_ref, r_ref, o_ref):
    o_ref[...] = r_ref[...] + jnp.dot(a_ref[...], w_ref[...], preferred_element_type=F32)


def matmul_residual(a, w, res, tm, tn):
    r, k = a.shape
    n = w.shape[1]
    return pl.pallas_call(
        _mmres_kernel,
        grid=(r // tm, n // tn),
        in_specs=[pl.BlockSpec((tm, k), lambda i, j: (i, 0)),
                  pl.BlockSpec((k, tn), lambda i, j: (0, j)),
                  pl.BlockSpec((tm, tn), lambda i, j: (i, j))],
        out_specs=pl.BlockSpec((tm, tn), lambda i, j: (i, j)),
        out_shape=jax.ShapeDtypeStruct((r, n), F32),
        compiler_params=_cp("arbitrary", "arbitrary"),
        name="matmul_residual",
    )(a, w, res)


def _ffn_out_kernel(tiles_per_seq, a_ref, b_ref, prev_ref, cw_ref, cb_ref, w_ref, r_ref, o_ref,
                    ext_ref, carry_ref):
    i = pl.program_id(0)
    k = pl.program_id(1)
    tm = a_ref.shape[0]

    @pl.when(i % tiles_per_seq == 0)
    def _():
        ext_ref[0:8, :] = prev_ref[0]

    @pl.when(i % tiles_per_seq != 0)
    def _():
        ext_ref[0:8, :] = carry_ref[k]

    a = a_ref[...]
    ext_ref[8:8 + tm, :] = a
    carry_ref[k] = a_ref[tm - 8:tm, :]
    cw = cw_ref[...]
    y = cb_ref[...] + ext_ref[6:6 + tm, :] * cw[0:1]
    y = y + ext_ref[7:7 + tm, :] * cw[1:2]
    y = y + a * cw[2:3]
    h = (_gelu(y) * b_ref[...]).astype(BF16)
    part = jnp.dot(h, w_ref[...], preferred_element_type=F32)

    @pl.when(k == 0)
    def _():
        o_ref[...] = r_ref[...] + part

    @pl.when(k != 0)
    def _():
        o_ref[...] += part


def ffn_out(h, prev, conv_w, conv_b, w, res, seq_len, tm, tk):
    r = h.shape[0]
    n = w.shape[1]
    nk = D_FF // tk
    kernel = functools.partial(_ffn_out_kernel, seq_len // tm)
    return pl.pallas_call(
        kernel,
        grid=(r // tm, nk),
        in_specs=[pl.BlockSpec((tm, tk), lambda i, k: (i, k)),
                  pl.BlockSpec((tm, tk), lambda i, k: (i, nk + k)),
                  pl.BlockSpec((1, 8, tk), lambda i, k: (i * tm // seq_len, 0, k)),
                  pl.BlockSpec((8, tk), lambda i, k: (0, k)),
                  pl.BlockSpec((1, tk), lambda i, k: (0, k)),
                  pl.BlockSpec((tk, n), lambda i, k: (k, 0)),
                  pl.BlockSpec((tm, n), lambda i, k: (i, 0))],
        out_specs=pl.BlockSpec((tm, n), lambda i, k: (i, 0)),
        out_shape=jax.ShapeDtypeStruct((r, n), F32),
        scratch_shapes=[pltpu.VMEM((tm + 8, tk), F32), pltpu.VMEM((nk, 8, tk), F32)],
        compiler_params=_cp("arbitrary", "arbitrary"),
        name="ffn_out",
    )(h, h, prev, conv_w, conv_b.reshape(1, D_FF), w, res)


def _rms_kernel(x_ref, g_ref, o_ref):
    x = x_ref[...]
    o_ref[...] = x * lax.rsqrt(jnp.mean(x * x, axis=-1, keepdims=True) + NORM_EPS) * g_ref[...]


def rms_norm_rows(x, g, tm):
    r, d = x.shape
    return pl.pallas_call(
        _rms_kernel,
        grid=(r // tm,),
        in_specs=[pl.BlockSpec((tm, d), lambda i: (i, 0)), pl.BlockSpec((1, d), lambda i: (0, 0))],
        out_specs=pl.BlockSpec((tm, d), lambda i: (i, 0)),
        out_shape=jax.ShapeDtypeStruct((r, d), F32),
        compiler_params=_cp("arbitrary"),
        name="final_rms_norm",
    )(x, g.reshape(1, d))


def _ssd_kernel(rows_in, valid_len, x_ref, b_ref, c_ref, z_ref, dt_ref, prev_ref, cw_ref, cb_ref, dtb_ref, alog_ref,
                dskip_ref, norm_ref, h0_ref, tril_ref, o_ref, ht_ref, ext_ref, st_ref, tt_ref, y_ref, xw_ref):
    L = SSD_CHUNK
    c = pl.program_id(1)

    @pl.when(c == 0)
    def _():
        ext_ref[0:8, :] = prev_ref[0]
        ht_ref[...] = h0_ref[...]

    @pl.when(c != 0)
    def _():
        ext_ref[0:8, :] = ext_ref[L:L + 8, :]

    if rows_in < L:
        ext_ref[8 + rows_in:8 + L, :] = jnp.zeros((L - rows_in, XBC_DIM), F32)
    ext_ref[8:8 + rows_in, 0:D_INNER_B] = x_ref[...]
    ext_ref[8:8 + rows_in, D_INNER_B:D_INNER_B + G_B * N_B] = b_ref[...]
    ext_ref[8:8 + rows_in, D_INNER_B + G_B * N_B:XBC_DIM] = c_ref[...]

    cw = cw_ref[...]
    y = cb_ref[...] + ext_ref[5:5 + L, :] * cw[0:1]
    y = y + ext_ref[6:6 + L, :] * cw[1:2]
    y = y + ext_ref[7:7 + L, :] * cw[2:3]
    y = y + ext_ref[8:8 + L, :] * cw[3:4]
    xbc = y * jax.nn.sigmoid(y)

    if rows_in < L:
        dt_raw = jnp.concatenate([dt_ref[...], jnp.zeros((L - rows_in, 128), F32)], axis=0)
    else:
        dt_raw = dt_ref[...]
    d_all = jax.nn.softplus(dt_raw + dtb_ref[...])
    if valid_len < L:
        row = lax.broadcasted_iota(jnp.int32, (L, 128), 0)
        d_all = jnp.where(row < valid_len, d_all, 0.0)
    a_row = -jnp.exp(alog_ref[...])
    tril = tril_ref[...]
    cs = jnp.dot(tril, d_all * a_row, preferred_element_type=F32, precision=lax.Precision.HIGHEST)
    st_ref[0] = cs
    st_ref[1] = jnp.exp(cs)
    st_ref[2] = jnp.exp(cs[L - 1:L, :] - cs) * d_all
    tt_ref[0] = cs.T
    tt_ref[1] = d_all.T
    lower = tril > 0.5

    for g in range(G_B):
        b_g = xbc[:, D_INNER_B + g * N_B:D_INNER_B + (g + 1) * N_B]
        c_g = xbc[:, D_INNER_B + G_B * N_B + g * N_B:D_INNER_B + G_B * N_B + (g + 1) * N_B]
        cgb = c_g.astype(BF16)
        cb = lax.dot_general(cgb, b_g.astype(BF16), (((1,), (1,)), ((), ())), preferred_element_type=F32)
        bt = b_g.T.astype(BF16)
        gs = slice(g * 512, (g + 1) * 512)
        hg = ht_ref[0, :, gs]
        yoff = jnp.dot(cgb, hg.astype(BF16), preferred_element_type=F32)
        for hh in range(8):
            h = g * 8 + hh
            hs = slice(h * HD_B, (h + 1) * HD_B)
            x_h = xbc[:, hs]
            seg = st_ref[0, :, h:h + 1] - tt_ref[0, h:h + 1, :]
            decay = jnp.exp(jnp.where(lower, seg, NEG_INF))
            m = (cb * decay * tt_ref[1, h:h + 1, :]).astype(BF16)
            y_h = jnp.dot(m, x_h.astype(BF16), preferred_element_type=F32)
            y_h = y_h + yoff[:, hh * HD_B:(hh + 1) * HD_B] * st_ref[1, :, h:h + 1]
            y_ref[:, hs] = y_h + dskip_ref[:, hs] * x_h
            xw_ref[:, hh * HD_B:(hh + 1) * HD_B] = (x_h * st_ref[2, :, h:h + 1]).astype(BF16)
        upd = jnp.dot(bt, xw_ref[...], preferred_element_type=F32)
        for hh in range(8):
            h = g * 8 + hh
            hs = slice(h * HD_B, (h + 1) * HD_B)
            ht_ref[0, :, hs] = ht_ref[0, :, hs] * st_ref[1, L - 1:L, h:h + 1] + upd[:, hh * HD_B:(hh + 1) * HD_B]

    if rows_in < L:
        z = jnp.concatenate([z_ref[...], jnp.zeros((L - rows_in, D_INNER_B), F32)], axis=0)
    else:
        z = z_ref[...]
    yz = y_ref[...] * (z * jax.nn.sigmoid(z))
    for g in range(G_B):
        gs = slice(g * 512, (g + 1) * 512)
        v = yz[:, gs]
        out = v * lax.rsqrt(jnp.mean(v * v, axis=-1, keepdims=True) + NORM_EPS) * norm_ref[:, gs]
        o_ref[:, gs] = out[0:rows_in].astype(BF16)


def ssd_mixer(proj, prev, conv_w, conv_b, dt_bias, a_log, d_skip, ssm_norm, h0t, nseq, seq_rows, valid_len):
    rows_in = min(seq_rows, SSD_CHUNK)
    nchunk = seq_rows // rows_in
    r = nseq * seq_rows
    tril = jnp.asarray(np.tril(np.ones((SSD_CHUNK, SSD_CHUNK), np.float32)))
    pad128 = lambda v: jnp.pad(v.astype(F32), (0, 128 - v.shape[0])).reshape(1, 128)
    kernel = functools.partial(_ssd_kernel, rows_in, min(valid_len, SSD_CHUNK))
    rowblk = lambda n, c: n * nchunk + c
    const = lambda n, c: (0, 0)
    return pl.pallas_call(
        kernel,
        grid=(nseq, nchunk),
        in_specs=[pl.BlockSpec((rows_in, D_INNER_B), lambda n, c: (rowblk(n, c), C_XBC // 2048)),
                  pl.BlockSpec((rows_in, 512), lambda n, c: (rowblk(n, c), (C_XBC + 2048) // 512)),
                  pl.BlockSpec((rows_in, 512), lambda n, c: (rowblk(n, c), (C_XBC + 2560) // 512)),
                  pl.BlockSpec((rows_in, D_INNER_B), lambda n, c: (rowblk(n, c), C_Z // 2048)),
                  pl.BlockSpec((rows_in, 128), lambda n, c: (rowblk(n, c), C_DT // 128)),
                  pl.BlockSpec((1, 8, XBC_DIM), lambda n, c: (n, 0, 0)),
                  pl.BlockSpec((8, XBC_DIM), const),
                  pl.BlockSpec((1, XBC_DIM), const),
                  pl.BlockSpec((1, 128), const),
                  pl.BlockSpec((1, 128), const),
                  pl.BlockSpec((1, D_INNER_B), const),
                  pl.BlockSpec((1, D_INNER_B), const),
                  pl.BlockSpec((1, N_B, D_INNER_B), lambda n, c: (n, 0, 0)),
                  pl.BlockSpec((SSD_CHUNK, SSD_CHUNK), const)],
        out_specs=[pl.BlockSpec((rows_in, D_INNER_B), lambda n, c: (rowblk(n, c), 0)),
                   pl.BlockSpec((1, N_B, D_INNER_B), lambda n, c: (n, 0, 0))],
        out_shape=[jax.ShapeDtypeStruct((r, D_INNER_B), BF16),
                   jax.ShapeDtypeStruct((nseq, N_B, D_INNER_B), F32)],
        scratch_shapes=[pltpu.VMEM((SSD_CHUNK + 8, XBC_DIM), F32),
                        pltpu.VMEM((3, SSD_CHUNK, 128), F32),
                        pltpu.VMEM((2, 128, SSD_CHUNK), F32),
                        pltpu.VMEM((SSD_CHUNK, D_INNER_B), F32),
                        pltpu.VMEM((SSD_CHUNK, 512), BF16)],
        compiler_params=_cp("arbitrary", "arbitrary"),
        name="ssd_mixer",
    )(proj, proj, proj, proj, proj, prev, conv_w, conv_b.reshape(1, XBC_DIM), pad128(dt_bias), pad128(a_log),
      jnp.repeat(d_skip.astype(F32), HD_B).reshape(1, D_INNER_B), ssm_norm.reshape(1, D_INNER_B), h0t, tril)


def _cmlp_kernel(rows_in, u_ref, v_ref, lg_ref, lb_ref, ws_ref, bst_ref, o_ref, vn_ref):
    L = CHUNK_C
    v = _gelu(v_ref[...])
    mu = jnp.mean(v, axis=-1, keepdims=True)
    vc = v - mu
    var = jnp.mean(vc * vc, axis=-1, keepdims=True)
    vn = vc * lax.rsqrt(var + NORM_EPS) * lg_ref[...] + lb_ref[...]
    vn_ref[...] = vn
    u = _gelu(u_ref[...])
    if rows_in < L:
        vn = jnp.concatenate([vn, jnp.zeros((L - rows_in, D_C), F32)], axis=0)
    vnb = vn.astype(BF16)
    row = lax.broadcasted_iota(jnp.int32, (L, L), 0)
    col = lax.broadcasted_iota(jnp.int32, (L, L), 1)
    gw = D_C // G_C
    for g in range(G_C):
        w = jnp.where(col <= row, ws_ref[g], 0.0).astype(BF16)
        s = jnp.dot(w, vnb[:, g * gw:(g + 1) * gw], preferred_element_type=F32) + bst_ref[:, g:g + 1]
        o_ref[:, g * gw:(g + 1) * gw] = (u[:, g * gw:(g + 1) * gw] * s[0:rows_in]).astype(BF16)


def chunk_mlp_mixer(proj, ln_g, ln_b, w_s, b_s, seq_rows):
    r = proj.shape[0]
    rows_in = min(seq_rows, CHUNK_C)
    kernel = functools.partial(_cmlp_kernel, rows_in)
    bst = jnp.pad(b_s.T.astype(F32), ((0, 0), (0, 128 - G_C)))
    return pl.pallas_call(
        kernel,
        grid=(r // rows_in,),
        in_specs=[pl.BlockSpec((rows_in, D_C), lambda i: (i, C_U // 2048)),
                  pl.BlockSpec((rows_in, D_C), lambda i: (i, C_V // 2048)),
                  pl.BlockSpec((1, D_C), lambda i: (0, 0)),
                  pl.BlockSpec((1, D_C), lambda i: (0, 0)),
                  pl.BlockSpec((G_C, CHUNK_C, CHUNK_C), lambda i: (0, 0, 0)),
                  pl.BlockSpec((CHUNK_C, 128), lambda i: (0, 0))],
        out_specs=[pl.BlockSpec((rows_in, D_C), lambda i: (i, 0)),
                   pl.BlockSpec((rows_in, D_C), lambda i: (i, 0))],
        out_shape=[jax.ShapeDtypeStruct((r, D_C), BF16), jax.ShapeDtypeStruct((r, D_C), F32)],
        compiler_params=_cp("arbitrary"),
        name="chunk_mlp_mixer",
    )(proj, proj, ln_g.reshape(1, D_C), ln_b.reshape(1, D_C), w_s, bst)


ROW_W = 2 * KV_A * HD_A
SCALE = HD_A ** -0.5
MASK_BIG = 1.0e30
M_INIT = -2.0e30
EXP2_SCALE = SCALE * math.log2(math.e)


CMP_ROWS = 4096
CHUNK_PAGES = CMP_ROWS // PAGE_SIZE
BLK_PER_PAGE = PAGE_SIZE // CMP_BLK


def _paged_chunk_pipeline(pt_ref, copies_for, n_chunks_per_seq):
    n = pl.program_id(0)
    k = pl.program_id(1)
    step = n * n_chunks_per_seq + k
    total = pl.num_programs(0) * n_chunks_per_seq
    slot = step % 2

    @pl.when(step == 0)
    def _():
        for cp in copies_for(n, k, slot):
            cp.start()

    @pl.when(step + 1 < total)
    def _():
        wrap = k + 1 == n_chunks_per_seq
        for cp in copies_for(jnp.where(wrap, n + 1, n), jnp.where(wrap, 0, k + 1), 1 - slot):
            cp.start()

    for cp in copies_for(n, k, slot):
        cp.wait()
    return slot


def _cmpmlp_paged_kernel(layer, pt_ref, cache_ref, pe_ref, w1_ref, w2_ref, kc_ref, vc_ref, buf_ref, sem):
    nkc = pl.num_programs(1)

    def copies_for(n, k, slot):
        return [pltpu.make_async_copy(cache_ref.at[layer, pt_ref[n, k * CHUNK_PAGES + p]],
                                      buf_ref.at[slot, :, :, pl.ds(p * 8, 8), :], sem.at[slot])
                for p in range(CHUNK_PAGES)]

    slot = _paged_chunk_pipeline(pt_ref, copies_for, nkc)
    for c in range(2):
        xs = [(buf_ref[slot, c, s] + pe_ref[s * 2 + c]).astype(BF16) for s in range(CMP_BLK)]
        acc = jnp.dot(jnp.concatenate(xs, axis=1), w1_ref[c], preferred_element_type=F32)
        out = jnp.dot(_gelu(acc).astype(BF16), w2_ref[c], preferred_element_type=F32)
        dst = kc_ref if c == 0 else vc_ref
        dst[0] = out.reshape(CHUNK_PAGES, KV_A * BLK_PER_PAGE, HD_A)


def compress_kv_paged(page_table, cache_t, layer, pe, w1, w2):
    nseq, npages = page_table.shape
    nkc = npages // CHUNK_PAGES
    rows8 = KV_A * BLK_PER_PAGE
    shape = jax.ShapeDtypeStruct((nseq, npages, rows8, HD_A), F32)
    ospec = pl.BlockSpec((1, CHUNK_PAGES, rows8, HD_A), lambda n, k, pt: (n, k, 0, 0))
    kernel = functools.partial(_cmpmlp_paged_kernel, layer)
    kc, vc = pl.pallas_call(
        kernel,
        grid_spec=pltpu.PrefetchScalarGridSpec(
            num_scalar_prefetch=1, grid=(nseq, nkc),
            in_specs=[pl.BlockSpec(memory_space=pl.ANY),
                      pl.BlockSpec((CMP_BLK * 2, 1, HD_A), lambda n, k, pt: (0, 0, 0)),
                      pl.BlockSpec((2, CMP_BLK * HD_A, HD_A), lambda n, k, pt: (0, 0, 0)),
                      pl.BlockSpec((2, HD_A, HD_A), lambda n, k, pt: (0, 0, 0))],
            out_specs=[ospec, ospec],
            scratch_shapes=[pltpu.VMEM((2, 2, CMP_BLK, CHUNK_PAGES * rows8, HD_A), F32),
                            pltpu.SemaphoreType.DMA((2,))]),
        out_shape=[shape, shape],
        compiler_params=_cp("arbitrary", "arbitrary"),
        name="compress_kv_paged",
    )(page_table, cache_t, pe.reshape(CMP_BLK * 2, 1, HD_A), w1.reshape(2, CMP_BLK * HD_A, HD_A), w2)

    def regroup(x):
        x = x.reshape(nseq, npages, KV_A, BLK_PER_PAGE, HD_A).transpose(0, 2, 1, 3, 4)
        return x.reshape(nseq, KV_A, npages * BLK_PER_PAGE, HD_A).astype(BF16)

    return regroup(kc), regroup(vc)


def _cmpmlp_kernel(x00_ref, x01_ref, x10_ref, x11_ref, pe_ref, w1_ref, w2_ref, kc_ref, vc_ref):
    x_refs = ((x00_ref, x01_ref), (x10_ref, x11_ref))
    nblk = x00_ref.shape[1] // CMP_BLK
    for c in range(2):
        def body(s, acc):
            xs = [x_refs[c][g][0, pl.ds(s, nblk, stride=CMP_BLK), :] for g in range(KV_A)]
            x = jnp.concatenate(xs, axis=0) + pe_ref[s * 2 + c]
            return acc + jnp.dot(x.astype(BF16), w1_ref[c, s], preferred_element_type=F32)

        acc = lax.fori_loop(0, CMP_BLK, body, jnp.zeros((KV_A * nblk, HD_A), F32))
        out = jnp.dot(_gelu(acc).astype(BF16), w2_ref[c], preferred_element_type=F32)
        dst = kc_ref if c == 0 else vc_ref
        for g in range(KV_A):
            dst[0, g] = out[g * nblk:(g + 1) * nblk].astype(BF16)


def compress_kv(src, lead, col_blk, nseq, seq_rows, pe, w1, w2):
    rows = min(CMP_ROWS, seq_rows)
    nchunk = seq_rows // rows
    nb = seq_rows // CMP_BLK
    shape = jax.ShapeDtypeStruct((nseq, KV_A, nb, HD_A), BF16)
    ospec = pl.BlockSpec((1, KV_A, rows // CMP_BLK, HD_A), lambda n, c: (n, 0, c, 0))

    def xspec(part):
        return pl.BlockSpec((1, rows, HD_A), lambda n, c: (lead, n * nchunk + c, col_blk * (ROW_W // HD_A) + part))

    return pl.pallas_call(
        _cmpmlp_kernel,
        grid=(nseq, nchunk),
        in_specs=[xspec(0), xspec(1), xspec(2), xspec(3),
                  pl.BlockSpec((CMP_BLK * 2, 1, HD_A), lambda n, c: (0, 0, 0)),
                  pl.BlockSpec((2, CMP_BLK, HD_A, HD_A), lambda n, c: (0, 0, 0, 0)),
                  pl.BlockSpec((2, HD_A, HD_A), lambda n, c: (0, 0, 0))],
        out_specs=[ospec, ospec],
        out_shape=[shape, shape],
        compiler_params=_cp("arbitrary", "arbitrary"),
        name="compress_kv",
    )(src, src, src, src, pe.reshape(CMP_BLK * 2, 1, HD_A), w1, w2)


def _cmpattn_kernel(q_pos0, ns, blocks_major, rb_ref, q_ref, kc_ref, vc_ref, pool_ref, o_ref, sel_ref, tb_ref):
    g = pl.program_id(1)
    qt = pl.program_id(2)
    tq = q_ref.shape[0]
    nb = kc_ref.shape[2]
    nsp = pool_ref.shape[0]
    q0 = q_pos0 + qt * tq
    qpos = q0 + lax.broadcasted_iota(jnp.int32, (tq, nb), 0)
    dist = qpos - (lax.broadcasted_iota(jnp.int32, (tq, nb), 1) * CMP_BLK + (CMP_BLK - 1))
    vis = dist >= 0
    anyvis = ((q0 + lax.broadcasted_iota(jnp.int32, (tq, 1), 0)) >= CMP_BLK - 1).astype(F32)
    kc = kc_ref[0, 0]
    vc = vc_ref[0, 0]
    near = (MAX_DIST + CMP_BLK - 1) // CMP_BLK
    shifted = q_pos0 % tq == 0 and tq % CMP_BLK == 0 and nb == 128 and tq // CMP_BLK + near <= 128
    if shifted:
        @pl.when(qt == 0)
        def _():
            qi = lax.broadcasted_iota(jnp.int32, (tq, nb), 0)
            c = lax.broadcasted_iota(jnp.int32, (tq, nb), 1)
            d = jnp.where(c < near + tq // CMP_BLK, qi + near * CMP_BLK - c * CMP_BLK - (CMP_BLK - 1), 4 * MAX_DIST)
            for h in range(HPG_A):
                tb_ref[h] = _t5_bias(d, rb_ref, g * HPG_A + h)

        shift = lax.rem(q0 // CMP_BLK - near + nb, nb)
    psum = jnp.zeros((tq, nb), F32)
    for h in range(HPG_A):
        hs = slice(h * HD_A, (h + 1) * HD_A)
        lc = lax.dot_general(q_ref[:, hs].astype(BF16), kc, (((1,), (1,)), ((), ())), preferred_element_type=F32)
        if shifted:
            lc = lc * SCALE + pltpu.roll(tb_ref[h], shift, 1)
        else:
            lc = lc * SCALE + _t5_bias(dist, rb_ref, g * HPG_A + h)
        lc = jnp.where(vis, lc, NEG_INF)
        e = jnp.exp(lc - jnp.max(lc, axis=-1, keepdims=True))
        p = e / jnp.sum(e, axis=-1, keepdims=True) * anyvis
        o_ref[:, hs] = jnp.dot(p.astype(BF16), vc, preferred_element_type=F32)
        psum = psum + p

    tqp = max(tq, 128)
    if tq < tqp:
        psum = jnp.concatenate([psum, jnp.zeros((tqp - tq, nb), F32)], axis=0)
    imp = lax.dot_general(pool_ref[...], psum, (((1,), (1,)), ((), ())), preferred_element_type=F32,
                          precision=lax.Precision.HIGHEST)
    j = lax.broadcasted_iota(jnp.int32, (nsp, tqp), 0)
    cur = lax.shift_right_logical(q0 + lax.broadcasted_iota(jnp.int32, (nsp, tqp), 1), 6)
    forced = (j == 0) | (j == cur) | (j == cur - 1)
    score = jnp.where(j <= cur, imp + jnp.where(forced, FORCE_BONUS, 0.0), -1.0)
    score = jnp.where(j < ns, score, -2.0)
    sel = jnp.zeros((nsp, tqp), F32)
    for _ in range(N_SEL):
        top = jnp.max(score, axis=0, keepdims=True)
        idx = jnp.min(jnp.where(score == top, j, 1 << 30), axis=0, keepdims=True)
        hit = j == idx
        sel = jnp.where(hit, 1.0, sel)
        score = jnp.where(hit, -3.0, score)
    if blocks_major:
        sel_ref[0] = sel.astype(BF16)
    else:
        sel_ref[0] = sel.T[0:tq].astype(BF16)


def compressed_attention(proj, kc, vc, rel_bias, nseq, seq_rows, tq, q_pos0, ns, blocks_major):
    r = nseq * seq_rows
    nb = kc.shape[2]
    nsp = -(-ns // 128) * 128
    nqt = seq_rows // tq
    pool = np.zeros((nsp, nb), np.float32)
    pool[np.arange(nb) // (SLC_BLK // CMP_BLK), np.arange(nb)] = 1.0
    kernel = functools.partial(_cmpattn_kernel, q_pos0, ns, blocks_major)
    kvspec = pl.BlockSpec((1, 1, nb, HD_A), lambda n, g, t: (n, g, 0, 0))
    if blocks_major:
        assert tq % 128 == 0
        sel_spec = pl.BlockSpec((1, nsp, tq), lambda n, g, t: (g, 0, n * nqt + t))
        sel_shape = jax.ShapeDtypeStruct((KV_A, nsp, r), BF16)
    else:
        sel_spec = pl.BlockSpec((1, tq, nsp), lambda n, g, t: (g, n * nqt + t, 0))
        sel_shape = jax.ShapeDtypeStruct((KV_A, r, nsp), BF16)
    return pl.pallas_call(
        kernel,
        grid=(nseq, KV_A, nqt),
        in_specs=[pl.BlockSpec(memory_space=pltpu.SMEM),
                  pl.BlockSpec((tq, HPG_A * HD_A), lambda n, g, t: (n * nqt + t, g)),
                  kvspec, kvspec,
                  pl.BlockSpec((nsp, nb), lambda n, g, t: (0, 0))],
        out_specs=[pl.BlockSpec((tq, HPG_A * HD_A), lambda n, g, t: (n * nqt + t, g)), sel_spec],
        out_shape=[jax.ShapeDtypeStruct((r, H_A * HD_A), F32), sel_shape],
        scratch_shapes=[pltpu.VMEM((HPG_A, tq, 128), F32)],
        compiler_params=_cp("arbitrary", "arbitrary", "arbitrary"),
        name="compressed_attention",
    )(rel_bias, proj, kc, vc, jnp.asarray(pool))


def _softmax_step(s, v, m_ref, l_ref, acc_ref):
    m_prev = m_ref[...]
    m_new = jnp.maximum(m_prev, jnp.max(s, axis=-1, keepdims=True))
    alpha = jnp.exp(m_prev - m_new)
    p = jnp.exp(s - m_new)
    l_ref[...] = alpha * l_ref[...] + jnp.sum(p, axis=-1, keepdims=True)
    acc_ref[...] = alpha * acc_ref[...] + jnp.dot(p.astype(BF16), v, preferred_element_type=F32)
    m_ref[...] = m_new


def _softmax_init(m_ref, l_ref, acc_ref):
    m_ref[...] = jnp.full(m_ref.shape, M_INIT, F32)
    l_ref[...] = jnp.zeros(l_ref.shape, F32)
    acc_ref[...] = jnp.zeros(acc_ref.shape, F32)


def _qk(q, k):
    return lax.dot_general(q, k, (((1,), (1,)), ((), ())), preferred_element_type=F32) * SCALE


PTQ = 128


def _attn_sweep_t(k_ref, q_t, vt_ref, bias_ref, tab_of, first, last, s_ref, p_ref, m_ref, l_ref, acc_ref):
    tq = PTQ

    def scores(kc):
        k0 = pl.multiple_of(kc * tq, tq)
        return jnp.dot(k_ref[pl.ds(k0, tq), :], q_t, preferred_element_type=F32)

    _softmax_init(m_ref, l_ref, acc_ref)
    s_ref[0] = scores(first)
    p_ref[...] = jnp.zeros(p_ref.shape, BF16)

    def body(i, carry):
        kc = first + i
        slot = i % 2
        s = s_ref[slot] + bias_ref[tab_of(kc)]
        s_ref[1 - slot] = scores(jnp.minimum(kc + 1, last))
        pv = jnp.dot(vt_ref[jnp.maximum(kc - 1, first)], p_ref[...], preferred_element_type=F32)
        m_prev = m_ref[...]
        m_new = jnp.maximum(m_prev, jnp.max(s, axis=0, keepdims=True))
        alpha = jnp.exp2((m_prev - m_new) * EXP2_SCALE)
        p = jnp.exp2((s - m_new) * EXP2_SCALE)
        l_ref[...] = alpha * l_ref[...] + jnp.sum(p, axis=0, keepdims=True)
        acc_ref[...] = alpha * (acc_ref[...] + pv)
        p_ref[...] = p.astype(BF16)
        m_ref[...] = m_new
        return carry

    lax.fori_loop(0, last - first + 1, body, 0)
    acc_ref[...] += jnp.dot(vt_ref[last], p_ref[...], preferred_element_type=F32)


def _prompt_attn_kernel(seq, rb_ref, q_ref, sel_ref, ks_ref, vs_ref, kw_ref, vw_ref, oc_ref, gn_ref, o_ref,
                        kaug_ref, vst_ref, kwb_ref, vwt_ref, btab_ref, s_ref, p_ref, m_ref, l_ref, acc_ref, os_ref):
    g = pl.program_id(1)
    qt = pl.program_id(2)
    tq = PTQ

    @pl.when(qt == 0)
    def _():
        kaug_ref[:, 0:HD_A] = ks_ref[...].astype(BF16)
        row = lax.broadcasted_iota(jnp.int32, (seq, 128), 0)
        lane = lax.broadcasted_iota(jnp.int32, (seq, 128), 1)
        kaug_ref[:, HD_A:HD_A + 128] = jnp.where(lax.shift_right_logical(row, 6) == lane, 1.0, 0.0).astype(BF16)
        kwb_ref[...] = kw_ref[...].astype(BF16)

        def transpose_chunk(c, carry):
            r0 = pl.multiple_of(c * tq, tq)
            vst_ref[c] = vs_ref[pl.ds(r0, tq), :].T.astype(BF16)
            vwt_ref[c] = vw_ref[pl.ds(r0, tq), :].T.astype(BF16)
            return carry

        lax.fori_loop(0, seq // tq, transpose_chunk, 0)
        kj = lax.broadcasted_iota(jnp.int32, (tq, tq), 0)
        qi = lax.broadcasted_iota(jnp.int32, (tq, tq), 1)
        for h in range(HPG_A):
            head = g * HPG_A + h
            cols = slice(h * tq, (h + 1) * tq)
            inv = 1.0 / SCALE
            far = jnp.full((tq, tq), rb_ref[N_BUCKETS - 1, head], F32) * inv
            btab_ref[0, :, cols] = _t5_bias(qi - kj + tq, rb_ref, head) * inv
            btab_ref[1, :, cols] = jnp.where(kj <= qi, _t5_bias(qi - kj, rb_ref, head) * inv, NEG_INF)
            btab_ref[2, :, cols] = jnp.where(kj > qi, far, NEG_INF)
            btab_ref[3, :, cols] = far

    selterm = ((sel_ref[0].astype(F32) - 1.0) * MASK_BIG).astype(BF16)
    q_t = jnp.concatenate([q_ref[:, h * HD_A:(h + 1) * HD_A].T.astype(BF16) for h in range(HPG_A)], axis=1)
    q_aug_t = jnp.concatenate([q_t, jnp.concatenate([selterm] * HPG_A, axis=1)], axis=0)

    back = WINDOW // tq

    def slc_tab(kc):
        return jnp.where(kc == qt, 1, jnp.where(kc == qt - 1, 0, 3))

    def win_tab(kc):
        return jnp.where(kc == qt - back, 2, slc_tab(kc))

    _attn_sweep_t(kaug_ref, q_aug_t, vst_ref, btab_ref, slc_tab, 0, qt, s_ref, p_ref, m_ref, l_ref, acc_ref)
    os_ref[...] = acc_ref[...] / l_ref[...]
    _attn_sweep_t(kwb_ref, q_t, vwt_ref, btab_ref, win_tab, jnp.maximum(qt - back, 0), qt, s_ref, p_ref, m_ref,
                  l_ref, acc_ref)
    ow = acc_ref[...] / l_ref[...]

    gates = jax.nn.sigmoid(gn_ref[...])
    gates_t = gates.T
    for h in range(HPG_A):
        lo, hi = 3 * h, 3 * (HPG_A + h)
        g0 = jnp.where(g == 0, gates[:, lo:lo + 1], gates[:, hi:hi + 1])
        g1 = jnp.where(g == 0, gates_t[lo + 1:lo + 2, :], gates_t[hi + 1:hi + 2, :])
        g2 = jnp.where(g == 0, gates_t[lo + 2:lo + 3, :], gates_t[hi + 2:hi + 3, :])
        cols = slice(h * tq, (h + 1) * tq)
        hs = slice(h * HD_A, (h + 1) * HD_A)
        sw = (g1 * os_ref[:, cols] + g2 * ow[:, cols]).T
        o_ref[:, hs] = (g0 * oc_ref[:, hs] + sw).astype(BF16)


def prompt_attention(proj, o_cmp, sel, rel_bias, nseq, seq):
    r = nseq * seq
    nqt = seq // PTQ
    nsp = sel.shape[1]
    assert nsp == 128 and seq % PTQ == 0
    kernel = functools.partial(_prompt_attn_kernel, seq)

    def kv(col):
        return pl.BlockSpec((seq, HD_A), lambda n, g, t: (n, col // HD_A + g))

    qspec = pl.BlockSpec((PTQ, HPG_A * HD_A), lambda n, g, t: (n * nqt + t, g))
    rows8 = HPG_A * PTQ
    return pl.pallas_call(
        kernel,
        grid=(nseq, KV_A, nqt),
        in_specs=[pl.BlockSpec(memory_space=pltpu.SMEM),
                  qspec,
                  pl.BlockSpec((1, nsp, PTQ), lambda n, g, t: (g, 0, n * nqt + t)),
                  kv(C_KVS), kv(C_KVS + KV_A * HD_A), kv(C_KVW), kv(C_KVW + KV_A * HD_A),
                  qspec,
                  pl.BlockSpec((PTQ, 128), lambda n, g, t: (n * nqt + t, C_GN // 128))],
        out_specs=qspec,
        out_shape=jax.ShapeDtypeStruct((r, H_A * HD_A), BF16),
        scratch_shapes=[pltpu.VMEM((seq, 2 * HD_A), BF16), pltpu.VMEM((nqt, HD_A, PTQ), BF16),
                        pltpu.VMEM((seq, HD_A), BF16), pltpu.VMEM((nqt, HD_A, PTQ), BF16),
                        pltpu.VMEM((4, PTQ, rows8), F32), pltpu.VMEM((2, PTQ, rows8), F32),
                        pltpu.VMEM((PTQ, rows8), BF16),
                        pltpu.VMEM((1, rows8), F32), pltpu.VMEM((1, rows8), F32), pltpu.VMEM((HD_A, rows8), F32),
                        pltpu.VMEM((HD_A, rows8), F32)],
        compiler_params=_cp("arbitrary", "arbitrary", "arbitrary"),
        name="prompt_attention",
    )(rel_bias, proj, sel, proj, proj, proj, proj, o_cmp, proj)


SUB = 512


def _sample_attn_kernel(past, layer, pt_ref, rb_ref, q_ref, sel_ref, cache_ref, kn_ref, vn_ref, kwc_ref, vwc_ref,
                        kwn_ref, vwn_ref, oc_ref, gn_ref, o_ref, buf_ref, sem, m_ref, l_ref, acc_ref):
    kc = pl.program_id(1)
    nkc = pl.num_programs(1)
    tq = q_ref.shape[0]
    rows8 = HPG_A * tq
    nsp = sel_ref.shape[2]
    chunk = CHUNK_PAGES * PAGE_SIZE
    ns_last = past // SLC_BLK

    def copies_for(n, k, slot):
        return [pltpu.make_async_copy(cache_ref.at[layer, pt_ref[n, k * CHUNK_PAGES + p]],
                                      buf_ref.at[slot, :, :, pl.ds(p * PAGE_SIZE, PAGE_SIZE), :], sem.at[slot])
                for p in range(CHUNK_PAGES)]

    slot = _paged_chunk_pipeline(pt_ref, copies_for, nkc)
    rowh = lax.shift_right_logical(lax.broadcasted_iota(jnp.int32, (rows8, 1), 0), int(math.log2(tq)))
    qi_s = lax.broadcasted_iota(jnp.int32, (rows8, SUB), 0) & (tq - 1)
    kj_s = lax.broadcasted_iota(jnp.int32, (rows8, SUB), 1)
    gates = jax.nn.sigmoid(gn_ref[...])

    for g in range(KV_A):
        stats = (m_ref.at[g], l_ref.at[g], acc_ref.at[g])
        selterm = ((sel_ref[g].astype(F32) - 1.0) * MASK_BIG).astype(BF16)
        selrows = jnp.concatenate([selterm] * HPG_A, axis=0)
        q_all = jnp.concatenate([q_ref[:, (g * HPG_A + h) * HD_A:(g * HPG_A + h + 1) * HD_A].astype(BF16)
                                 for h in range(HPG_A)], axis=0)
        q_aug = jnp.concatenate([q_all, selrows], axis=1)
        far = jnp.zeros((rows8, 1), F32)
        for h in range(HPG_A):
            far = jnp.where(rowh == h, rb_ref[N_BUCKETS - 1, g * HPG_A + h], far)

        def head_bias(dist):
            return jnp.concatenate([_t5_bias(dist[h * tq:(h + 1) * tq], rb_ref, g * HPG_A + h)
                                    for h in range(HPG_A)], axis=0)

        @pl.when(kc == 0)
        def _():
            _softmax_init(*stats)

        for sub in range(chunk // SUB):
            k = buf_ref[slot, 0, g, sub * SUB:(sub + 1) * SUB, :]
            v = buf_ref[slot, 1, g, sub * SUB:(sub + 1) * SUB, :]
            blk = lax.shift_right_logical(kc * chunk + sub * SUB + lax.broadcasted_iota(jnp.int32, (SUB, nsp), 0), 6)
            onehot = jnp.where(blk == lax.broadcasted_iota(jnp.int32, (SUB, nsp), 1), 1.0, 0.0).astype(BF16)
            s = _qk(q_aug, jnp.concatenate([k, onehot], axis=1))
            if sub < chunk // SUB - 1:
                _softmax_step(s + far, v, *stats)
            else:
                @pl.when(kc < nkc - 1)
                def _():
                    _softmax_step(s + far, v, *stats)

                @pl.when(kc == nkc - 1)
                def _():
                    _softmax_step(s + head_bias(qi_s + SUB - kj_s), v, *stats)

        @pl.when(kc == nkc - 1)
        def _():
            m_g, l_g, acc_g = stats
            gk = slice(g * HD_A, (g + 1) * HD_A)
            qi = lax.broadcasted_iota(jnp.int32, (rows8, 128), 0) & (tq - 1)
            kj = lax.broadcasted_iota(jnp.int32, (rows8, 128), 1)
            pad = jnp.zeros((128 - tq, HD_A), BF16)
            b_new = jnp.where(kj <= qi, head_bias(qi - kj), NEG_INF)
            k_new = jnp.concatenate([kn_ref[:, gk].astype(BF16), pad], axis=0)
            v_new = jnp.concatenate([vn_ref[:, gk].astype(BF16), pad], axis=0)
            s = _qk(q_all, k_new) + b_new + selrows[:, ns_last:ns_last + 1].astype(F32) * SCALE
            _softmax_step(s, v_new, *stats)
            o_s = acc_g[...] / l_g[...]
            _softmax_init(*stats)
            wrows = kwc_ref.shape[2]
            qi_w = lax.broadcasted_iota(jnp.int32, (rows8, wrows), 0) & (tq - 1)
            kj_w = lax.broadcasted_iota(jnp.int32, (rows8, wrows), 1)
            dw = qi_w + wrows - kj_w
            s = _qk(q_all, kwc_ref[0, 0, :, gk].astype(BF16)) + jnp.where(dw < WINDOW, head_bias(dw), NEG_INF)
            _softmax_step(s, vwc_ref[0, 0, :, gk].astype(BF16), *stats)
            k_new = jnp.concatenate([kwn_ref[:, gk].astype(BF16), pad], axis=0)
            v_new = jnp.concatenate([vwn_ref[:, gk].astype(BF16), pad], axis=0)
            _softmax_step(_qk(q_all, k_new) + b_new, v_new, *stats)
            o_w = acc_g[...] / l_g[...]
            for h in range(HPG_A):
                lane = 3 * (g * HPG_A + h)
                rows = slice(h * tq, (h + 1) * tq)
                hs = slice((g * HPG_A + h) * HD_A, (g * HPG_A + h + 1) * HD_A)
                o_ref[:, hs] = (gates[:, lane:lane + 1] * oc_ref[:, hs] + gates[:, lane + 1:lane + 2] * o_s[rows, :]
                                + gates[:, lane + 2:lane + 3] * o_w[rows, :]).astype(BF16)


def sample_attention(proj, o_cmp, sel, page_table, cache_t, layer, cache_win, rel_bias, nseq, tq, past):
    r = nseq * tq
    nsp = sel.shape[2]
    chunk = CHUNK_PAGES * PAGE_SIZE
    nkc = past // chunk
    kernel = functools.partial(_sample_attn_kernel, past, layer)
    wrows = cache_win.shape[2]
    assert wrows == WINDOW
    gw = KV_A * HD_A

    def new_rows(col):
        return pl.BlockSpec((tq, gw), lambda n, k, pt: (n, col // gw))

    def win_rows(c):
        return pl.BlockSpec((1, 1, wrows, gw), lambda n, k, pt: (layer, n, 0, c))

    qspec = pl.BlockSpec((tq, H_A * HD_A), lambda n, k, pt: (n, 0))
    rows8 = HPG_A * tq
    return pl.pallas_call(
        kernel,
        grid_spec=pltpu.PrefetchScalarGridSpec(
            num_scalar_prefetch=1, grid=(nseq, nkc),
            in_specs=[pl.BlockSpec(memory_space=pltpu.SMEM),
                      qspec,
                      pl.BlockSpec((KV_A, tq, nsp), lambda n, k, pt: (0, n, 0)),
                      pl.BlockSpec(memory_space=pl.ANY),
                      new_rows(C_KVS), new_rows(C_KVS + gw),
                      win_rows(0), win_rows(1),
                      new_rows(C_KVW), new_rows(C_KVW + gw),
                      qspec,
                      pl.BlockSpec((tq, 128), lambda n, k, pt: (n, C_GN // 128))],
            out_specs=qspec,
            scratch_shapes=[pltpu.VMEM((2, 2, KV_A, chunk, HD_A), BF16), pltpu.SemaphoreType.DMA((2,)),
                            pltpu.VMEM((KV_A, rows8, 1), F32), pltpu.VMEM((KV_A, rows8, 1), F32),
                            pltpu.VMEM((KV_A, rows8, HD_A), F32)]),
        out_shape=jax.ShapeDtypeStruct((r, H_A * HD_A), BF16),
        compiler_params=_cp("arbitrary", "arbitrary"),
        name="sample_attention",
    )(page_table, rel_bias, proj, sel, cache_t, proj, proj, cache_win, cache_win, proj, proj, o_cmp, proj)


def _prep_w_in(w):
    src = 0
    pieces = []
    for width, dst in _SEGS:
        pieces.append((dst, w[:, src:src + width]))
        src += width
    pieces.sort(key=lambda p: p[0])
    cols, pos = [], 0
    for dst, piece in pieces:
        assert dst >= pos
        if dst > pos:
            cols.append(jnp.zeros((w.shape[0], dst - pos), w.dtype))
        cols.append(piece)
        pos = dst + piece.shape[1]
    if pos < IN_PAD:
        cols.append(jnp.zeros((w.shape[0], IN_PAD - pos), w.dtype))
    return jnp.concatenate(cols, axis=1).astype(BF16)


def _state_rows(state, nseq, width):
    k = state.shape[1]
    return jnp.concatenate([jnp.zeros((nseq, 8 - k, width), F32), state.astype(F32)], axis=1)


def _trunk_layer(x, p, rel_bias, nseq, seq_rows, valid, tiles, attn_fn, conv_b_prev, conv_f_prev, h0t):
    tm, tm_merge, tm_ffn = tiles
    proj = rms_matmul(x, p['norm1'], p['w_in'], tm, 768)
    o_a = attn_fn(proj)
    o_b, h_t = ssd_mixer(proj, conv_b_prev, p['conv_b_w'], p['conv_b_bias'], p['dt_bias'], p['a_log'], p['d_skip'],
                         p['ssm_norm'], h0t, nseq, seq_rows, valid)
    o_c, v_rows = chunk_mlp_mixer(proj, p['v_ln_g'], p['v_ln_b'], p['w_spatial'], p['b_spatial'], seq_rows)
    merged = gated_merge(o_a, o_b, o_c, p['w_br_a'], p['w_br_b'], p['w_br_c'], proj, tm_merge, 512)
    x = matmul_residual(merged, p['w_out'], x, tm, 1024)
    h = rms_matmul(x, p['norm2'], p['w_ff_in'], tm, 1024)
    x = ffn_out(h, conv_f_prev, p['ff_conv_w'], p['ff_conv_b'], p['w_ff_out'], x, seq_rows, tm_ffn, 512)
    return x, proj, h, h_t, v_rows


def kernel(x_prompt, x_sample, cache_cmp_kv, cache_slc_kv, cache_win_kv, state_ssm, state_conv_ssm, state_conv_ffn,
           page_table, rel_bias, norm1, w_in, cmp_pe, cmp_w1, cmp_w2, conv_b_w, conv_b_bias, dt_bias, a_log, d_skip,
           ssm_norm, v_ln_g, v_ln_b, w_spatial, b_spatial, w_br_a, w_br_b, w_br_c, w_out, norm2, w_ff_in,
           ff_conv_w, ff_conv_b, w_ff_out, final_norm):
    bp, seq = x_prompt.shape[:2]
    bs, dec = x_sample.shape[:2]
    depth = w_in.shape[0]
    n_pool = cache_cmp_kv.shape[1]
    past = page_table.shape[1] * PAGE_SIZE
    st = SAMPLE_T
    kv_shape = (2, KV_A, HD_A)

    page_table = page_table.astype(jnp.int32)
    cmp_t = cache_cmp_kv.reshape(depth, n_pool, BLK_PER_PAGE, CMP_BLK, 2, KV_A, HD_A).transpose(0, 1, 4, 3, 5, 2, 6)
    cmp_t = cmp_t.reshape(depth, n_pool, 2, CMP_BLK, KV_A * BLK_PER_PAGE, HD_A)
    slc_t = cache_slc_kv.transpose(0, 1, 3, 4, 2, 5).astype(BF16)
    cache_win = cache_win_kv.reshape(depth, bs, WINDOW, ROW_W)

    xp = x_prompt.reshape(bp * seq, D_MODEL)
    xs = jnp.pad(x_sample, ((0, 0), (0, st - dec), (0, 0))).reshape(bs * st, D_MODEL)
    zeros_b = jnp.zeros((bp, 8, XBC_DIM), F32)
    zeros_f = jnp.zeros((bp, 8, D_FF), F32)
    zeros_h = jnp.zeros((bp, N_B, D_INNER_B), F32)
    ns_p = -(-seq // SLC_BLK)
    ns_s = -(-(past + dec) // SLC_BLK)

    outs = [[] for _ in range(13)]
    for l in range(depth):
        p = {'norm1': norm1[l], 'w_in': _prep_w_in(w_in[l]), 'conv_b_w': jnp.pad(conv_b_w[l], ((0, 8 - CONV_B), (0, 0))),
             'conv_b_bias': conv_b_bias[l], 'dt_bias': dt_bias[l], 'a_log': a_log[l], 'd_skip': d_skip[l],
             'ssm_norm': ssm_norm[l], 'v_ln_g': v_ln_g[l], 'v_ln_b': v_ln_b[l], 'w_spatial': w_spatial[l],
             'b_spatial': b_spatial[l], 'w_br_a': w_br_a[l].astype(BF16), 'w_br_b': w_br_b[l].astype(BF16),
             'w_br_c': w_br_c[l].astype(BF16), 'w_out': w_out[l].astype(BF16), 'norm2': norm2[l],
             'w_ff_in': w_ff_in[l].astype(BF16), 'ff_conv_w': jnp.pad(ff_conv_w[l], ((0, 8 - CONV_F), (0, 0))),
             'ff_conv_b': ff_conv_b[l], 'w_ff_out': w_ff_out[l].astype(BF16)}
        pe, w1, w2 = cmp_pe[l], cmp_w1[l].astype(BF16), cmp_w2[l].astype(BF16)

        def attn_prompt(proj):
            kc, vc = compress_kv(proj.reshape(1, bp * seq, IN_PAD), 0, C_KVC // ROW_W, bp, seq, pe, w1, w2)
            o_cmp, sel = compressed_attention(proj, kc, vc, rel_bias, bp, seq, PTQ, 0, ns_p, True)
            return prompt_attention(proj, o_cmp, sel, rel_bias, bp, seq)

        def attn_sample(proj):
            kc, vc = compress_kv_paged(page_table, cmp_t, l, pe, w1, w2)
            o_cmp, sel = compressed_attention(proj, kc, vc, rel_bias, bs, st, st, past, ns_s, False)
            return sample_attention(proj, o_cmp, sel, page_table, slc_t, l, cache_win, rel_bias, bs, st, past)

        xp, proj, h, h_t, _ = _trunk_layer(xp, p, rel_bias, bp, seq, seq, (1024, 512, 1024), attn_prompt,
                                           zeros_b, zeros_f, zeros_h)
        pr = proj.reshape(bp, seq, IN_PAD)
        outs[0].append(pr[:, :, C_KVC:C_KVC + ROW_W].reshape((bp, seq) + kv_shape))
        outs[1].append(pr[:, :, C_KVS:C_KVS + ROW_W].reshape((bp, seq) + kv_shape))
        outs[2].append(pr[:, seq - WINDOW:, C_KVW:C_KVW + ROW_W].reshape((bp, WINDOW) + kv_shape))
        outs[3].append(h_t.reshape(bp, N_B, H_B, HD_B).transpose(0, 2, 3, 1))
        outs[4].append(pr[:, seq - (CONV_B - 1):, C_XBC:C_XBC + XBC_DIM])
        outs[5].append(h.reshape(bp, seq, 2 * D_FF)[:, seq - (CONV_F - 1):, :D_FF])

        h0t = state_ssm[l].astype(F32).transpose(0, 3, 1, 2).reshape(bs, N_B, D_INNER_B)
        xs, proj, h, h_t, v_rows = _trunk_layer(xs, p, rel_bias, bs, st, dec, (bs * st, bs * st, st), attn_sample,
                                                _state_rows(state_conv_ssm[l], bs, XBC_DIM),
                                                _state_rows(state_conv_ffn[l], bs, D_FF), h0t)
        pr = proj.reshape(bs, st, IN_PAD)
        kvw_new = pr[:, :dec, C_KVW:C_KVW + ROW_W].reshape((bs, dec) + kv_shape)
        outs[6].append(pr[:, :dec, C_KVC:C_KVC + ROW_W].reshape((bs, dec) + kv_shape))
        outs[7].append(pr[:, :dec, C_KVS:C_KVS + ROW_W].reshape((bs, dec) + kv_shape))
        outs[8].append(jnp.concatenate([cache_win_kv[l][:, dec:], kvw_new], axis=1))
        outs[9].append(h_t.reshape(bs, N_B, H_B, HD_B).transpose(0, 2, 3, 1))
        xbc_all = jnp.concatenate([state_conv_ssm[l], pr[:, :dec, C_XBC:C_XBC + XBC_DIM]], axis=1)
        outs[10].append(xbc_all[:, dec:])
        a_all = jnp.concatenate([state_conv_ffn[l], h.reshape(bs, st, 2 * D_FF)[:, :dec, :D_FF]], axis=1)
        outs[11].append(a_all[:, dec:])
        outs[12].append(v_rows.reshape(bs, st, D_C)[:, :dec])

    y_prompt = rms_norm_rows(xp, final_norm, 1024).reshape(bp, seq, D_MODEL)
    y_sample = rms_norm_rows(xs, final_norm, bs * st).reshape(bs, st, D_MODEL)[:, :dec]
    return (y_prompt, y_sample) + tuple(jnp.stack(o) for o in outs)
```

```python
import functools
import math

import numpy as np
import jax
import jax.numpy as jnp
from jax import lax
from jax.experimental import pallas as pl
from jax.experimental.pallas import tpu as pltpu

F32 = jnp.float32
BF16 = jnp.bfloat16

D_MODEL = 2048
DEPTH = 4
PAGE_SIZE = 128
H_A = 16
HD_A = 128
KV_A = 2
HPG_A = H_A // KV_A
CMP_BLK = 32
SLC_BLK = 64
N_SEL = 16
WINDOW = 512
FORCE_BONUS = 1.0e4
NEG_INF = -1.0e30
N_BUCKETS = 32
MAX_DIST = 128
D_INNER_B = 2048
HD_B = 64
H_B = D_INNER_B // HD_B
G_B = 4
N_B = 128
CONV_B = 4
SSD_CHUNK = 128
XBC_DIM = D_INNER_B + 2 * G_B * N_B
D_C = 2048
G_C = 8
CHUNK_C = 128
D_FF = 5632
CONV_F = 3
NORM_EPS = 1e-6

C_Q = 0
C_Z = 2048
C_U = 4096
C_V = 6144
C_GM = 8192
C_XBC = 14336
C_KVC = 17408
C_KVS = 17920
C_KVW = 18432
C_GN = 18944
C_DT = 19072
IN_PAD = 19200
_SEGS = ((H_A * HD_A, C_Q), (512, C_KVC), (512, C_KVS), (512, C_KVW), (3 * H_A, C_GN), (D_INNER_B, C_Z),
         (XBC_DIM, C_XBC), (H_B, C_DT), (D_C, C_U), (D_C, C_V), (3 * D_MODEL, C_GM))

VMEM_LIMIT = 56 * 1024 * 1024
SAMPLE_T = 16


def _cp(*sem):
    return pltpu.CompilerParams(dimension_semantics=sem, vmem_limit_bytes=VMEM_LIMIT)


def _gelu(x):
    return jax.nn.gelu(x)


def _t5_thresholds():
    n = np.arange(0, 4 * MAX_DIST)
    max_exact = N_BUCKETS // 2
    nf = np.maximum(n, 1).astype(np.float64)
    large = max_exact + (np.log(nf / max_exact) / math.log(MAX_DIST / max_exact) * (N_BUCKETS - max_exact)).astype(np.int64)
    b = np.where(n < max_exact, n, np.minimum(large, N_BUCKETS - 1))
    return [int(np.argmax(b >= k)) for k in range(1, N_BUCKETS)]


_T5_THR = _t5_thresholds()


def _t5_bias(dist, rb_ref, head):
    out = jnp.full(dist.shape, rb_ref[0, head], F32)
    for k, thr in enumerate(_T5_THR):
        out = jnp.where(dist >= thr, rb_ref[k + 1, head], out)
    return out


def _rmsmm_kernel(x_ref, g_ref, w_ref, o_ref, xn_ref):
    @pl.when(pl.program_id(1) == 0)
    def _():
        x = x_ref[...]
        y = x * lax.rsqrt(jnp.mean(x * x, axis=-1, keepdims=True) + NORM_EPS)
        xn_ref[...] = (y * g_ref[...]).astype(BF16)

    o_ref[...] = jnp.dot(xn_ref[...], w_ref[...], preferred_element_type=F32)


def rms_matmul(x, g, w, tm, tn):
    r, d = x.shape
    n = w.shape[1]
    return pl.pallas_call(
        _rmsmm_kernel,
        grid=(r // tm, n // tn),
        in_specs=[pl.BlockSpec((tm, d), lambda i, j: (i, 0)),
                  pl.BlockSpec((1, d), lambda i, j: (0, 0)),
                  pl.BlockSpec((d, tn), lambda i, j: (0, j))],
        out_specs=pl.BlockSpec((tm, tn), lambda i, j: (i, j)),
        out_shape=jax.ShapeDtypeStruct((r, n), F32),
        scratch_shapes=[pltpu.VMEM((tm, d), BF16)],
        compiler_params=_cp("arbitrary", "arbitrary"),
        name="rms_matmul",
    )(x, g.reshape(1, d), w)


def _merge_kernel(oa_ref, ob_ref, oc_ref, wa_ref, wb_ref, wc_ref, ga_ref, gb_ref, gc_ref, o_ref):
    acc = jax.nn.sigmoid(ga_ref[...]) * jnp.dot(oa_ref[...], wa_ref[...], preferred_element_type=F32)
    acc = acc + jax.nn.sigmoid(gb_ref[...]) * jnp.dot(ob_ref[...], wb_ref[...], preferred_element_type=F32)
    acc = acc + jax.nn.sigmoid(gc_ref[...]) * jnp.dot(oc_ref[...], wc_ref[...], preferred_element_type=F32)
    o_ref[...] = acc.astype(BF16)


def gated_merge(o_a, o_b, o_c, w_a, w_b, w_c, proj, tm, tn):
    r, d = o_a.shape
    n = w_a.shape[1]
    gblk = C_GM // tn
    per = D_MODEL // tn
    lhs = pl.BlockSpec((tm, d), lambda i, j: (i, 0))
    wsp = pl.BlockSpec((d, tn), lambda i, j: (0, j))

    def gate(b):
        return pl.BlockSpec((tm, tn), lambda i, j: (i, gblk + b * per + j))

    return pl.pallas_call(
        _merge_kernel,
        grid=(r // tm, n // tn),
        in_specs=[lhs, lhs, lhs, wsp, wsp, wsp, gate(0), gate(1), gate(2)],
        out_specs=pl.BlockSpec((tm, tn), lambda i, j: (i, j)),
        out_shape=jax.ShapeDtypeStruct((r, n), BF16),
        compiler_params=_cp("arbitrary", "arbitrary"),
        name="gated_merge",
    )(o_a, o_b, o_c, w_a, w_b, w_c, proj, proj, proj)


def _mmres_kernel(a_ref, w_ref, r_ref, o_ref):
    o_ref[...] = r_ref[...] + jnp.dot(a_ref[...], w_ref[...], preferred_element_type=F32)


def matmul_residual(a, w, res, tm, tn):
    r, k = a.shape
    n = w.shape[1]
    return pl.pallas_call(
        _mmres_kernel,
        grid=(r // tm, n // tn),
        in_specs=[pl.BlockSpec((tm, k), lambda i, j: (i, 0)),
                  pl.BlockSpec((k, tn), lambda i, j: (0, j)),
                  pl.BlockSpec((tm, tn), lambda i, j: (i, j))],
        out_specs=pl.BlockSpec((tm, tn), lambda i, j: (i, j)),
        out_shape=jax.ShapeDtypeStruct((r, n), F32),
        compiler_params=_cp("arbitrary", "arbitrary"),
        name="matmul_residual",
    )(a, w, res)


def _ffn_out_kernel(tiles_per_seq, a_ref, b_ref, prev_ref, cw_ref, cb_ref, w_ref, r_ref, o_ref,
                    ext_ref, carry_ref):
    i = pl.program_id(0)
    k = pl.program_id(1)
    tm = a_ref.shape[0]

    @pl.when(i % tiles_per_seq == 0)
    def _():
        ext_ref[0:8, :] = prev_ref[0]

    @pl.when(i % tiles_per_seq != 0)
    def _():
        ext_ref[0:8, :] = carry_ref[k]

    a = a_ref[...]
    ext_ref[8:8 + tm, :] = a
    carry_ref[k] = a_ref[tm - 8:tm, :]
    cw = cw_ref[...]
    y = cb_ref[...] + ext_ref[6:6 + tm, :] * cw[0:1]
    y = y + ext_ref[7:7 + tm, :] * cw[1:2]
    y = y + a * cw[2:3]
    h = (_gelu(y) * b_ref[...]).astype(BF16)
    part = jnp.dot(h, w_ref[...], preferred_element_type=F32)

    @pl.when(k == 0)
    def _():
        o_ref[...] = r_ref[...] + part

    @pl.when(k != 0)
    def _():
        o_ref[...] += part


def ffn_out(h, prev, conv_w, conv_b, w, res, seq_len, tm, tk):
    r = h.shape[0]
    n = w.shape[1]
    nk = D_FF // tk
    kernel = functools.partial(_ffn_out_kernel, seq_len // tm)
    return pl.pallas_call(
        kernel,
        grid=(r // tm, nk),
        in_specs=[pl.BlockSpec((tm, tk), lambda i, k: (i, k)),
                  pl.BlockSpec((tm, tk), lambda i, k: (i, nk + k)),
                  pl.BlockSpec((1, 8, tk), lambda i, k: (i * tm // seq_len, 0, k)),
                  pl.BlockSpec((8, tk), lambda i, k: (0, k)),
                  pl.BlockSpec((1, tk), lambda i, k: (0, k)),
                  pl.BlockSpec((tk, n), lambda i, k: (k, 0)),
                  pl.BlockSpec((tm, n), lambda i, k: (i, 0))],
        out_specs=pl.BlockSpec((tm, n), lambda i, k: (i, 0)),
        out_shape=jax.ShapeDtypeStruct((r, n), F32),
        scratch_shapes=[pltpu.VMEM((tm + 8, tk), F32), pltpu.VMEM((nk, 8, tk), F32)],
        compiler_params=_cp("arbitrary", "arbitrary"),
        name="ffn_out",
    )(h, h, prev, conv_w, conv_b.reshape(1, D_FF), w, res)


def _rms_kernel(x_ref, g_ref, o_ref):
    x = x_ref[...]
    o_ref[...] = x * lax.rsqrt(jnp.mean(x * x, axis=-1, keepdims=True) + NORM_EPS) * g_ref[...]


def rms_norm_rows(x, g, tm):
    r, d = x.shape
    return pl.pallas_call(
        _rms_kernel,
        grid=(r // tm,),
        in_specs=[pl.BlockSpec((tm, d), lambda i: (i, 0)), pl.BlockSpec((1, d), lambda i: (0, 0))],
        out_specs=pl.BlockSpec((tm, d), lambda i: (i, 0)),
        out_shape=jax.ShapeDtypeStruct((r, d), F32),
        compiler_params=_cp("arbitrary"),
        name="final_rms_norm",
    )(x, g.reshape(1, d))


def _ssd_kernel(rows_in, valid_len, x_ref, b_ref, c_ref, z_ref, dt_ref, prev_ref, cw_ref, cb_ref, dtb_ref, alog_ref,
                dskip_ref, norm_ref, h0_ref, tril_ref, o_ref, ht_ref, ext_ref, st_ref, tt_ref, y_ref, xw_ref):
    L = SSD_CHUNK
    c = pl.program_id(1)

    @pl.when(c == 0)
    def _():
        ext_ref[0:8, :] = prev_ref[0]
        ht_ref[...] = h0_ref[...]

    @pl.when(c != 0)
    def _():
        ext_ref[0:8, :] = ext_ref[L:L + 8, :]

    if rows_in < L:
        ext_ref[8 + rows_in:8 + L, :] = jnp.zeros((L - rows_in, XBC_DIM), F32)
    ext_ref[8:8 + rows_in, 0:D_INNER_B] = x_ref[...]
    ext_ref[8:8 + rows_in, D_INNER_B:D_INNER_B + G_B * N_B] = b_ref[...]
    ext_ref[8:8 + rows_in, D_INNER_B + G_B * N_B:XBC_DIM] = c_ref[...]

    cw = cw_ref[...]
    y = cb_ref[...] + ext_ref[5:5 + L, :] * cw[0:1]
    y = y + ext_ref[6:6 + L, :] * cw[1:2]
    y = y + ext_ref[7:7 + L, :] * cw[2:3]
    y = y + ext_ref[8:8 + L, :] * cw[3:4]
    xbc = y * jax.nn.sigmoid(y)

    if rows_in < L:
        dt_raw = jnp.concatenate([dt_ref[...], jnp.zeros((L - rows_in, 128), F32)], axis=0)
    else:
        dt_raw = dt_ref[...]
    d_all = jax.nn.softplus(dt_raw + dtb_ref[...])
    if valid_len < L:
        row = lax.broadcasted_iota(jnp.int32, (L, 128), 0)
        d_all = jnp.where(row < valid_len, d_all, 0.0)
    a_row = -jnp.exp(alog_ref[...])
    tril = tril_ref[...]
    cs = jnp.dot(tril, d_all * a_row, preferred_element_type=F32, precision=lax.Precision.HIGHEST)
    st_ref[0] = cs
    st_ref[1] = jnp.exp(cs)
    st_ref[2] = jnp.exp(cs[L - 1:L, :] - cs) * d_all
    tt_ref[0] = cs.T
    tt_ref[1] = d_all.T
    lower = tril > 0.5

    for g in range(G_B):
        b_g = xbc[:, D_INNER_B + g * N_B:D_INNER_B + (g + 1) * N_B]
        c_g = xbc[:, D_INNER_B + G_B * N_B + g * N_B:D_INNER_B + G_B * N_B + (g + 1) * N_B]
        cgb = c_g.astype(BF16)
        cb = lax.dot_general(cgb, b_g.astype(BF16), (((1,), (1,)), ((), ())), preferred_element_type=F32)
        bt = b_g.T.astype(BF16)
        gs = slice(g * 512, (g + 1) * 512)
        hg = ht_ref[0, :, gs]
        yoff = jnp.dot(cgb, hg.astype(BF16), preferred_element_type=F32)
        for hh in range(8):
            h = g * 8 + hh
            hs = slice(h * HD_B, (h + 1) * HD_B)
            x_h = xbc[:, hs]
            seg = st_ref[0, :, h:h + 1] - tt_ref[0, h:h + 1, :]
            decay = jnp.exp(jnp.where(lower, seg, NEG_INF))
            m = (cb * decay * tt_ref[1, h:h + 1, :]).astype(BF16)
            y_h = jnp.dot(m, x_h.astype(BF16), preferred_element_type=F32)
            y_h = y_h + yoff[:, hh * HD_B:(hh + 1) * HD_B] * st_ref[1, :, h:h + 1]
            y_ref[:, hs] = y_h + dskip_ref[:, hs] * x_h
            xw_ref[:, hh * HD_B:(hh + 1) * HD_B] = (x_h * st_ref[2, :, h:h + 1]).astype(BF16)
        upd = jnp.dot(bt, xw_ref[...], preferred_element_type=F32)
        for hh in range(8):
            h = g * 8 + hh
            hs = slice(h * HD_B, (h + 1) * HD_B)
            ht_ref[0, :, hs] = ht_ref[0, :, hs] * st_ref[1, L - 1:L, h:h + 1] + upd[:, hh * HD_B:(hh + 1) * HD_B]

    if rows_in < L:
        z = jnp.concatenate([z_ref[...], jnp.zeros((L - rows_in, D_INNER_B), F32)], axis=0)
    else:
        z = z_ref[...]
    yz = y_ref[...] * (z * jax.nn.sigmoid(z))
    for g in range(G_B):
        gs = slice(g * 512, (g + 1) * 512)
        v = yz[:, gs]
        out = v * lax.rsqrt(jnp.mean(v * v, axis=-1, keepdims=True) + NORM_EPS) * norm_ref[:, gs]
        o_ref[:, gs] = out[0:rows_in].astype(BF16)


def ssd_mixer(proj, prev, conv_w, conv_b, dt_bias, a_log, d_skip, ssm_norm, h0t, nseq, seq_rows, valid_len):
    rows_in = min(seq_rows, SSD_CHUNK)
    nchunk = seq_rows // rows_in
    r = nseq * seq_rows
    tril = jnp.asarray(np.tril(np.ones((SSD_CHUNK, SSD_CHUNK), np.float32)))
    pad128 = lambda v: jnp.pad(v.astype(F32), (0, 128 - v.shape[0])).reshape(1, 128)
    kernel = functools.partial(_ssd_kernel, rows_in, min(valid_len, SSD_CHUNK))
    rowblk = lambda n, c: n * nchunk + c
    const = lambda n, c: (0, 0)
    return pl.pallas_call(
        kernel,
        grid=(nseq, nchunk),
        in_specs=[pl.BlockSpec((rows_in, D_INNER_B), lambda n, c: (rowblk(n, c), C_XBC // 2048)),
                  pl.BlockSpec((rows_in, 512), lambda n, c: (rowblk(n, c), (C_XBC + 2048) // 512)),
                  pl.BlockSpec((rows_in, 512), lambda n, c: (rowblk(n, c), (C_XBC + 2560) // 512)),
                  pl.BlockSpec((rows_in, D_INNER_B), lambda n, c: (rowblk(n, c), C_Z // 2048)),
                  pl.BlockSpec((rows_in, 128), lambda n, c: (rowblk(n, c), C_DT // 128)),
                  pl.BlockSpec((1, 8, XBC_DIM), lambda n, c: (n, 0, 0)),
                  pl.BlockSpec((8, XBC_DIM), const),
                  pl.BlockSpec((1, XBC_DIM), const),
                  pl.BlockSpec((1, 128), const),
                  pl.BlockSpec((1, 128), const),
                  pl.BlockSpec((1, D_INNER_B), const),
                  pl.BlockSpec((1, D_INNER_B), const),
                  pl.BlockSpec((1, N_B, D_INNER_B), lambda n, c: (n, 0, 0)),
                  pl.BlockSpec((SSD_CHUNK, SSD_CHUNK), const)],
        out_specs=[pl.BlockSpec((rows_in, D_INNER_B), lambda n, c: (rowblk(n, c), 0)),
                   pl.BlockSpec((1, N_B, D_INNER_B), lambda n, c: (n, 0, 0))],
        out_shape=[jax.ShapeDtypeStruct((r, D_INNER_B), BF16),
                   jax.ShapeDtypeStruct((nseq, N_B, D_INNER_B), F32)],
        scratch_shapes=[pltpu.VMEM((SSD_CHUNK + 8, XBC_DIM), F32),
                        pltpu.VMEM((3, SSD_CHUNK, 128), F32),
                        pltpu.VMEM((2, 128, SSD_CHUNK), F32),
                        pltpu.VMEM((SSD_CHUNK, D_INNER_B), F32),
                        pltpu.VMEM((SSD_CHUNK, 512), BF16)],
        compiler_params=_cp("arbitrary", "arbitrary"),
        name="ssd_mixer",
    )(proj, proj, proj, proj, proj, prev, conv_w, conv_b.reshape(1, XBC_DIM), pad128(dt_bias), pad128(a_log),
      jnp.repeat(d_skip.astype(F32), HD_B).reshape(1, D_INNER_B), ssm_norm.reshape(1, D_INNER_B), h0t, tril)


def _cmlp_kernel(rows_in, u_ref, v_ref, lg_ref, lb_ref, ws_ref, bst_ref, o_ref, vn_ref):
    L = CHUNK_C
    v = _gelu(v_ref[...])
    mu = jnp.mean(v, axis=-1, keepdims=True)
    vc = v - mu
    var = jnp.mean(vc * vc, axis=-1, keepdims=True)
    vn = vc * lax.rsqrt(var + NORM_EPS) * lg_ref[...] + lb_ref[...]
    vn_ref[...] = vn
    u = _gelu(u_ref[...])
    if rows_in < L:
        vn = jnp.concatenate([vn, jnp.zeros((L - rows_in, D_C), F32)], axis=0)
    vnb = vn.astype(BF16)
    row = lax.broadcasted_iota(jnp.int32, (L, L), 0)
    col = lax.broadcasted_iota(jnp.int32, (L, L), 1)
    gw = D_C // G_C
    for g in range(G_C):
        w = jnp.where(col <= row, ws_ref[g], 0.0).astype(BF16)
        s = jnp.dot(w, vnb[:, g * gw:(g + 1) * gw], preferred_element_type=F32) + bst_ref[:, g:g + 1]
        o_ref[:, g * gw:(g + 1) * gw] = (u[:, g * gw:(g + 1) * gw] * s[0:rows_in]).astype(BF16)


def chunk_mlp_mixer(proj, ln_g, ln_b, w_s, b_s, seq_rows):
    r = proj.shape[0]
    rows_in = min(seq_rows, CHUNK_C)
    kernel = functools.partial(_cmlp_kernel, rows_in)
    bst = jnp.pad(b_s.T.astype(F32), ((0, 0), (0, 128 - G_C)))
    return pl.pallas_call(
        kernel,
        grid=(r // rows_in,),
        in_specs=[pl.BlockSpec((rows_in, D_C), lambda i: (i, C_U // 2048)),
                  pl.BlockSpec((rows_in, D_C), lambda i: (i, C_V // 2048)),
                  pl.BlockSpec((1, D_C), lambda i: (0, 0)),
                  pl.BlockSpec((1, D_C), lambda i: (0, 0)),
                  pl.BlockSpec((G_C, CHUNK_C, CHUNK_C), lambda i: (0, 0, 0)),
                  pl.BlockSpec((CHUNK_C, 128), lambda i: (0, 0))],
        out_specs=[pl.BlockSpec((rows_in, D_C), lambda i: (i, 0)),
                   pl.BlockSpec((rows_in, D_C), lambda i: (i, 0))],
        out_shape=[jax.ShapeDtypeStruct((r, D_C), BF16), jax.ShapeDtypeStruct((r, D_C), F32)],
        compiler_params=_cp("arbitrary"),
        name="chunk_mlp_mixer",
    )(proj, proj, ln_g.reshape(1, D_C), ln_b.reshape(1, D_C), w_s, bst)


ROW_W = 2 * KV_A * HD_A
SCALE = HD_A ** -0.5
MASK_BIG = 1.0e30
M_INIT = -2.0e30
EXP2_SCALE = SCALE * math.log2(math.e)


CMP_ROWS = 4096
CHUNK_PAGES = CMP_ROWS // PAGE_SIZE
BLK_PER_PAGE = PAGE_SIZE // CMP_BLK


def _paged_chunk_pipeline(pt_ref, copies_for, n_chunks_per_seq):
    n = pl.program_id(0)
    k = pl.program_id(1)
    step = n * n_chunks_per_seq + k
    total = pl.num_programs(0) * n_chunks_per_seq
    slot = step % 2

    @pl.when(step == 0)
    def _():
        for cp in copies_for(n, k, slot):
            cp.start()

    @pl.when(step + 1 < total)
    def _():
        wrap = k + 1 == n_chunks_per_seq
        for cp in copies_for(jnp.where(wrap, n + 1, n), jnp.where(wrap, 0, k + 1), 1 - slot):
            cp.start()

    for cp in copies_for(n, k, slot):
        cp.wait()
    return slot


def _cmpmlp_paged_kernel(layer, pt_ref, cache_ref, pe_ref, w1_ref, w2_ref, kc_ref, vc_ref, buf_ref, sem):
    nkc = pl.num_programs(1)

    def copies_for(n, k, slot):
        return [pltpu.make_async_copy(cache_ref.at[layer, pt_ref[n, k * CHUNK_PAGES + p]],
                                      buf_ref.at[slot, :, :, pl.ds(p * 8, 8), :], sem.at[slot])
                for p in range(CHUNK_PAGES)]

    slot = _paged_chunk_pipeline(pt_ref, copies_for, nkc)
    for c in range(2):
        xs = [(buf_ref[slot, c, s] + pe_ref[s * 2 + c]).astype(BF16) for s in range(CMP_BLK)]
        acc = jnp.dot(jnp.concatenate(xs, axis=1), w1_ref[c], preferred_element_type=F32)
        out = jnp.dot(_gelu(acc).astype(BF16), w2_ref[c], preferred_element_type=F32)
        dst = kc_ref if c == 0 else vc_ref
        dst[0] = out.reshape(CHUNK_PAGES, KV_A * BLK_PER_PAGE, HD_A)


def compress_kv_paged(page_table, cache_t, layer, pe, w1, w2):
    nseq, npages = page_table.shape
    nkc = npages // CHUNK_PAGES
    rows8 = KV_A * BLK_PER_PAGE
    shape = jax.ShapeDtypeStruct((nseq, npages, rows8, HD_A), F32)
    ospec = pl.BlockSpec((1, CHUNK_PAGES, rows8, HD_A), lambda n, k, pt: (n, k, 0, 0))
    kernel = functools.partial(_cmpmlp_paged_kernel, layer)
    kc, vc = pl.pallas_call(
        kernel,
        grid_spec=pltpu.PrefetchScalarGridSpec(
            num_scalar_prefetch=1, grid=(nseq, nkc),
            in_specs=[pl.BlockSpec(memory_space=pl.ANY),
                      pl.BlockSpec((CMP_BLK * 2, 1, HD_A), lambda n, k, pt: (0, 0, 0)),
                      pl.BlockSpec((2, CMP_BLK * HD_A, HD_A), lambda n, k, pt: (0, 0, 0)),
                      pl.BlockSpec((2, HD_A, HD_A), lambda n, k, pt: (0, 0, 0))],
            out_specs=[ospec, ospec],
            scratch_shapes=[pltpu.VMEM((2, 2, CMP_BLK, CHUNK_PAGES * rows8, HD_A), F32),
                            pltpu.SemaphoreType.DMA((2,))]),
        out_shape=[shape, shape],
        compiler_params=_cp("arbitrary", "arbitrary"),
        name="compress_kv_paged",
    )(page_table, cache_t, pe.reshape(CMP_BLK * 2, 1, HD_A), w1.reshape(2, CMP_BLK * HD_A, HD_A), w2)

    def regroup(x):
        x = x.reshape(nseq, npages, KV_A, BLK_PER_PAGE, HD_A).transpose(0, 2, 1, 3, 4)
        return x.reshape(nseq, KV_A, npages * BLK_PER_PAGE, HD_A).astype(BF16)

    return regroup(kc), regroup(vc)


def _cmpmlp_kernel(x00_ref, x01_ref, x10_ref, x11_ref, pe_ref, w1_ref, w2_ref, kc_ref, vc_ref):
    x_refs = ((x00_ref, x01_ref), (x10_ref, x11_ref))
    nblk = x00_ref.shape[1] // CMP_BLK
    for c in range(2):
        def body(s, acc):
            xs = [x_refs[c][g][0, pl.ds(s, nblk, stride=CMP_BLK), :] for g in range(KV_A)]
            x = jnp.concatenate(xs, axis=0) + pe_ref[s * 2 + c]
            return acc + jnp.dot(x.astype(BF16), w1_ref[c, s], preferred_element_type=F32)

        acc = lax.fori_loop(0, CMP_BLK, body, jnp.zeros((KV_A * nblk, HD_A), F32))
        out = jnp.dot(_gelu(acc).astype(BF16), w2_ref[c], preferred_element_type=F32)
        dst = kc_ref if c == 0 else vc_ref
        for g in range(KV_A):
            dst[0, g] = out[g * nblk:(g + 1) * nblk].astype(BF16)


def compress_kv(src, lead, col_blk, nseq, seq_rows, pe, w1, w2):
    rows = min(CMP_ROWS, seq_rows)
    nchunk = seq_rows // rows
    nb = seq_rows // CMP_BLK
    shape = jax.ShapeDtypeStruct((nseq, KV_A, nb, HD_A), BF16)
    ospec = pl.BlockSpec((1, KV_A, rows // CMP_BLK, HD_A), lambda n, c: (n, 0, c, 0))

    def xspec(part):
        return pl.BlockSpec((1, rows, HD_A), lambda n, c: (lead, n * nchunk + c, col_blk * (ROW_W // HD_A) + part))

    return pl.pallas_call(
        _cmpmlp_kernel,
        grid=(nseq, nchunk),
        in_specs=[xspec(0), xspec(1), xspec(2), xspec(3),
                  pl.BlockSpec((CMP_BLK * 2, 1, HD_A), lambda n, c: (0, 0, 0)),
                  pl.BlockSpec((2, CMP_BLK, HD_A, HD_A), lambda n, c: (0, 0, 0, 0)),
                  pl.BlockSpec((2, HD_A, HD_A), lambda n, c: (0, 0, 0))],
        out_specs=[ospec, ospec],
        out_shape=[shape, shape],
        compiler_params=_cp("arbitrary", "arbitrary"),
        name="compress_kv",
    )(src, src, src, src, pe.reshape(CMP_BLK * 2, 1, HD_A), w1, w2)


def _select_blocks(imp, q0, ns):
    nsp, tqp = imp.shape
    j = lax.broadcasted_iota(jnp.int32, (nsp, tqp), 0)
    cur = lax.shift_right_logical(q0 + lax.broadcasted_iota(jnp.int32, (nsp, tqp), 1), 6)
    forced = (j == 0) | (j == cur) | (j == cur - 1)
    score = jnp.where(j <= cur, imp + jnp.where(forced, FORCE_BONUS, 0.0), -1.0)
    score = jnp.where(j < ns, score, -2.0)
    sel = jnp.zeros((nsp, tqp), F32)
    for _ in range(N_SEL):
        top = jnp.max(score, axis=0, keepdims=True)
        idx = jnp.min(jnp.where(score == top, j, 1 << 30), axis=0, keepdims=True)
        hit = j == idx
        sel = jnp.where(hit, 1.0, sel)
        score = jnp.where(hit, -3.0, score)
    return sel


def _cmp_bias_table(tb_ref, rb_ref, g, tq, nb):
    near = (MAX_DIST + CMP_BLK - 1) // CMP_BLK
    qi = lax.broadcasted_iota(jnp.int32, (tq, nb), 0)
    c = lax.broadcasted_iota(jnp.int32, (tq, nb), 1)
    d = jnp.where(c < near + tq // CMP_BLK, qi + near * CMP_BLK - c * CMP_BLK - (CMP_BLK - 1), 4 * MAX_DIST)
    for h in range(HPG_A):
        tb_ref[h] = _t5_bias(d, rb_ref, g * HPG_A + h)


def _cmp_bias_shift(q0, nb):
    near = (MAX_DIST + CMP_BLK - 1) // CMP_BLK
    return lax.rem(q0 // CMP_BLK - near + nb, nb)


def _cmpattn_kernel(q_pos0, ns, rb_ref, q_ref, kc_ref, vc_ref, pool_ref, o_ref, sel_ref):
    g = pl.program_id(1)
    qt = pl.program_id(2)
    tq = q_ref.shape[0]
    nb = kc_ref.shape[2]
    q0 = q_pos0 + qt * tq
    qpos = q0 + lax.broadcasted_iota(jnp.int32, (tq, nb), 0)
    dist = qpos - (lax.broadcasted_iota(jnp.int32, (tq, nb), 1) * CMP_BLK + (CMP_BLK - 1))
    vis = dist >= 0
    anyvis = ((q0 + lax.broadcasted_iota(jnp.int32, (tq, 1), 0)) >= CMP_BLK - 1).astype(F32)
    kc = kc_ref[0, 0]
    vc = vc_ref[0, 0]
    psum = jnp.zeros((tq, nb), F32)
    for h in range(HPG_A):
        hs = slice(h * HD_A, (h + 1) * HD_A)
        lc = lax.dot_general(q_ref[:, hs].astype(BF16), kc, (((1,), (1,)), ((), ())), preferred_element_type=F32)
        lc = lc * SCALE + _t5_bias(dist, rb_ref, g * HPG_A + h)
        lc = jnp.where(vis, lc, NEG_INF)
        e = jnp.exp(lc - jnp.max(lc, axis=-1, keepdims=True))
        p = e / jnp.sum(e, axis=-1, keepdims=True) * anyvis
        o_ref[:, hs] = jnp.dot(p.astype(BF16), vc, preferred_element_type=F32)
        psum = psum + p

    tqp = max(tq, 128)
    if tq < tqp:
        psum = jnp.concatenate([psum, jnp.zeros((tqp - tq, nb), F32)], axis=0)
    imp = lax.dot_general(pool_ref[...], psum, (((1,), (1,)), ((), ())), preferred_element_type=F32,
                          precision=lax.Precision.HIGHEST)
    sel = _select_blocks(imp, q0, ns)
    sel_ref[0] = sel.T[0:tq].astype(BF16)


def compressed_attention(proj, kc, vc, rel_bias, nseq, seq_rows, tq, q_pos0, ns):
    r = nseq * seq_rows
    nb = kc.shape[2]
    nsp = -(-ns // 128) * 128
    nqt = seq_rows // tq
    pool = np.zeros((nsp, nb), np.float32)
    pool[np.arange(nb) // (SLC_BLK // CMP_BLK), np.arange(nb)] = 1.0
    kernel = functools.partial(_cmpattn_kernel, q_pos0, ns)
    kvspec = pl.BlockSpec((1, 1, nb, HD_A), lambda n, g, t: (n, g, 0, 0))
    sel_spec = pl.BlockSpec((1, tq, nsp), lambda n, g, t: (g, n * nqt + t, 0))
    sel_shape = jax.ShapeDtypeStruct((KV_A, r, nsp), BF16)
    return pl.pallas_call(
        kernel,
        grid=(nseq, KV_A, nqt),
        in_specs=[pl.BlockSpec(memory_space=pltpu.SMEM),
                  pl.BlockSpec((tq, HPG_A * HD_A), lambda n, g, t: (n * nqt + t, g)),
                  kvspec, kvspec,
                  pl.BlockSpec((nsp, nb), lambda n, g, t: (0, 0))],
        out_specs=[pl.BlockSpec((tq, HPG_A * HD_A), lambda n, g, t: (n * nqt + t, g)), sel_spec],
        out_shape=[jax.ShapeDtypeStruct((r, H_A * HD_A), F32), sel_shape],
        compiler_params=_cp("arbitrary", "arbitrary", "arbitrary"),
        name="compressed_attention",
    )(rel_bias, proj, kc, vc, jnp.asarray(pool))


def _softmax_step(s, v, m_ref, l_ref, acc_ref):
    m_prev = m_ref[...]
    m_new = jnp.maximum(m_prev, jnp.max(s, axis=-1, keepdims=True))
    alpha = jnp.exp(m_prev - m_new)
    p = jnp.exp(s - m_new)
    l_ref[...] = alpha * l_ref[...] + jnp.sum(p, axis=-1, keepdims=True)
    acc_ref[...] = alpha * acc_ref[...] + jnp.dot(p.astype(BF16), v, preferred_element_type=F32)
    m_ref[...] = m_new


def _softmax_init(m_ref, l_ref, acc_ref):
    m_ref[...] = jnp.full(m_ref.shape, M_INIT, F32)
    l_ref[...] = jnp.zeros(l_ref.shape, F32)
    acc_ref[...] = jnp.zeros(acc_ref.shape, F32)


def _qk(q, k):
    return lax.dot_general(q, k, (((1,), (1,)), ((), ())), preferred_element_type=F32) * SCALE


PTQ = 128


def _attn_sweep_t(k_ref, q_t, vt_ref, bias_ref, tab_of, first, last, s_ref, p_ref, m_ref, l_ref, acc_ref):
    tq = PTQ

    def scores(kc):
        k0 = pl.multiple_of(kc * tq, tq)
        return jnp.dot(k_ref[pl.ds(k0, tq), :], q_t, preferred_element_type=F32)

    _softmax_init(m_ref, l_ref, acc_ref)
    s_ref[0] = scores(first)
    p_ref[...] = jnp.zeros(p_ref.shape, BF16)

    def body(i, carry):
        kc = first + i
        slot = i % 2
        s = s_ref[slot] + bias_ref[tab_of(kc)]
        s_ref[1 - slot] = scores(jnp.minimum(kc + 1, last))
        pv = jnp.dot(vt_ref[jnp.maximum(kc - 1, first)], p_ref[...], preferred_element_type=F32)
        m_prev = m_ref[...]
        m_new = jnp.maximum(m_prev, jnp.max(s, axis=0, keepdims=True))
        alpha = jnp.exp2((m_prev - m_new) * EXP2_SCALE)
        p = jnp.exp2((s - m_new) * EXP2_SCALE)
        l_ref[...] = alpha * l_ref[...] + jnp.sum(p, axis=0, keepdims=True)
        acc_ref[...] = alpha * (acc_ref[...] + pv)
        p_ref[...] = p.astype(BF16)
        m_ref[...] = m_new
        return carry

    lax.fori_loop(0, last - first + 1, body, 0)
    acc_ref[...] += jnp.dot(vt_ref[last], p_ref[...], preferred_element_type=F32)


def _prompt_attn_kernel(seq, ns, rb_ref, q_ref, kc_ref, vc_ref, pool_ref, ks_ref, vs_ref, kw_ref, vw_ref, gn_ref,
                        o_ref, kaug_ref, vst_ref, kwb_ref, vwt_ref, btab_ref, tb_ref, s_ref, p_ref, m_ref, l_ref,
                        acc_ref, os_ref, oc_ref):
    g = pl.program_id(1)
    qt = pl.program_id(2)
    tq = PTQ
    nb = kc_ref.shape[2]

    @pl.when(qt == 0)
    def _():
        _cmp_bias_table(tb_ref, rb_ref, g, tq, nb)
        kaug_ref[:, 0:HD_A] = ks_ref[...].astype(BF16)
        row = lax.broadcasted_iota(jnp.int32, (seq, 128), 0)
        lane = lax.broadcasted_iota(jnp.int32, (seq, 128), 1)
        kaug_ref[:, HD_A:HD_A + 128] = jnp.where(lax.shift_right_logical(row, 6) == lane, 1.0, 0.0).astype(BF16)
        kwb_ref[...] = kw_ref[...].astype(BF16)

        def transpose_chunk(c, carry):
            r0 = pl.multiple_of(c * tq, tq)
            vst_ref[c] = vs_ref[pl.ds(r0, tq), :].T.astype(BF16)
            vwt_ref[c] = vw_ref[pl.ds(r0, tq), :].T.astype(BF16)
            return carry

        lax.fori_loop(0, seq // tq, transpose_chunk, 0)
        kj = lax.broadcasted_iota(jnp.int32, (tq, tq), 0)
        qi = lax.broadcasted_iota(jnp.int32, (tq, tq), 1)
        for h in range(HPG_A):
            head = g * HPG_A + h
            cols = slice(h * tq, (h + 1) * tq)
            inv = 1.0 / SCALE
            far = jnp.full((tq, tq), rb_ref[N_BUCKETS - 1, head], F32) * inv
            btab_ref[0, :, cols] = _t5_bias(qi - kj + tq, rb_ref, head) * inv
            btab_ref[1, :, cols] = jnp.where(kj <= qi, _t5_bias(qi - kj, rb_ref, head) * inv, NEG_INF)
            btab_ref[2, :, cols] = jnp.where(kj > qi, far, NEG_INF)
            btab_ref[3, :, cols] = far

    q_t = jnp.concatenate([q_ref[:, h * HD_A:(h + 1) * HD_A].T.astype(BF16) for h in range(HPG_A)], axis=1)

    q0 = qt * tq
    shift = _cmp_bias_shift(q0, nb)
    bias_c = jnp.concatenate([pltpu.roll(tb_ref[h], shift, 1).T for h in range(HPG_A)], axis=1)
    blk = lax.broadcasted_iota(jnp.int32, (nb, tq), 0)
    qpos = q0 + lax.broadcasted_iota(jnp.int32, (nb, tq), 1)
    vis = jnp.concatenate([qpos >= blk * CMP_BLK + (CMP_BLK - 1)] * HPG_A, axis=1)
    anyvis = jnp.concatenate([(qpos[0:1] >= CMP_BLK - 1).astype(F32)] * HPG_A, axis=1)
    lc = jnp.dot(kc_ref[0, 0], q_t, preferred_element_type=F32) * SCALE + bias_c
    lc = jnp.where(vis, lc, NEG_INF)
    e = jnp.exp(lc - jnp.max(lc, axis=0, keepdims=True))
    pc = e / jnp.sum(e, axis=0, keepdims=True) * anyvis
    vc_t = vc_ref[0, 0].astype(F32).T.astype(BF16)
    oc_ref[...] = jnp.dot(vc_t, pc.astype(BF16), preferred_element_type=F32)
    psum = pc[:, 0:tq]
    for h in range(1, HPG_A):
        psum = psum + pc[:, h * tq:(h + 1) * tq]
    imp = jnp.dot(pool_ref[...], psum, preferred_element_type=F32, precision=lax.Precision.HIGHEST)
    sel = _select_blocks(imp, q0, ns)
    selterm = ((sel - 1.0) * MASK_BIG).astype(BF16)
    q_aug_t = jnp.concatenate([q_t, jnp.concatenate([selterm] * HPG_A, axis=1)], axis=0)

    back = WINDOW // tq

    def slc_tab(kc):
        return jnp.where(kc == qt, 1, jnp.where(kc == qt - 1, 0, 3))

    def win_tab(kc):
        return jnp.where(kc == qt - back, 2, slc_tab(kc))

    _attn_sweep_t(kaug_ref, q_aug_t, vst_ref, btab_ref, slc_tab, 0, qt, s_ref, p_ref, m_ref, l_ref, acc_ref)
    os_ref[...] = acc_ref[...] / l_ref[...]
    _attn_sweep_t(kwb_ref, q_t, vwt_ref, btab_ref, win_tab, jnp.maximum(qt - back, 0), qt, s_ref, p_ref, m_ref,
                  l_ref, acc_ref)
    ow = acc_ref[...] / l_ref[...]

    gates = jax.nn.sigmoid(gn_ref[...])
    gates_t = gates.T
    for h in range(HPG_A):
        lo, hi = 3 * h, 3 * (HPG_A + h)
        g0, g1, g2 = (jnp.where(g == 0, gates_t[lo + b:lo + b + 1, :], gates_t[hi + b:hi + b + 1, :])
                      for b in range(3))
        cols = slice(h * tq, (h + 1) * tq)
        mix = g0 * oc_ref[:, cols] + g1 * os_ref[:, cols] + g2 * ow[:, cols]
        o_ref[:, h * HD_A:(h + 1) * HD_A] = mix.T.astype(BF16)


def prompt_attention(proj, kc, vc, rel_bias, nseq, seq):
    r = nseq * seq
    nqt = seq // PTQ
    nb = kc.shape[2]
    ns = -(-seq // SLC_BLK)
    nsp = 128
    assert ns <= nsp and nb == 128 and seq % PTQ == 0
    pool = np.zeros((nsp, nb), np.float32)
    pool[np.arange(nb) // (SLC_BLK // CMP_BLK), np.arange(nb)] = 1.0
    kernel = functools.partial(_prompt_attn_kernel, seq, ns)
    kvspec = pl.BlockSpec((1, 1, nb, HD_A), lambda n, g, t: (n, g, 0, 0))

    def kv(col):
        return pl.BlockSpec((seq, HD_A), lambda n, g, t: (n, col // HD_A + g))

    qspec = pl.BlockSpec((PTQ, HPG_A * HD_A), lambda n, g, t: (n * nqt + t, g))
    rows8 = HPG_A * PTQ
    return pl.pallas_call(
        kernel,
        grid=(nseq, KV_A, nqt),
        in_specs=[pl.BlockSpec(memory_space=pltpu.SMEM),
                  qspec,
                  kvspec, kvspec,
                  pl.BlockSpec((nsp, nb), lambda n, g, t: (0, 0)),
                  kv(C_KVS), kv(C_KVS + KV_A * HD_A), kv(C_KVW), kv(C_KVW + KV_A * HD_A),
                  pl.BlockSpec((PTQ, 128), lambda n, g, t: (n * nqt + t, C_GN // 128))],
        out_specs=qspec,
        out_shape=jax.ShapeDtypeStruct((r, H_A * HD_A), BF16),
        scratch_shapes=[pltpu.VMEM((seq, 2 * HD_A), BF16), pltpu.VMEM((nqt, HD_A, PTQ), BF16),
                        pltpu.VMEM((seq, HD_A), BF16), pltpu.VMEM((nqt, HD_A, PTQ), BF16),
                        pltpu.VMEM((4, PTQ, rows8), F32), pltpu.VMEM((HPG_A, PTQ, nb), F32),
                        pltpu.VMEM((2, PTQ, rows8), F32), pltpu.VMEM((PTQ, rows8), BF16),
                        pltpu.VMEM((1, rows8), F32), pltpu.VMEM((1, rows8), F32), pltpu.VMEM((HD_A, rows8), F32),
                        pltpu.VMEM((HD_A, rows8), F32), pltpu.VMEM((HD_A, rows8), F32)],
        compiler_params=_cp("arbitrary", "arbitrary", "arbitrary"),
        name="prompt_attention",
    )(rel_bias, proj, kc, vc, jnp.asarray(pool), proj, proj, proj, proj, proj)


SUB = 512


def _sample_attn_kernel(past, layer, pt_ref, rb_ref, q_ref, sel_ref, cache_ref, kn_ref, vn_ref, kwc_ref, vwc_ref,
                        kwn_ref, vwn_ref, oc_ref, gn_ref, o_ref, buf_ref, sem, m_ref, l_ref, acc_ref):
    kc = pl.program_id(1)
    nkc = pl.num_programs(1)
    tq = q_ref.shape[0]
    rows8 = HPG_A * tq
    nsp = sel_ref.shape[2]
    chunk = CHUNK_PAGES * PAGE_SIZE
    ns_last = past // SLC_BLK

    def copies_for(n, k, slot):
        return [pltpu.make_async_copy(cache_ref.at[layer, pt_ref[n, k * CHUNK_PAGES + p]],
                                      buf_ref.at[slot, :, :, pl.ds(p * PAGE_SIZE, PAGE_SIZE), :], sem.at[slot])
                for p in range(CHUNK_PAGES)]

    slot = _paged_chunk_pipeline(pt_ref, copies_for, nkc)
    rowh = lax.shift_right_logical(lax.broadcasted_iota(jnp.int32, (rows8, 1), 0), int(math.log2(tq)))
    qi_s = lax.broadcasted_iota(jnp.int32, (rows8, SUB), 0) & (tq - 1)
    kj_s = lax.broadcasted_iota(jnp.int32, (rows8, SUB), 1)
    gates = jax.nn.sigmoid(gn_ref[...])

    for g in range(KV_A):
        stats = (m_ref.at[g], l_ref.at[g], acc_ref.at[g])
        selterm = ((sel_ref[g].astype(F32) - 1.0) * MASK_BIG).astype(BF16)
        selrows = jnp.concatenate([selterm] * HPG_A, axis=0)
        q_all = jnp.concatenate([q_ref[:, (g * HPG_A + h) * HD_A:(g * HPG_A + h + 1) * HD_A].astype(BF16)
                                 for h in range(HPG_A)], axis=0)
        q_aug = jnp.concatenate([q_all, selrows], axis=1)
        far = jnp.zeros((rows8, 1), F32)
        for h in range(HPG_A):
            far = jnp.where(rowh == h, rb_ref[N_BUCKETS - 1, g * HPG_A + h], far)

        def head_bias(dist):
            return jnp.concatenate([_t5_bias(dist[h * tq:(h + 1) * tq], rb_ref, g * HPG_A + h)
                                    for h in range(HPG_A)], axis=0)

        @pl.when(kc == 0)
        def _():
            _softmax_init(*stats)

        for sub in range(chunk // SUB):
            k = buf_ref[slot, 0, g, sub * SUB:(sub + 1) * SUB, :]
            v = buf_ref[slot, 1, g, sub * SUB:(sub + 1) * SUB, :]
            blk = lax.shift_right_logical(kc * chunk + sub * SUB + lax.broadcasted_iota(jnp.int32, (SUB, nsp), 0), 6)
            onehot = jnp.where(blk == lax.broadcasted_iota(jnp.int32, (SUB, nsp), 1), 1.0, 0.0).astype(BF16)
            s = _qk(q_aug, jnp.concatenate([k, onehot], axis=1))
            if sub < chunk // SUB - 1:
                _softmax_step(s + far, v, *stats)
            else:
                @pl.when(kc < nkc - 1)
                def _():
                    _softmax_step(s + far, v, *stats)

                @pl.when(kc == nkc - 1)
                def _():
                    _softmax_step(s + head_bias(qi_s + SUB - kj_s), v, *stats)

        @pl.when(kc == nkc - 1)
        def _():
            m_g, l_g, acc_g = stats
            gk = slice(g * HD_A, (g + 1) * HD_A)
            qi = lax.broadcasted_iota(jnp.int32, (rows8, 128), 0) & (tq - 1)
            kj = lax.broadcasted_iota(jnp.int32, (rows8, 128), 1)
            pad = jnp.zeros((128 - tq, HD_A), BF16)
            b_new = jnp.where(kj <= qi, head_bias(qi - kj), NEG_INF)
            k_new = jnp.concatenate([kn_ref[:, gk].astype(BF16), pad], axis=0)
            v_new = jnp.concatenate([vn_ref[:, gk].astype(BF16), pad], axis=0)
            s = _qk(q_all, k_new) + b_new + selrows[:, ns_last:ns_last + 1].astype(F32) * SCALE
            _softmax_step(s, v_new, *stats)
            o_s = acc_g[...] / l_g[...]
            _softmax_init(*stats)
            wrows = kwc_ref.shape[2]
            qi_w = lax.broadcasted_iota(jnp.int32, (rows8, wrows), 0) & (tq - 1)
            kj_w = lax.broadcasted_iota(jnp.int32, (rows8, wrows), 1)
            dw = qi_w + wrows - kj_w
            s = _qk(q_all, kwc_ref[0, 0, :, gk].astype(BF16)) + jnp.where(dw < WINDOW, head_bias(dw), NEG_INF)
            _softmax_step(s, vwc_ref[0, 0, :, gk].astype(BF16), *stats)
            k_new = jnp.concatenate([kwn_ref[:, gk].astype(BF16), pad], axis=0)
            v_new = jnp.concatenate([vwn_ref[:, gk].astype(BF16), pad], axis=0)
            _softmax_step(_qk(q_all, k_new) + b_new, v_new, *stats)
            o_w = acc_g[...] / l_g[...]
            for h in range(HPG_A):
                lane = 3 * (g * HPG_A + h)
                rows = slice(h * tq, (h + 1) * tq)
                hs = slice((g * HPG_A + h) * HD_A, (g * HPG_A + h + 1) * HD_A)
                o_ref[:, hs] = (gates[:, lane:lane + 1] * oc_ref[:, hs] + gates[:, lane + 1:lane + 2] * o_s[rows, :]
                                + gates[:, lane + 2:lane + 3] * o_w[rows, :]).astype(BF16)


def sample_attention(proj, o_cmp, sel, page_table, cache_t, layer, cache_win, rel_bias, nseq, tq, past):
    r = nseq * tq
    nsp = sel.shape[2]
    chunk = CHUNK_PAGES * PAGE_SIZE
    nkc = past // chunk
    kernel = functools.partial(_sample_attn_kernel, past, layer)
    wrows = cache_win.shape[2]
    assert wrows == WINDOW
    gw = KV_A * HD_A

    def new_rows(col):
        return pl.BlockSpec((tq, gw), lambda n, k, pt: (n, col // gw))

    def win_rows(c):
        return pl.BlockSpec((1, 1, wrows, gw), lambda n, k, pt: (layer, n, 0, c))

    qspec = pl.BlockSpec((tq, H_A * HD_A), lambda n, k, pt: (n, 0))
    rows8 = HPG_A * tq
    return pl.pallas_call(
        kernel,
        grid_spec=pltpu.PrefetchScalarGridSpec(
            num_scalar_prefetch=1, grid=(nseq, nkc),
            in_specs=[pl.BlockSpec(memory_space=pltpu.SMEM),
                      qspec,
                      pl.BlockSpec((KV_A, tq, nsp), lambda n, k, pt: (0, n, 0)),
                      pl.BlockSpec(memory_space=pl.ANY),
                      new_rows(C_KVS), new_rows(C_KVS + gw),
                      win_rows(0), win_rows(1),
                      new_rows(C_KVW), new_rows(C_KVW + gw),
                      qspec,
                      pl.BlockSpec((tq, 128), lambda n, k, pt: (n, C_GN // 128))],
            out_specs=qspec,
            scratch_shapes=[pltpu.VMEM((2, 2, KV_A, chunk, HD_A), BF16), pltpu.SemaphoreType.DMA((2,)),
                            pltpu.VMEM((KV_A, rows8, 1), F32), pltpu.VMEM((KV_A, rows8, 1), F32),
                            pltpu.VMEM((KV_A, rows8, HD_A), F32)]),
        out_shape=jax.ShapeDtypeStruct((r, H_A * HD_A), BF16),
        compiler_params=_cp("arbitrary", "arbitrary"),
        name="sample_attention",
    )(page_table, rel_bias, proj, sel, cache_t, proj, proj, cache_win, cache_win, proj, proj, o_cmp, proj)


def _prep_w_in(w):
    src = 0
    pieces = []
    for width, dst in _SEGS:
        pieces.append((dst, w[:, src:src + width]))
        src += width
    pieces.sort(key=lambda p: p[0])
    cols, pos = [], 0
    for dst, piece in pieces:
        assert dst >= pos
        if dst > pos:
            cols.append(jnp.zeros((w.shape[0], dst - pos), w.dtype))
        cols.append(piece)
        pos = dst + piece.shape[1]
    if pos < IN_PAD:
        cols.append(jnp.zeros((w.shape[0], IN_PAD - pos), w.dtype))
    return jnp.concatenate(cols, axis=1).astype(BF16)


def _state_rows(state, nseq, width):
    k = state.shape[1]
    return jnp.concatenate([jnp.zeros((nseq, 8 - k, width), F32), state.astype(F32)], axis=1)


def _trunk_layer(x, p, rel_bias, nseq, seq_rows, valid, tiles, attn_fn, conv_b_prev, conv_f_prev, h0t):
    tm, tm_merge, tm_ffn = tiles
    proj = rms_matmul(x, p['norm1'], p['w_in'], tm, 768)
    o_a = attn_fn(proj)
    o_b, h_t = ssd_mixer(proj, conv_b_prev, p['conv_b_w'], p['conv_b_bias'], p['dt_bias'], p['a_log'], p['d_skip'],
                         p['ssm_norm'], h0t, nseq, seq_rows, valid)
    o_c, v_rows = chunk_mlp_mixer(proj, p['v_ln_g'], p['v_ln_b'], p['w_spatial'], p['b_spatial'], seq_rows)
    merged = gated_merge(o_a, o_b, o_c, p['w_br_a'], p['w_br_b'], p['w_br_c'], proj, tm_merge, 512)
    x = matmul_residual(merged, p['w_out'], x, tm, 1024)
    h = rms_matmul(x, p['norm2'], p['w_ff_in'], tm, 1024)
    x = ffn_out(h, conv_f_prev, p['ff_conv_w'], p['ff_conv_b'], p['w_ff_out'], x, seq_rows, tm_ffn, 512)
    return x, proj, h, h_t, v_rows


def kernel(x_prompt, x_sample, cache_cmp_kv, cache_slc_kv, cache_win_kv, state_ssm, state_conv_ssm, state_conv_ffn,
           page_table, rel_bias, norm1, w_in, cmp_pe, cmp_w1, cmp_w2, conv_b_w, conv_b_bias, dt_bias, a_log, d_skip,
           ssm_norm, v_ln_g, v_ln_b, w_spatial, b_spatial, w_br_a, w_br_b, w_br_c, w_out, norm2, w_ff_in,
           ff_conv_w, ff_conv_b, w_ff_out, final_norm):
    bp, seq = x_prompt.shape[:2]
    bs, dec = x_sample.shape[:2]
    depth = w_in.shape[0]
    n_pool = cache_cmp_kv.shape[1]
    past = page_table.shape[1] * PAGE_SIZE
    st = SAMPLE_T
    kv_shape = (2, KV_A, HD_A)

    page_table = page_table.astype(jnp.int32)
    cmp_t = cache_cmp_kv.reshape(depth, n_pool, BLK_PER_PAGE, CMP_BLK, 2, KV_A, HD_A).transpose(0, 1, 4, 3, 5, 2, 6)
    cmp_t = cmp_t.reshape(depth, n_pool, 2, CMP_BLK, KV_A * BLK_PER_PAGE, HD_A)
    slc_t = cache_slc_kv.transpose(0, 1, 3, 4, 2, 5).astype(BF16)
    cache_win = cache_win_kv.reshape(depth, bs, WINDOW, ROW_W)

    xp = x_prompt.reshape(bp * seq, D_MODEL)
    xs = jnp.pad(x_sample, ((0, 0), (0, st - dec), (0, 0))).reshape(bs * st, D_MODEL)
    zeros_b = jnp.zeros((bp, 8, XBC_DIM), F32)
    zeros_f = jnp.zeros((bp, 8, D_FF), F32)
    zeros_h = jnp.zeros((bp, N_B, D_INNER_B), F32)
    ns_s = -(-(past + dec) // SLC_BLK)

    outs = [[] for _ in range(13)]
    for l in range(depth):
        p = {'norm1': norm1[l], 'w_in': _prep_w_in(w_in[l]), 'conv_b_w': jnp.pad(conv_b_w[l], ((0, 8 - CONV_B), (0, 0))),
             'conv_b_bias': conv_b_bias[l], 'dt_bias': dt_bias[l], 'a_log': a_log[l], 'd_skip': d_skip[l],
             'ssm_norm': ssm_norm[l], 'v_ln_g': v_ln_g[l], 'v_ln_b': v_ln_b[l], 'w_spatial': w_spatial[l],
             'b_spatial': b_spatial[l], 'w_br_a': w_br_a[l].astype(BF16), 'w_br_b': w_br_b[l].astype(BF16),
             'w_br_c': w_br_c[l].astype(BF16), 'w_out': w_out[l].astype(BF16), 'norm2': norm2[l],
             'w_ff_in': w_ff_in[l].astype(BF16), 'ff_conv_w': jnp.pad(ff_conv_w[l], ((0, 8 - CONV_F), (0, 0))),
             'ff_conv_b': ff_conv_b[l], 'w_ff_out': w_ff_out[l].astype(BF16)}
        pe, w1, w2 = cmp_pe[l], cmp_w1[l].astype(BF16), cmp_w2[l].astype(BF16)

        def attn_prompt(proj):
            kc, vc = compress_kv(proj.reshape(1, bp * seq, IN_PAD), 0, C_KVC // ROW_W, bp, seq, pe, w1, w2)
            return prompt_attention(proj, kc, vc, rel_bias, bp, seq)

        def attn_sample(proj):
            kc, vc = compress_kv_paged(page_table, cmp_t, l, pe, w1, w2)
            o_cmp, sel = compressed_attention(proj, kc, vc, rel_bias, bs, st, st, past, ns_s)
            return sample_attention(proj, o_cmp, sel, page_table, slc_t, l, cache_win, rel_bias, bs, st, past)

        xp, proj, h, h_t, _ = _trunk_layer(xp, p, rel_bias, bp, seq, seq, (1024, 512, 1024), attn_prompt,
                                           zeros_b, zeros_f, zeros_h)
        pr = proj.reshape(bp, seq, IN_PAD)
        outs[0].append(pr[:, :, C_KVC:C_KVC + ROW_W].reshape((bp, seq) + kv_shape))
        outs[1].append(pr[:, :, C_KVS:C_KVS + ROW_W].reshape((bp, seq) + kv_shape))
        outs[2].append(pr[:, seq - WINDOW:, C_KVW:C_KVW + ROW_W].reshape((bp, WINDOW) + kv_shape))
        outs[3].append(h_t.reshape(bp, N_B, H_B, HD_B).transpose(0, 2, 3, 1))
        outs[4].append(pr[:, seq - (CONV_B - 1):, C_XBC:C_XBC + XBC_DIM])
        outs[5].append(h.reshape(bp, seq, 2 * D_FF)[:, seq - (CONV_F - 1):, :D_FF])

        h0t = state_ssm[l].astype(F32).transpose(0, 3, 1, 2).reshape(bs, N_B, D_INNER_B)
        xs, proj, h, h_t, v_rows = _trunk_layer(xs, p, rel_bias, bs, st, dec, (bs * st, bs * st, st), attn_sample,
                                                _state_rows(state_conv_ssm[l], bs, XBC_DIM),
                                                _state_rows(state_conv_ffn[l], bs, D_FF), h0t)
        pr = proj.reshape(bs, st, IN_PAD)
        kvw_new = pr[:, :dec, C_KVW:C_KVW + ROW_W].reshape((bs, dec) + kv_shape)
        outs[6].append(pr[:, :dec, C_KVC:C_KVC + ROW_W].reshape((bs, dec) + kv_shape))
        outs[7].append(pr[:, :dec, C_KVS:C_KVS + ROW_W].reshape((bs, dec) + kv_shape))
        outs[8].append(jnp.concatenate([cache_win_kv[l][:, dec:], kvw_new], axis=1))
        outs[9].append(h_t.reshape(bs, N_B, H_B, HD_B).transpose(0, 2, 3, 1))
        xbc_all = jnp.concatenate([state_conv_ssm[l], pr[:, :dec, C_XBC:C_XBC + XBC_DIM]], axis=1)
        outs[10].append(xbc_all[:, dec:])
        a_all = jnp.concatenate([state_conv_ffn[l], h.reshape(bs, st, 2 * D_FF)[:, :dec, :D_FF]], axis=1)
        outs[11].append(a_all[:, dec:])
        outs[12].append(v_rows.reshape(bs, st, D_C)[:, :dec])

    y_prompt = rms_norm_rows(xp, final_norm, 1024).reshape(bp, seq, D_MODEL)
    y_sample = rms_norm_rows(xs, final_norm, bs * st).reshape(bs, st, D_MODEL)[:, :dec]
    return (y_prompt, y_sample) + tuple(jnp.stack(o) for o in outs)
```

```python
import functools
import math

import numpy as np
import jax
import jax.numpy as jnp
from jax import lax
from jax.experimental import pallas as pl
from jax.experimental.pallas import tpu as pltpu

F32 = jnp.float32
BF16 = jnp.bfloat16

D_MODEL = 2048
DEPTH = 4
PAGE_SIZE = 128
H_A = 16
HD_A = 128
KV_A = 2
HPG_A = H_A // KV_A
CMP_BLK = 32
SLC_BLK = 64
N_SEL = 16
WINDOW = 512
FORCE_BONUS = 1.0e4
NEG_INF = -1.0e30
N_BUCKETS = 32
MAX_DIST = 128
D_INNER_B = 2048
HD_B = 64
H_B = D_INNER_B // HD_B
G_B = 4
N_B = 128
CONV_B = 4
SSD_CHUNK = 128
XBC_DIM = D_INNER_B + 2 * G_B * N_B
D_C = 2048
G_C = 8
CHUNK_C = 128
D_FF = 5632
CONV_F = 3
NORM_EPS = 1e-6

C_Q = 0
C_Z = 2048
C_U = 4096
C_V = 6144
C_GM = 8192
C_XBC = 14336
C_KVC = 17408
C_KVS = 17920
C_KVW = 18432
C_GN = 18944
C_DT = 19072
IN_PAD = 19200
_SEGS = ((H_A * HD_A, C_Q), (512, C_KVC), (512, C_KVS), (512, C_KVW), (3 * H_A, C_GN), (D_INNER_B, C_Z),
         (XBC_DIM, C_XBC), (H_B, C_DT), (D_C, C_U), (D_C, C_V), (3 * D_MODEL, C_GM))

VMEM_LIMIT = 56 * 1024 * 1024
SAMPLE_T = 16


def _cp(*sem):
    return pltpu.CompilerParams(dimension_semantics=sem, vmem_limit_bytes=VMEM_LIMIT)


def _gelu(x):
    return jax.nn.gelu(x)


def _t5_thresholds():
    n = np.arange(0, 4 * MAX_DIST)
    max_exact = N_BUCKETS // 2
    nf = np.maximum(n, 1).astype(np.float64)
    large = max_exact + (np.log(nf / max_exact) / math.log(MAX_DIST / max_exact) * (N_BUCKETS - max_exact)).astype(np.int64)
    b = np.where(n < max_exact, n, np.minimum(large, N_BUCKETS - 1))
    return [int(np.argmax(b >= k)) for k in range(1, N_BUCKETS)]


_T5_THR = _t5_thresholds()


def _t5_bias(dist, rb_ref, head):
    out = jnp.full(dist.shape, rb_ref[0, head], F32)
    for k, thr in enumerate(_T5_THR):
        out = jnp.where(dist >= thr, rb_ref[k + 1, head], out)
    return out


def _rmsmm_kernel(x_ref, g_ref, w_ref, o_ref, xn_ref):
    @pl.when(pl.program_id(1) == 0)
    def _():
        x = x_ref[...]
        y = x * lax.rsqrt(jnp.mean(x * x, axis=-1, keepdims=True) + NORM_EPS)
        xn_ref[...] = (y * g_ref[...]).astype(BF16)

    o_ref[...] = jnp.dot(xn_ref[...], w_ref[...], preferred_element_type=F32)


def rms_matmul(x, g, w, tm, tn):
    r, d = x.shape
    n = w.shape[1]
    return pl.pallas_call(
        _rmsmm_kernel,
        grid=(r // tm, n // tn),
        in_specs=[pl.BlockSpec((tm, d), lambda i, j: (i, 0)),
                  pl.BlockSpec((1, d), lambda i, j: (0, 0)),
                  pl.BlockSpec((d, tn), lambda i, j: (0, j))],
        out_specs=pl.BlockSpec((tm, tn), lambda i, j: (i, j)),
        out_shape=jax.ShapeDtypeStruct((r, n), F32),
        scratch_shapes=[pltpu.VMEM((tm, d), BF16)],
        compiler_params=_cp("arbitrary", "arbitrary"),
        name="rms_matmul",
    )(x, g.reshape(1, d), w)


def _merge_kernel(oa_ref, ob_ref, oc_ref, wa_ref, wb_ref, wc_ref, ga_ref, gb_ref, gc_ref, o_ref):
    acc = jax.nn.sigmoid(ga_ref[...]) * jnp.dot(oa_ref[...], wa_ref[...], preferred_element_type=F32)
    acc = acc + jax.nn.sigmoid(gb_ref[...]) * jnp.dot(ob_ref[...], wb_ref[...], preferred_element_type=F32)
    acc = acc + jax.nn.sigmoid(gc_ref[...]) * jnp.dot(oc_ref[...], wc_ref[...], preferred_element_type=F32)
    o_ref[...] = acc.astype(BF16)


def gated_merge(o_a, o_b, o_c, w_a, w_b, w_c, proj, tm, tn):
    r, d = o_a.shape
    n = w_a.shape[1]
    gblk = C_GM // tn
    per = D_MODEL // tn
    lhs = pl.BlockSpec((tm, d), lambda i, j: (i, 0))
    wsp = pl.BlockSpec((d, tn), lambda i, j: (0, j))

    def gate(b):
        return pl.BlockSpec((tm, tn), lambda i, j: (i, gblk + b * per + j))

    return pl.pallas_call(
        _merge_kernel,
        grid=(r // tm, n // tn),
        in_specs=[lhs, lhs, lhs, wsp, wsp, wsp, gate(0), gate(1), gate(2)],
        out_specs=pl.BlockSpec((tm, tn), lambda i, j: (i, j)),
        out_shape=jax.ShapeDtypeStruct((r, n), BF16),
        compiler_params=_cp("arbitrary", "arbitrary"),
        name="gated_merge",
    )(o_a, o_b, o_c, w_a, w_b, w_c, proj, proj, proj)


def _mmres_kernel(a_ref, w_ref, r_ref, o_ref):
    o_ref[...] = r_ref[...] + jnp.dot(a_ref[...], w_ref[...], preferred_element_type=F32)


def matmul_residual(a, w, res, tm, tn):
    r, k = a.shape
    n = w.shape[1]
    return pl.pallas_call(
        _mmres_kernel,
        grid=(r // tm, n // tn),
        in_specs=[pl.BlockSpec((tm, k), lambda i, j: (i, 0)),
                  pl.BlockSpec((k, tn), lambda i, j: (0, j)),
                  pl.BlockSpec((tm, tn), lambda i, j: (i, j))],
        out_specs=pl.BlockSpec((tm, tn), lambda i, j: (i, j)),
        out_shape=jax.ShapeDtypeStruct((r, n), F32),
        compiler_params=_cp("arbitrary", "arbitrary"),
        name="matmul_residual",
    )(a, w, res)


def _conv_ffn_kernel(tiles_per_seq, tail_start, x_ref, g_ref, wa0_ref, wb0_ref, wa_ref, wb_ref, prev_ref, cw_ref,
                     cb_ref, wo_ref, o_ref, tail_ref, xn_ref, ab_ref, nxt_ref, h_ref, ext_ref, carry_ref):
    i = pl.program_id(0)
    k = pl.program_id(1)
    tm = x_ref.shape[0]

    @pl.when(k == 0)
    def _():
        x = x_ref[...]
        y = x * lax.rsqrt(jnp.mean(x * x, axis=-1, keepdims=True) + NORM_EPS)
        xn0 = (y * g_ref[...]).astype(BF16)
        xn_ref[...] = xn0
        o_ref[...] = x
        ab_ref[0] = jnp.dot(xn0, wa0_ref[...], preferred_element_type=F32)
        ab_ref[1] = jnp.dot(xn0, wb0_ref[...], preferred_element_type=F32)

    @pl.when(k != 0)
    def _():
        ab_ref[...] = nxt_ref[...]

    @pl.when(i % tiles_per_seq == 0)
    def _():
        ext_ref[0:8, :] = prev_ref[0]

    @pl.when(i % tiles_per_seq != 0)
    def _():
        ext_ref[0:8, :] = carry_ref[k]

    a = ab_ref[0]
    ext_ref[8:8 + tm, :] = a
    carry_ref[k] = ext_ref[tm:tm + 8, :]
    tail_ref[0] = ext_ref[8 + tail_start:16 + tail_start, :]
    cw = cw_ref[...]
    y = cb_ref[...] + ext_ref[6:6 + tm, :] * cw[0:1]
    y = y + ext_ref[7:7 + tm, :] * cw[1:2]
    y = y + a * cw[2:3]
    h_ref[...] = (_gelu(y) * ab_ref[1]).astype(BF16)
    xn = xn_ref[...]
    nxt_ref[0] = jnp.dot(xn, wa_ref[...], preferred_element_type=F32)
    nxt_ref[1] = jnp.dot(xn, wb_ref[...], preferred_element_type=F32)
    o_ref[...] += jnp.dot(h_ref[...], wo_ref[...], preferred_element_type=F32)


def conv_ffn(x, g, w_in, prev, conv_w, conv_b, w_out, seq_len, tm, tk, tail_start):
    r, d = x.shape
    nk = D_FF // tk
    nseq = r // seq_len
    tiles_per_seq = seq_len // tm
    kernel = functools.partial(_conv_ffn_kernel, tiles_per_seq, tail_start)
    out, tails = pl.pallas_call(
        kernel,
        grid=(r // tm, nk),
        in_specs=[pl.BlockSpec((tm, d), lambda i, k: (i, 0)),
                  pl.BlockSpec((1, d), lambda i, k: (0, 0)),
                  pl.BlockSpec((d, tk), lambda i, k: (0, 0)),
                  pl.BlockSpec((d, tk), lambda i, k: (0, nk)),
                  pl.BlockSpec((d, tk), lambda i, k: (0, jnp.minimum(k + 1, nk - 1))),
                  pl.BlockSpec((d, tk), lambda i, k: (0, nk + jnp.minimum(k + 1, nk - 1))),
                  pl.BlockSpec((1, 8, tk), lambda i, k: (i * tm // seq_len, 0, k)),
                  pl.BlockSpec((8, tk), lambda i, k: (0, k)),
                  pl.BlockSpec((1, tk), lambda i, k: (0, k)),
                  pl.BlockSpec((tk, d), lambda i, k: (k, 0))],
        out_specs=[pl.BlockSpec((tm, d), lambda i, k: (i, 0)),
                   pl.BlockSpec((1, 8, tk), lambda i, k: (i, 0, k))],
        out_shape=[jax.ShapeDtypeStruct((r, d), F32), jax.ShapeDtypeStruct((r // tm, 8, D_FF), F32)],
        scratch_shapes=[pltpu.VMEM((tm, d), BF16), pltpu.VMEM((2, tm, tk), F32), pltpu.VMEM((2, tm, tk), F32),
                        pltpu.VMEM((tm, tk), BF16),
                        pltpu.VMEM((tm + 8, tk), F32),
                        pltpu.VMEM((nk, 8, tk), F32)],
        compiler_params=_cp("arbitrary", "arbitrary"),
        name="conv_ffn",
    )(x, g.reshape(1, d), w_in, w_in, w_in, w_in, prev, conv_w, conv_b.reshape(1, D_FF), w_out)
    return out, tails.reshape(nseq, tiles_per_seq, 8, D_FF)[:, tiles_per_seq - 1]


def _rms_kernel(x_ref, g_ref, o_ref):
    x = x_ref[...]
    o_ref[...] = x * lax.rsqrt(jnp.mean(x * x, axis=-1, keepdims=True) + NORM_EPS) * g_ref[...]


def rms_norm_rows(x, g, tm):
    r, d = x.shape
    return pl.pallas_call(
        _rms_kernel,
        grid=(r // tm,),
        in_specs=[pl.BlockSpec((tm, d), lambda i: (i, 0)), pl.BlockSpec((1, d), lambda i: (0, 0))],
        out_specs=pl.BlockSpec((tm, d), lambda i: (i, 0)),
        out_shape=jax.ShapeDtypeStruct((r, d), F32),
        compiler_params=_cp("arbitrary"),
        name="final_rms_norm",
    )(x, g.reshape(1, d))


def _ssd_kernel(rows_in, valid_len, x_ref, b_ref, c_ref, z_ref, dt_ref, prev_ref, cw_ref, cb_ref, dtb_ref, alog_ref,
                dskip_ref, norm_ref, h0_ref, tril_ref, o_ref, ht_ref, ext_ref, st_ref, tt_ref, y_ref, xw_ref):
    L = SSD_CHUNK
    c = pl.program_id(1)

    @pl.when(c == 0)
    def _():
        ext_ref[0:8, :] = prev_ref[0]
        ht_ref[...] = h0_ref[...]

    @pl.when(c != 0)
    def _():
        ext_ref[0:8, :] = ext_ref[L:L + 8, :]

    if rows_in < L:
        ext_ref[8 + rows_in:8 + L, :] = jnp.zeros((L - rows_in, XBC_DIM), F32)
    ext_ref[8:8 + rows_in, 0:D_INNER_B] = x_ref[...]
    ext_ref[8:8 + rows_in, D_INNER_B:D_INNER_B + G_B * N_B] = b_ref[...]
    ext_ref[8:8 + rows_in, D_INNER_B + G_B * N_B:XBC_DIM] = c_ref[...]

    cw = cw_ref[...]
    y = cb_ref[...] + ext_ref[5:5 + L, :] * cw[0:1]
    y = y + ext_ref[6:6 + L, :] * cw[1:2]
    y = y + ext_ref[7:7 + L, :] * cw[2:3]
    y = y + ext_ref[8:8 + L, :] * cw[3:4]
    xbc = y * jax.nn.sigmoid(y)

    if rows_in < L:
        dt_raw = jnp.concatenate([dt_ref[...], jnp.zeros((L - rows_in, 128), F32)], axis=0)
    else:
        dt_raw = dt_ref[...]
    d_all = jax.nn.softplus(dt_raw + dtb_ref[...])
    if valid_len < L:
        row = lax.broadcasted_iota(jnp.int32, (L, 128), 0)
        d_all = jnp.where(row < valid_len, d_all, 0.0)
    a_row = -jnp.exp(alog_ref[...])
    tril = tril_ref[...]
    cs = jnp.dot(tril, d_all * a_row, preferred_element_type=F32, precision=lax.Precision.HIGHEST)
    st_ref[0] = cs
    st_ref[1] = jnp.exp(cs)
    st_ref[2] = jnp.exp(cs[L - 1:L, :] - cs) * d_all
    tt_ref[0] = cs.T
    tt_ref[1] = d_all.T
    lower = tril > 0.5

    for g in range(G_B):
        b_g = xbc[:, D_INNER_B + g * N_B:D_INNER_B + (g + 1) * N_B]
        c_g = xbc[:, D_INNER_B + G_B * N_B + g * N_B:D_INNER_B + G_B * N_B + (g + 1) * N_B]
        cgb = c_g.astype(BF16)
        cb = lax.dot_general(cgb, b_g.astype(BF16), (((1,), (1,)), ((), ())), preferred_element_type=F32)
        bt = b_g.T.astype(BF16)
        gs = slice(g * 512, (g + 1) * 512)
        hg = ht_ref[0, :, gs]
        yoff = jnp.dot(cgb, hg.astype(BF16), preferred_element_type=F32)
        for hh in range(8):
            h = g * 8 + hh
            hs = slice(h * HD_B, (h + 1) * HD_B)
            x_h = xbc[:, hs]
            seg = st_ref[0, :, h:h + 1] - tt_ref[0, h:h + 1, :]
            decay = jnp.exp(jnp.where(lower, seg, NEG_INF))
            m = (cb * decay * tt_ref[1, h:h + 1, :]).astype(BF16)
            y_h = jnp.dot(m, x_h.astype(BF16), preferred_element_type=F32)
            y_h = y_h + yoff[:, hh * HD_B:(hh + 1) * HD_B] * st_ref[1, :, h:h + 1]
            y_ref[:, hs] = y_h + dskip_ref[:, hs] * x_h
            xw_ref[:, hh * HD_B:(hh + 1) * HD_B] = (x_h * st_ref[2, :, h:h + 1]).astype(BF16)
        upd = jnp.dot(bt, xw_ref[...], preferred_element_type=F32)
        for hh in range(8):
            h = g * 8 + hh
            hs = slice(h * HD_B, (h + 1) * HD_B)
            ht_ref[0, :, hs] = ht_ref[0, :, hs] * st_ref[1, L - 1:L, h:h + 1] + upd[:, hh * HD_B:(hh + 1) * HD_B]

    if rows_in < L:
        z = jnp.concatenate([z_ref[...], jnp.zeros((L - rows_in, D_INNER_B), F32)], axis=0)
    else:
        z = z_ref[...]
    yz = y_ref[...] * (z * jax.nn.sigmoid(z))
    for g in range(G_B):
        gs = slice(g * 512, (g + 1) * 512)
        v = yz[:, gs]
        out = v * lax.rsqrt(jnp.mean(v * v, axis=-1, keepdims=True) + NORM_EPS) * norm_ref[:, gs]
        o_ref[:, gs] = out[0:rows_in].astype(BF16)


def ssd_mixer(proj, prev, conv_w, conv_b, dt_bias, a_log, d_skip, ssm_norm, h0t, nseq, seq_rows, valid_len):
    rows_in = min(seq_rows, SSD_CHUNK)
    nchunk = seq_rows // rows_in
    r = nseq * seq_rows
    tril = jnp.asarray(np.tril(np.ones((SSD_CHUNK, SSD_CHUNK), np.float32)))
    pad128 = lambda v: jnp.pad(v.astype(F32), (0, 128 - v.shape[0])).reshape(1, 128)
    kernel = functools.partial(_ssd_kernel, rows_in, min(valid_len, SSD_CHUNK))
    rowblk = lambda n, c: n * nchunk + c
    const = lambda n, c: (0, 0)
    return pl.pallas_call(
        kernel,
        grid=(nseq, nchunk),
        in_specs=[pl.BlockSpec((rows_in, D_INNER_B), lambda n, c: (rowblk(n, c), C_XBC // 2048)),
                  pl.BlockSpec((rows_in, 512), lambda n, c: (rowblk(n, c), (C_XBC + 2048) // 512)),
                  pl.BlockSpec((rows_in, 512), lambda n, c: (rowblk(n, c), (C_XBC + 2560) // 512)),
                  pl.BlockSpec((rows_in, D_INNER_B), lambda n, c: (rowblk(n, c), C_Z // 2048)),
                  pl.BlockSpec((rows_in, 128), lambda n, c: (rowblk(n, c), C_DT // 128)),
                  pl.BlockSpec((1, 8, XBC_DIM), lambda n, c: (n, 0, 0)),
                  pl.BlockSpec((8, XBC_DIM), const),
                  pl.BlockSpec((1, XBC_DIM), const),
                  pl.BlockSpec((1, 128), const),
                  pl.BlockSpec((1, 128), const),
                  pl.BlockSpec((1, D_INNER_B), const),
                  pl.BlockSpec((1, D_INNER_B), const),
                  pl.BlockSpec((1, N_B, D_INNER_B), lambda n, c: (n, 0, 0)),
                  pl.BlockSpec((SSD_CHUNK, SSD_CHUNK), const)],
        out_specs=[pl.BlockSpec((rows_in, D_INNER_B), lambda n, c: (rowblk(n, c), 0)),
                   pl.BlockSpec((1, N_B, D_INNER_B), lambda n, c: (n, 0, 0))],
        out_shape=[jax.ShapeDtypeStruct((r, D_INNER_B), BF16),
                   jax.ShapeDtypeStruct((nseq, N_B, D_INNER_B), F32)],
        scratch_shapes=[pltpu.VMEM((SSD_CHUNK + 8, XBC_DIM), F32),
                        pltpu.VMEM((3, SSD_CHUNK, 128), F32),
                        pltpu.VMEM((2, 128, SSD_CHUNK), F32),
                        pltpu.VMEM((SSD_CHUNK, D_INNER_B), F32),
                        pltpu.VMEM((SSD_CHUNK, 512), BF16)],
        compiler_params=_cp("arbitrary", "arbitrary"),
        name="ssd_mixer",
    )(proj, proj, proj, proj, proj, prev, conv_w, conv_b.reshape(1, XBC_DIM), pad128(dt_bias), pad128(a_log),
      jnp.repeat(d_skip.astype(F32), HD_B).reshape(1, D_INNER_B), ssm_norm.reshape(1, D_INNER_B), h0t, tril)


def _cmlp_kernel(rows_in, u_ref, v_ref, lg_ref, lb_ref, ws_ref, bst_ref, o_ref, vn_ref):
    L = CHUNK_C
    v = _gelu(v_ref[...])
    mu = jnp.mean(v, axis=-1, keepdims=True)
    vc = v - mu
    var = jnp.mean(vc * vc, axis=-1, keepdims=True)
    vn = vc * lax.rsqrt(var + NORM_EPS) * lg_ref[...] + lb_ref[...]
    vn_ref[...] = vn
    u = _gelu(u_ref[...])
    if rows_in < L:
        vn = jnp.concatenate([vn, jnp.zeros((L - rows_in, D_C), F32)], axis=0)
    vnb = vn.astype(BF16)
    row = lax.broadcasted_iota(jnp.int32, (L, L), 0)
    col = lax.broadcasted_iota(jnp.int32, (L, L), 1)
    gw = D_C // G_C
    for g in range(G_C):
        w = jnp.where(col <= row, ws_ref[g], 0.0).astype(BF16)
        s = jnp.dot(w, vnb[:, g * gw:(g + 1) * gw], preferred_element_type=F32) + bst_ref[:, g:g + 1]
        o_ref[:, g * gw:(g + 1) * gw] = (u[:, g * gw:(g + 1) * gw] * s[0:rows_in]).astype(BF16)


def chunk_mlp_mixer(proj, ln_g, ln_b, w_s, b_s, seq_rows):
    r = proj.shape[0]
    rows_in = min(seq_rows, CHUNK_C)
    kernel = functools.partial(_cmlp_kernel, rows_in)
    bst = jnp.pad(b_s.T.astype(F32), ((0, 0), (0, 128 - G_C)))
    return pl.pallas_call(
        kernel,
        grid=(r // rows_in,),
        in_specs=[pl.BlockSpec((rows_in, D_C), lambda i: (i, C_U // 2048)),
                  pl.BlockSpec((rows_in, D_C), lambda i: (i, C_V // 2048)),
                  pl.BlockSpec((1, D_C), lambda i: (0, 0)),
                  pl.BlockSpec((1, D_C), lambda i: (0, 0)),
                  pl.BlockSpec((G_C, CHUNK_C, CHUNK_C), lambda i: (0, 0, 0)),
                  pl.BlockSpec((CHUNK_C, 128), lambda i: (0, 0))],
        out_specs=[pl.BlockSpec((rows_in, D_C), lambda i: (i, 0)),
                   pl.BlockSpec((rows_in, D_C), lambda i: (i, 0))],
        out_shape=[jax.ShapeDtypeStruct((r, D_C), BF16), jax.ShapeDtypeStruct((r, D_C), F32)],
        compiler_params=_cp("arbitrary"),
        name="chunk_mlp_mixer",
    )(proj, proj, ln_g.reshape(1, D_C), ln_b.reshape(1, D_C), w_s, bst)


ROW_W = 2 * KV_A * HD_A
SCALE = HD_A ** -0.5
MASK_BIG = 1.0e30
M_INIT = -2.0e30
EXP2_SCALE = SCALE * math.log2(math.e)


CMP_ROWS = 4096
CHUNK_PAGES = CMP_ROWS // PAGE_SIZE
BLK_PER_PAGE = PAGE_SIZE // CMP_BLK


def _paged_chunk_pipeline(pt_ref, copies_for, n_chunks_per_seq):
    n = pl.program_id(0)
    k = pl.program_id(1)
    step = n * n_chunks_per_seq + k
    total = pl.num_programs(0) * n_chunks_per_seq
    slot = step % 2

    @pl.when(step == 0)
    def _():
        for cp in copies_for(n, k, slot):
            cp.start()

    @pl.when(step + 1 < total)
    def _():
        wrap = k + 1 == n_chunks_per_seq
        for cp in copies_for(jnp.where(wrap, n + 1, n), jnp.where(wrap, 0, k + 1), 1 - slot):
            cp.start()

    for cp in copies_for(n, k, slot):
        cp.wait()
    return slot


def _cmpmlp_paged_kernel(layer, pt_ref, cache_ref, pe_ref, w1_ref, w2_ref, kc_ref, vc_ref, buf_ref, sem):
    nkc = pl.num_programs(1)

    def copies_for(n, k, slot):
        return [pltpu.make_async_copy(cache_ref.at[layer, pt_ref[n, k * CHUNK_PAGES + p]],
                                      buf_ref.at[slot, :, :, pl.ds(p * 8, 8), :], sem.at[slot])
                for p in range(CHUNK_PAGES)]

    slot = _paged_chunk_pipeline(pt_ref, copies_for, nkc)
    for c in range(2):
        xs = [(buf_ref[slot, c, s] + pe_ref[s * 2 + c]).astype(BF16) for s in range(CMP_BLK)]
        acc = jnp.dot(jnp.concatenate(xs, axis=1), w1_ref[c], preferred_element_type=F32)
        out = jnp.dot(_gelu(acc).astype(BF16), w2_ref[c], preferred_element_type=F32)
        dst = kc_ref if c == 0 else vc_ref
        dst[0] = out.reshape(CHUNK_PAGES, KV_A * BLK_PER_PAGE, HD_A)


def compress_kv_paged(page_table, cache_t, layer, pe, w1, w2):
    nseq, npages = page_table.shape
    nkc = npages // CHUNK_PAGES
    rows8 = KV_A * BLK_PER_PAGE
    shape = jax.ShapeDtypeStruct((nseq, npages, rows8, HD_A), F32)
    ospec = pl.BlockSpec((1, CHUNK_PAGES, rows8, HD_A), lambda n, k, pt: (n, k, 0, 0))
    kernel = functools.partial(_cmpmlp_paged_kernel, layer)
    kc, vc = pl.pallas_call(
        kernel,
        grid_spec=pltpu.PrefetchScalarGridSpec(
            num_scalar_prefetch=1, grid=(nseq, nkc),
            in_specs=[pl.BlockSpec(memory_space=pl.ANY),
                      pl.BlockSpec((CMP_BLK * 2, 1, HD_A), lambda n, k, pt: (0, 0, 0)),
                      pl.BlockSpec((2, CMP_BLK * HD_A, HD_A), lambda n, k, pt: (0, 0, 0)),
                      pl.BlockSpec((2, HD_A, HD_A), lambda n, k, pt: (0, 0, 0))],
            out_specs=[ospec, ospec],
            scratch_shapes=[pltpu.VMEM((2, 2, CMP_BLK, CHUNK_PAGES * rows8, HD_A), F32),
                            pltpu.SemaphoreType.DMA((2,))]),
        out_shape=[shape, shape],
        compiler_params=_cp("arbitrary", "arbitrary"),
        name="compress_kv_paged",
    )(page_table, cache_t, pe.reshape(CMP_BLK * 2, 1, HD_A), w1.reshape(2, CMP_BLK * HD_A, HD_A), w2)

    def regroup(x):
        x = x.reshape(nseq, npages, KV_A, BLK_PER_PAGE, HD_A).transpose(0, 2, 1, 3, 4)
        return x.reshape(nseq, KV_A, npages * BLK_PER_PAGE, HD_A).astype(BF16)

    return regroup(kc), regroup(vc)


def _cmpmlp_kernel(x00_ref, x01_ref, x10_ref, x11_ref, pe_ref, w1_ref, w2_ref, kc_ref, vc_ref):
    x_refs = ((x00_ref, x01_ref), (x10_ref, x11_ref))
    nblk = x00_ref.shape[1] // CMP_BLK
    for c in range(2):
        def body(s, acc):
            xs = [x_refs[c][g][0, pl.ds(s, nblk, stride=CMP_BLK), :] for g in range(KV_A)]
            x = jnp.concatenate(xs, axis=0) + pe_ref[s * 2 + c]
            return acc + jnp.dot(x.astype(BF16), w1_ref[c, s], preferred_element_type=F32)

        acc = lax.fori_loop(0, CMP_BLK, body, jnp.zeros((KV_A * nblk, HD_A), F32))
        out = jnp.dot(_gelu(acc).astype(BF16), w2_ref[c], preferred_element_type=F32)
        dst = kc_ref if c == 0 else vc_ref
        for g in range(KV_A):
            dst[0, g] = out[g * nblk:(g + 1) * nblk].astype(BF16)


def compress_kv(src, lead, col_blk, nseq, seq_rows, pe, w1, w2):
    rows = min(CMP_ROWS, seq_rows)
    nchunk = seq_rows // rows
    nb = seq_rows // CMP_BLK
    shape = jax.ShapeDtypeStruct((nseq, KV_A, nb, HD_A), BF16)
    ospec = pl.BlockSpec((1, KV_A, rows // CMP_BLK, HD_A), lambda n, c: (n, 0, c, 0))

    def xspec(part):
        return pl.BlockSpec((1, rows, HD_A), lambda n, c: (lead, n * nchunk + c, col_blk * (ROW_W // HD_A) + part))

    return pl.pallas_call(
        _cmpmlp_kernel,
        grid=(nseq, nchunk),
        in_specs=[xspec(0), xspec(1), xspec(2), xspec(3),
                  pl.BlockSpec((CMP_BLK * 2, 1, HD_A), lambda n, c: (0, 0, 0)),
                  pl.BlockSpec((2, CMP_BLK, HD_A, HD_A), lambda n, c: (0, 0, 0, 0)),
                  pl.BlockSpec((2, HD_A, HD_A), lambda n, c: (0, 0, 0))],
        out_specs=[ospec, ospec],
        out_shape=[shape, shape],
        compiler_params=_cp("arbitrary", "arbitrary"),
        name="compress_kv",
    )(src, src, src, src, pe.reshape(CMP_BLK * 2, 1, HD_A), w1, w2)


def _select_blocks(imp, q0, ns):
    nsp, tqp = imp.shape
    j = lax.broadcasted_iota(jnp.int32, (nsp, tqp), 0)
    cur = lax.shift_right_logical(q0 + lax.broadcasted_iota(jnp.int32, (nsp, tqp), 1), 6)
    forced = (j == 0) | (j == cur) | (j == cur - 1)
    score = jnp.where(j <= cur, imp + jnp.where(forced, FORCE_BONUS, 0.0), -1.0)
    score = jnp.where(j < ns, score, -2.0)
    sel = jnp.zeros((nsp, tqp), F32)
    for _ in range(N_SEL):
        top = jnp.max(score, axis=0, keepdims=True)
        idx = jnp.min(jnp.where(score == top, j, 1 << 30), axis=0, keepdims=True)
        hit = j == idx
        sel = jnp.where(hit, 1.0, sel)
        score = jnp.where(hit, -3.0, score)
    return sel


def _cmp_bias_table(tb_ref, rb_ref, g, tq, nb):
    near = (MAX_DIST + CMP_BLK - 1) // CMP_BLK
    qi = lax.broadcasted_iota(jnp.int32, (tq, nb), 0)
    c = lax.broadcasted_iota(jnp.int32, (tq, nb), 1)
    d = jnp.where(c < near + tq // CMP_BLK, qi + near * CMP_BLK - c * CMP_BLK - (CMP_BLK - 1), 4 * MAX_DIST)
    for h in range(HPG_A):
        tb_ref[h] = _t5_bias(d, rb_ref, g * HPG_A + h)


def _cmp_bias_shift(q0, nb):
    near = (MAX_DIST + CMP_BLK - 1) // CMP_BLK
    return lax.rem(q0 // CMP_BLK - near + nb, nb)


def _cmpattn_kernel(q_pos0, ns, rb_ref, q_ref, kc_ref, vc_ref, pool_ref, o_ref, sel_ref):
    g = pl.program_id(1)
    qt = pl.program_id(2)
    tq = q_ref.shape[0]
    nb = kc_ref.shape[2]
    q0 = q_pos0 + qt * tq
    qpos = q0 + lax.broadcasted_iota(jnp.int32, (tq, nb), 0)
    dist = qpos - (lax.broadcasted_iota(jnp.int32, (tq, nb), 1) * CMP_BLK + (CMP_BLK - 1))
    vis = dist >= 0
    anyvis = ((q0 + lax.broadcasted_iota(jnp.int32, (tq, 1), 0)) >= CMP_BLK - 1).astype(F32)
    kc = kc_ref[0, 0]
    vc = vc_ref[0, 0]
    psum = jnp.zeros((tq, nb), F32)
    for h in range(HPG_A):
        hs = slice(h * HD_A, (h + 1) * HD_A)
        lc = lax.dot_general(q_ref[:, hs].astype(BF16), kc, (((1,), (1,)), ((), ())), preferred_element_type=F32)
        lc = lc * SCALE + _t5_bias(dist, rb_ref, g * HPG_A + h)
        lc = jnp.where(vis, lc, NEG_INF)
        e = jnp.exp(lc - jnp.max(lc, axis=-1, keepdims=True))
        p = e / jnp.sum(e, axis=-1, keepdims=True) * anyvis
        o_ref[:, hs] = jnp.dot(p.astype(BF16), vc, preferred_element_type=F32)
        psum = psum + p

    tqp = max(tq, 128)
    if tq < tqp:
        psum = jnp.concatenate([psum, jnp.zeros((tqp - tq, nb), F32)], axis=0)
    imp = lax.dot_general(pool_ref[...], psum, (((1,), (1,)), ((), ())), preferred_element_type=F32,
                          precision=lax.Precision.HIGHEST)
    sel = _select_blocks(imp, q0, ns)
    sel_ref[0] = sel.T[0:tq].astype(BF16)


def compressed_attention(proj, kc, vc, rel_bias, nseq, seq_rows, tq, q_pos0, ns):
    r = nseq * seq_rows
    nb = kc.shape[2]
    nsp = -(-ns // 128) * 128
    nqt = seq_rows // tq
    pool = np.zeros((nsp, nb), np.float32)
    pool[np.arange(nb) // (SLC_BLK // CMP_BLK), np.arange(nb)] = 1.0
    kernel = functools.partial(_cmpattn_kernel, q_pos0, ns)
    kvspec = pl.BlockSpec((1, 1, nb, HD_A), lambda n, g, t: (n, g, 0, 0))
    sel_spec = pl.BlockSpec((1, tq, nsp), lambda n, g, t: (g, n * nqt + t, 0))
    sel_shape = jax.ShapeDtypeStruct((KV_A, r, nsp), BF16)
    return pl.pallas_call(
        kernel,
        grid=(nseq, KV_A, nqt),
        in_specs=[pl.BlockSpec(memory_space=pltpu.SMEM),
                  pl.BlockSpec((tq, HPG_A * HD_A), lambda n, g, t: (n * nqt + t, g)),
                  kvspec, kvspec,
                  pl.BlockSpec((nsp, nb), lambda n, g, t: (0, 0))],
        out_specs=[pl.BlockSpec((tq, HPG_A * HD_A), lambda n, g, t: (n * nqt + t, g)), sel_spec],
        out_shape=[jax.ShapeDtypeStruct((r, H_A * HD_A), F32), sel_shape],
        compiler_params=_cp("arbitrary", "arbitrary", "arbitrary"),
        name="compressed_attention",
    )(rel_bias, proj, kc, vc, jnp.asarray(pool))


def _softmax_step(s, v, m_ref, l_ref, acc_ref):
    m_prev = m_ref[...]
    m_new = jnp.maximum(m_prev, jnp.max(s, axis=-1, keepdims=True))
    alpha = jnp.exp(m_prev - m_new)
    p = jnp.exp(s - m_new)
    l_ref[...] = alpha * l_ref[...] + jnp.sum(p, axis=-1, keepdims=True)
    acc_ref[...] = alpha * acc_ref[...] + jnp.dot(p.astype(BF16), v, preferred_element_type=F32)
    m_ref[...] = m_new


def _softmax_init(m_ref, l_ref, acc_ref):
    m_ref[...] = jnp.full(m_ref.shape, M_INIT, F32)
    l_ref[...] = jnp.zeros(l_ref.shape, F32)
    acc_ref[...] = jnp.zeros(acc_ref.shape, F32)


def _qk(q, k):
    return lax.dot_general(q, k, (((1,), (1,)), ((), ())), preferred_element_type=F32) * SCALE


PTQ = 128


def _attn_sweep_t(k_ref, q_t, vt_ref, bias_ref, tab_of, first, last, s_ref, p_ref, m_ref, l_ref, acc_ref):
    tq = PTQ

    def scores(kc):
        k0 = pl.multiple_of(kc * tq, tq)
        return jnp.dot(k_ref[pl.ds(k0, tq), :], q_t, preferred_element_type=F32)

    _softmax_init(m_ref, l_ref, acc_ref)
    s_ref[0] = scores(first)
    p_ref[...] = jnp.zeros(p_ref.shape, BF16)

    def body(i, carry):
        kc = first + i
        slot = i % 2
        s = s_ref[slot] + bias_ref[tab_of(kc)]
        s_ref[1 - slot] = scores(jnp.minimum(kc + 1, last))
        pv = jnp.dot(vt_ref[jnp.maximum(kc - 1, first)], p_ref[...], preferred_element_type=F32)
        m_prev = m_ref[...]
        m_new = jnp.maximum(m_prev, jnp.max(s, axis=0, keepdims=True))
        alpha = jnp.exp2((m_prev - m_new) * EXP2_SCALE)
        p = jnp.exp2((s - m_new) * EXP2_SCALE)
        l_ref[...] = alpha * l_ref[...] + jnp.sum(p, axis=0, keepdims=True)
        acc_ref[...] = alpha * (acc_ref[...] + pv)
        p_ref[...] = p.astype(BF16)
        m_ref[...] = m_new
        return carry

    lax.fori_loop(0, last - first + 1, body, 0)
    acc_ref[...] += jnp.dot(vt_ref[last], p_ref[...], preferred_element_type=F32)


def _prompt_attn_kernel(seq, ns, rb_ref, q_ref, kc_ref, vc_ref, pool_ref, ks_ref, vs_ref, kw_ref, vw_ref, gn_ref,
                        o_ref, kaug_ref, vst_ref, kwb_ref, vwt_ref, btab_ref, tb_ref, s_ref, p_ref, m_ref, l_ref,
                        acc_ref, os_ref, oc_ref):
    g = pl.program_id(1)
    qt = pl.program_id(2)
    tq = PTQ
    nb = kc_ref.shape[2]

    @pl.when(qt == 0)
    def _():
        _cmp_bias_table(tb_ref, rb_ref, g, tq, nb)
        kaug_ref[:, 0:HD_A] = ks_ref[...].astype(BF16)
        row = lax.broadcasted_iota(jnp.int32, (seq, 128), 0)
        lane = lax.broadcasted_iota(jnp.int32, (seq, 128), 1)
        kaug_ref[:, HD_A:HD_A + 128] = jnp.where(lax.shift_right_logical(row, 6) == lane, 1.0, 0.0).astype(BF16)
        kwb_ref[...] = kw_ref[...].astype(BF16)

        def transpose_chunk(c, carry):
            r0 = pl.multiple_of(c * tq, tq)
            vst_ref[c] = vs_ref[pl.ds(r0, tq), :].T.astype(BF16)
            vwt_ref[c] = vw_ref[pl.ds(r0, tq), :].T.astype(BF16)
            return carry

        lax.fori_loop(0, seq // tq, transpose_chunk, 0)
        kj = lax.broadcasted_iota(jnp.int32, (tq, tq), 0)
        qi = lax.broadcasted_iota(jnp.int32, (tq, tq), 1)
        for h in range(HPG_A):
            head = g * HPG_A + h
            cols = slice(h * tq, (h + 1) * tq)
            inv = 1.0 / SCALE
            far = jnp.full((tq, tq), rb_ref[N_BUCKETS - 1, head], F32) * inv
            btab_ref[0, :, cols] = _t5_bias(qi - kj + tq, rb_ref, head) * inv
            btab_ref[1, :, cols] = jnp.where(kj <= qi, _t5_bias(qi - kj, rb_ref, head) * inv, NEG_INF)
            btab_ref[2, :, cols] = jnp.where(kj > qi, far, NEG_INF)
            btab_ref[3, :, cols] = far

    q_t = jnp.concatenate([q_ref[:, h * HD_A:(h + 1) * HD_A].T.astype(BF16) for h in range(HPG_A)], axis=1)

    q0 = qt * tq
    shift = _cmp_bias_shift(q0, nb)
    bias_c = jnp.concatenate([pltpu.roll(tb_ref[h], shift, 1).T for h in range(HPG_A)], axis=1)
    blk = lax.broadcasted_iota(jnp.int32, (nb, tq), 0)
    qpos = q0 + lax.broadcasted_iota(jnp.int32, (nb, tq), 1)
    vis = jnp.concatenate([qpos >= blk * CMP_BLK + (CMP_BLK - 1)] * HPG_A, axis=1)
    anyvis = jnp.concatenate([(qpos[0:1] >= CMP_BLK - 1).astype(F32)] * HPG_A, axis=1)
    lc = jnp.dot(kc_ref[0, 0], q_t, preferred_element_type=F32) * SCALE + bias_c
    lc = jnp.where(vis, lc, NEG_INF)
    e = jnp.exp(lc - jnp.max(lc, axis=0, keepdims=True))
    pc = e / jnp.sum(e, axis=0, keepdims=True) * anyvis
    vc_t = vc_ref[0, 0].astype(F32).T.astype(BF16)
    oc_ref[...] = jnp.dot(vc_t, pc.astype(BF16), preferred_element_type=F32)
    psum = pc[:, 0:tq]
    for h in range(1, HPG_A):
        psum = psum + pc[:, h * tq:(h + 1) * tq]
    imp = jnp.dot(pool_ref[...], psum, preferred_element_type=F32, precision=lax.Precision.HIGHEST)
    sel = _select_blocks(imp, q0, ns)
    selterm = ((sel - 1.0) * MASK_BIG).astype(BF16)
    q_aug_t = jnp.concatenate([q_t, jnp.concatenate([selterm] * HPG_A, axis=1)], axis=0)

    back = WINDOW // tq

    def slc_tab(kc):
        return jnp.where(kc == qt, 1, jnp.where(kc == qt - 1, 0, 3))

    def win_tab(kc):
        return jnp.where(kc == qt - back, 2, slc_tab(kc))

    _attn_sweep_t(kaug_ref, q_aug_t, vst_ref, btab_ref, slc_tab, 0, qt, s_ref, p_ref, m_ref, l_ref, acc_ref)
    os_ref[...] = acc_ref[...] / l_ref[...]
    _attn_sweep_t(kwb_ref, q_t, vwt_ref, btab_ref, win_tab, jnp.maximum(qt - back, 0), qt, s_ref, p_ref, m_ref,
                  l_ref, acc_ref)
    ow = acc_ref[...] / l_ref[...]

    gates = jax.nn.sigmoid(gn_ref[...])
    gates_t = gates.T
    for h in range(HPG_A):
        lo, hi = 3 * h, 3 * (HPG_A + h)
        g0, g1, g2 = (jnp.where(g == 0, gates_t[lo + b:lo + b + 1, :], gates_t[hi + b:hi + b + 1, :])
                      for b in range(3))
        cols = slice(h * tq, (h + 1) * tq)
        mix = g0 * oc_ref[:, cols] + g1 * os_ref[:, cols] + g2 * ow[:, cols]
        o_ref[:, h * HD_A:(h + 1) * HD_A] = mix.T.astype(BF16)


def prompt_attention(proj, kc, vc, rel_bias, nseq, seq):
    r = nseq * seq
    nqt = seq // PTQ
    nb = kc.shape[2]
    ns = -(-seq // SLC_BLK)
    nsp = 128
    assert ns <= nsp and nb == 128 and seq % PTQ == 0
    pool = np.zeros((nsp, nb), np.float32)
    pool[np.arange(nb) // (SLC_BLK // CMP_BLK), np.arange(nb)] = 1.0
    kernel = functools.partial(_prompt_attn_kernel, seq, ns)
    kvspec = pl.BlockSpec((1, 1, nb, HD_A), lambda n, g, t: (n, g, 0, 0))

    def kv(col):
        return pl.BlockSpec((seq, HD_A), lambda n, g, t: (n, col // HD_A + g))

    qspec = pl.BlockSpec((PTQ, HPG_A * HD_A), lambda n, g, t: (n * nqt + t, g))
    rows8 = HPG_A * PTQ
    return pl.pallas_call(
        kernel,
        grid=(nseq, KV_A, nqt),
        in_specs=[pl.BlockSpec(memory_space=pltpu.SMEM),
                  qspec,
                  kvspec, kvspec,
                  pl.BlockSpec((nsp, nb), lambda n, g, t: (0, 0)),
                  kv(C_KVS), kv(C_KVS + KV_A * HD_A), kv(C_KVW), kv(C_KVW + KV_A * HD_A),
                  pl.BlockSpec((PTQ, 128), lambda n, g, t: (n * nqt + t, C_GN // 128))],
        out_specs=qspec,
        out_shape=jax.ShapeDtypeStruct((r, H_A * HD_A), BF16),
        scratch_shapes=[pltpu.VMEM((seq, 2 * HD_A), BF16), pltpu.VMEM((nqt, HD_A, PTQ), BF16),
                        pltpu.VMEM((seq, HD_A), BF16), pltpu.VMEM((nqt, HD_A, PTQ), BF16),
                        pltpu.VMEM((4, PTQ, rows8), F32), pltpu.VMEM((HPG_A, PTQ, nb), F32),
                        pltpu.VMEM((2, PTQ, rows8), F32), pltpu.VMEM((PTQ, rows8), BF16),
                        pltpu.VMEM((1, rows8), F32), pltpu.VMEM((1, rows8), F32), pltpu.VMEM((HD_A, rows8), F32),
                        pltpu.VMEM((HD_A, rows8), F32), pltpu.VMEM((HD_A, rows8), F32)],
        compiler_params=_cp("arbitrary", "arbitrary", "arbitrary"),
        name="prompt_attention",
    )(rel_bias, proj, kc, vc, jnp.asarray(pool), proj, proj, proj, proj, proj)


SUB = 512


def _sample_attn_kernel(past, layer, pt_ref, rb_ref, q_ref, sel_ref, cache_ref, kn_ref, vn_ref, kwc_ref, vwc_ref,
                        kwn_ref, vwn_ref, oc_ref, gn_ref, o_ref, buf_ref, sem, m_ref, l_ref, acc_ref):
    kc = pl.program_id(1)
    nkc = pl.num_programs(1)
    tq = q_ref.shape[0]
    rows8 = HPG_A * tq
    nsp = sel_ref.shape[2]
    chunk = CHUNK_PAGES * PAGE_SIZE
    ns_last = past // SLC_BLK

    def copies_for(n, k, slot):
        return [pltpu.make_async_copy(cache_ref.at[layer, pt_ref[n, k * CHUNK_PAGES + p]],
                                      buf_ref.at[slot, :, :, pl.ds(p * PAGE_SIZE, PAGE_SIZE), :], sem.at[slot])
                for p in range(CHUNK_PAGES)]

    slot = _paged_chunk_pipeline(pt_ref, copies_for, nkc)
    rowh = lax.shift_right_logical(lax.broadcasted_iota(jnp.int32, (rows8, 1), 0), int(math.log2(tq)))
    qi_s = lax.broadcasted_iota(jnp.int32, (rows8, SUB), 0) & (tq - 1)
    kj_s = lax.broadcasted_iota(jnp.int32, (rows8, SUB), 1)
    gates = jax.nn.sigmoid(gn_ref[...])

    for g in range(KV_A):
        stats = (m_ref.at[g], l_ref.at[g], acc_ref.at[g])
        selterm = ((sel_ref[g].astype(F32) - 1.0) * MASK_BIG).astype(BF16)
        selrows = jnp.concatenate([selterm] * HPG_A, axis=0)
        q_all = jnp.concatenate([q_ref[:, (g * HPG_A + h) * HD_A:(g * HPG_A + h + 1) * HD_A].astype(BF16)
                                 for h in range(HPG_A)], axis=0)
        q_aug = jnp.concatenate([q_all, selrows], axis=1)
        far = jnp.zeros((rows8, 1), F32)
        for h in range(HPG_A):
            far = jnp.where(rowh == h, rb_ref[N_BUCKETS - 1, g * HPG_A + h], far)

        def head_bias(dist):
            return jnp.concatenate([_t5_bias(dist[h * tq:(h + 1) * tq], rb_ref, g * HPG_A + h)
                                    for h in range(HPG_A)], axis=0)

        @pl.when(kc == 0)
        def _():
            _softmax_init(*stats)

        for sub in range(chunk // SUB):
            k = buf_ref[slot, 0, g, sub * SUB:(sub + 1) * SUB, :]
            v = buf_ref[slot, 1, g, sub * SUB:(sub + 1) * SUB, :]
            blk = lax.shift_right_logical(kc * chunk + sub * SUB + lax.broadcasted_iota(jnp.int32, (SUB, nsp), 0), 6)
            onehot = jnp.where(blk == lax.broadcasted_iota(jnp.int32, (SUB, nsp), 1), 1.0, 0.0).astype(BF16)
            s = _qk(q_aug, jnp.concatenate([k, onehot], axis=1))
            if sub < chunk // SUB - 1:
                _softmax_step(s + far, v, *stats)
            else:
                @pl.when(kc < nkc - 1)
                def _():
                    _softmax_step(s + far, v, *stats)

                @pl.when(kc == nkc - 1)
                def _():
                    _softmax_step(s + head_bias(qi_s + SUB - kj_s), v, *stats)

        @pl.when(kc == nkc - 1)
        def _():
            m_g, l_g, acc_g = stats
            gk = slice(g * HD_A, (g + 1) * HD_A)
            qi = lax.broadcasted_iota(jnp.int32, (rows8, 128), 0) & (tq - 1)
            kj = lax.broadcasted_iota(jnp.int32, (rows8, 128), 1)
            pad = jnp.zeros((128 - tq, HD_A), BF16)
            b_new = jnp.where(kj <= qi, head_bias(qi - kj), NEG_INF)
            k_new = jnp.concatenate([kn_ref[:, gk].astype(BF16), pad], axis=0)
            v_new = jnp.concatenate([vn_ref[:, gk].astype(BF16), pad], axis=0)
            s = _qk(q_all, k_new) + b_new + selrows[:, ns_last:ns_last + 1].astype(F32) * SCALE
            _softmax_step(s, v_new, *stats)
            o_s = acc_g[...] / l_g[...]
            _softmax_init(*stats)
            wrows = kwc_ref.shape[2]
            qi_w = lax.broadcasted_iota(jnp.int32, (rows8, wrows), 0) & (tq - 1)
            kj_w = lax.broadcasted_iota(jnp.int32, (rows8, wrows), 1)
            dw = qi_w + wrows - kj_w
            s = _qk(q_all, kwc_ref[0, 0, :, gk].astype(BF16)) + jnp.where(dw < WINDOW, head_bias(dw), NEG_INF)
            _softmax_step(s, vwc_ref[0, 0, :, gk].astype(BF16), *stats)
            k_new = jnp.concatenate([kwn_ref[:, gk].astype(BF16), pad], axis=0)
            v_new = jnp.concatenate([vwn_ref[:, gk].astype(BF16), pad], axis=0)
            _softmax_step(_qk(q_all, k_new) + b_new, v_new, *stats)
            o_w = acc_g[...] / l_g[...]
            for h in range(HPG_A):
                lane = 3 * (g * HPG_A + h)
                rows = slice(h * tq, (h + 1) * tq)
                hs = slice((g * HPG_A + h) * HD_A, (g * HPG_A + h + 1) * HD_A)
                o_ref[:, hs] = (gates[:, lane:lane + 1] * oc_ref[:, hs] + gates[:, lane + 1:lane + 2] * o_s[rows, :]
                                + gates[:, lane + 2:lane + 3] * o_w[rows, :]).astype(BF16)


def sample_attention(proj, o_cmp, sel, page_table, cache_t, layer, cache_win, rel_bias, nseq, tq, past):
    r = nseq * tq
    nsp = sel.shape[2]
    chunk = CHUNK_PAGES * PAGE_SIZE
    nkc = past // chunk
    kernel = functools.partial(_sample_attn_kernel, past, layer)
    wrows = cache_win.shape[2]
    assert wrows == WINDOW
    gw = KV_A * HD_A

    def new_rows(col):
        return pl.BlockSpec((tq, gw), lambda n, k, pt: (n, col // gw))

    def win_rows(c):
        return pl.BlockSpec((1, 1, wrows, gw), lambda n, k, pt: (layer, n, 0, c))

    qspec = pl.BlockSpec((tq, H_A * HD_A), lambda n, k, pt: (n, 0))
    rows8 = HPG_A * tq
    return pl.pallas_call(
        kernel,
        grid_spec=pltpu.PrefetchScalarGridSpec(
            num_scalar_prefetch=1, grid=(nseq, nkc),
            in_specs=[pl.BlockSpec(memory_space=pltpu.SMEM),
                      qspec,
                      pl.BlockSpec((KV_A, tq, nsp), lambda n, k, pt: (0, n, 0)),
                      pl.BlockSpec(memory_space=pl.ANY),
                      new_rows(C_KVS), new_rows(C_KVS + gw),
                      win_rows(0), win_rows(1),
                      new_rows(C_KVW), new_rows(C_KVW + gw),
                      qspec,
                      pl.BlockSpec((tq, 128), lambda n, k, pt: (n, C_GN // 128))],
            out_specs=qspec,
            scratch_shapes=[pltpu.VMEM((2, 2, KV_A, chunk, HD_A), BF16), pltpu.SemaphoreType.DMA((2,)),
                            pltpu.VMEM((KV_A, rows8, 1), F32), pltpu.VMEM((KV_A, rows8, 1), F32),
                            pltpu.VMEM((KV_A, rows8, HD_A), F32)]),
        out_shape=jax.ShapeDtypeStruct((r, H_A * HD_A), BF16),
        compiler_params=_cp("arbitrary", "arbitrary"),
        name="sample_attention",
    )(page_table, rel_bias, proj, sel, cache_t, proj, proj, cache_win, cache_win, proj, proj, o_cmp, proj)


def _prep_w_in(w):
    src = 0
    pieces = []
    for width, dst in _SEGS:
        pieces.append((dst, w[:, src:src + width]))
        src += width
    pieces.sort(key=lambda p: p[0])
    cols, pos = [], 0
    for dst, piece in pieces:
        assert dst >= pos
        if dst > pos:
            cols.append(jnp.zeros((w.shape[0], dst - pos), w.dtype))
        cols.append(piece)
        pos = dst + piece.shape[1]
    if pos < IN_PAD:
        cols.append(jnp.zeros((w.shape[0], IN_PAD - pos), w.dtype))
    return jnp.concatenate(cols, axis=1).astype(BF16)


def _state_rows(state, nseq, width):
    k = state.shape[1]
    return jnp.concatenate([jnp.zeros((nseq, 8 - k, width), F32), state.astype(F32)], axis=1)


def _trunk_layer(x, p, rel_bias, nseq, seq_rows, valid, tiles, attn_fn, conv_b_prev, conv_f_prev, h0t):
    tm, tm_merge, tm_ffn = tiles
    proj = rms_matmul(x, p['norm1'], p['w_in'], tm, 768)
    o_a = attn_fn(proj)
    o_b, h_t = ssd_mixer(proj, conv_b_prev, p['conv_b_w'], p['conv_b_bias'], p['dt_bias'], p['a_log'], p['d_skip'],
                         p['ssm_norm'], h0t, nseq, seq_rows, valid)
    o_c, v_rows = chunk_mlp_mixer(proj, p['v_ln_g'], p['v_ln_b'], p['w_spatial'], p['b_spatial'], seq_rows)
    merged = gated_merge(o_a, o_b, o_c, p['w_br_a'], p['w_br_b'], p['w_br_c'], proj, tm_merge, 512)
    x = matmul_residual(merged, p['w_out'], x, tm, 1024)
    tail_start = tm_ffn - 8 if valid == seq_rows else 0
    x, a_tail = conv_ffn(x, p['norm2'], p['w_ff_in'], conv_f_prev, p['ff_conv_w'], p['ff_conv_b'], p['w_ff_out'],
                         seq_rows, tm_ffn, 512, tail_start)
    return x, proj, a_tail, h_t, v_rows


def kernel(x_prompt, x_sample, cache_cmp_kv, cache_slc_kv, cache_win_kv, state_ssm, state_conv_ssm, state_conv_ffn,
           page_table, rel_bias, norm1, w_in, cmp_pe, cmp_w1, cmp_w2, conv_b_w, conv_b_bias, dt_bias, a_log, d_skip,
           ssm_norm, v_ln_g, v_ln_b, w_spatial, b_spatial, w_br_a, w_br_b, w_br_c, w_out, norm2, w_ff_in,
           ff_conv_w, ff_conv_b, w_ff_out, final_norm):
    bp, seq = x_prompt.shape[:2]
    bs, dec = x_sample.shape[:2]
    depth = w_in.shape[0]
    n_pool = cache_cmp_kv.shape[1]
    past = page_table.shape[1] * PAGE_SIZE
    st = SAMPLE_T
    kv_shape = (2, KV_A, HD_A)

    page_table = page_table.astype(jnp.int32)
    cmp_t = cache_cmp_kv.reshape(depth, n_pool, BLK_PER_PAGE, CMP_BLK, 2, KV_A, HD_A).transpose(0, 1, 4, 3, 5, 2, 6)
    cmp_t = cmp_t.reshape(depth, n_pool, 2, CMP_BLK, KV_A * BLK_PER_PAGE, HD_A)
    slc_t = cache_slc_kv.transpose(0, 1, 3, 4, 2, 5).astype(BF16)
    cache_win = cache_win_kv.reshape(depth, bs, WINDOW, ROW_W)

    xp = x_prompt.reshape(bp * seq, D_MODEL)
    xs = jnp.pad(x_sample, ((0, 0), (0, st - dec), (0, 0))).reshape(bs * st, D_MODEL)
    zeros_b = jnp.zeros((bp, 8, XBC_DIM), F32)
    zeros_f = jnp.zeros((bp, 8, D_FF), F32)
    zeros_h = jnp.zeros((bp, N_B, D_INNER_B), F32)
    ns_s = -(-(past + dec) // SLC_BLK)

    outs = [[] for _ in range(13)]
    for l in range(depth):
        p = {'norm1': norm1[l], 'w_in': _prep_w_in(w_in[l]), 'conv_b_w': jnp.pad(conv_b_w[l], ((0, 8 - CONV_B), (0, 0))),
             'conv_b_bias': conv_b_bias[l], 'dt_bias': dt_bias[l], 'a_log': a_log[l], 'd_skip': d_skip[l],
             'ssm_norm': ssm_norm[l], 'v_ln_g': v_ln_g[l], 'v_ln_b': v_ln_b[l], 'w_spatial': w_spatial[l],
             'b_spatial': b_spatial[l], 'w_br_a': w_br_a[l].astype(BF16), 'w_br_b': w_br_b[l].astype(BF16),
             'w_br_c': w_br_c[l].astype(BF16), 'w_out': w_out[l].astype(BF16), 'norm2': norm2[l],
             'w_ff_in': w_ff_in[l].astype(BF16), 'ff_conv_w': jnp.pad(ff_conv_w[l], ((0, 8 - CONV_F), (0, 0))),
             'ff_conv_b': ff_conv_b[l], 'w_ff_out': w_ff_out[l].astype(BF16)}
        pe, w1, w2 = cmp_pe[l], cmp_w1[l].astype(BF16), cmp_w2[l].astype(BF16)

        def attn_prompt(proj):
            kc, vc = compress_kv(proj.reshape(1, bp * seq, IN_PAD), 0, C_KVC // ROW_W, bp, seq, pe, w1, w2)
            return prompt_attention(proj, kc, vc, rel_bias, bp, seq)

        def attn_sample(proj):
            kc, vc = compress_kv_paged(page_table, cmp_t, l, pe, w1, w2)
            o_cmp, sel = compressed_attention(proj, kc, vc, rel_bias, bs, st, st, past, ns_s)
            return sample_attention(proj, o_cmp, sel, page_table, slc_t, l, cache_win, rel_bias, bs, st, past)

        xp, proj, a_tail, h_t, _ = _trunk_layer(xp, p, rel_bias, bp, seq, seq, (1024, 512, 512), attn_prompt,
                                           zeros_b, zeros_f, zeros_h)
        pr = proj.reshape(bp, seq, IN_PAD)
        outs[0].append(pr[:, :, C_KVC:C_KVC + ROW_W].reshape((bp, seq) + kv_shape))
        outs[1].append(pr[:, :, C_KVS:C_KVS + ROW_W].reshape((bp, seq) + kv_shape))
        outs[2].append(pr[:, seq - WINDOW:, C_KVW:C_KVW + ROW_W].reshape((bp, WINDOW) + kv_shape))
        outs[3].append(h_t.reshape(bp, N_B, H_B, HD_B).transpose(0, 2, 3, 1))
        outs[4].append(pr[:, seq - (CONV_B - 1):, C_XBC:C_XBC + XBC_DIM])
        outs[5].append(a_tail[:, 8 - (CONV_F - 1):])

        h0t = state_ssm[l].astype(F32).transpose(0, 3, 1, 2).reshape(bs, N_B, D_INNER_B)
        xs, proj, a_tail, h_t, v_rows = _trunk_layer(xs, p, rel_bias, bs, st, dec, (bs * st, bs * st, st), attn_sample,
                                                _state_rows(state_conv_ssm[l], bs, XBC_DIM),
                                                _state_rows(state_conv_ffn[l], bs, D_FF), h0t)
        pr = proj.reshape(bs, st, IN_PAD)
        kvw_new = pr[:, :dec, C_KVW:C_KVW + ROW_W].reshape((bs, dec) + kv_shape)
        outs[6].append(pr[:, :dec, C_KVC:C_KVC + ROW_W].reshape((bs, dec) + kv_shape))
        outs[7].append(pr[:, :dec, C_KVS:C_KVS + ROW_W].reshape((bs, dec) + kv_shape))
        outs[8].append(jnp.concatenate([cache_win_kv[l][:, dec:], kvw_new], axis=1))
        outs[9].append(h_t.reshape(bs, N_B, H_B, HD_B).transpose(0, 2, 3, 1))
        xbc_all = jnp.concatenate([state_conv_ssm[l], pr[:, :dec, C_XBC:C_XBC + XBC_DIM]], axis=1)
        outs[10].append(xbc_all[:, dec:])
        a_all = jnp.concatenate([state_conv_ffn[l], a_tail[:, :dec]], axis=1)
        outs[11].append(a_all[:, dec:])
        outs[12].append(v_rows.reshape(bs, st, D_C)[:, :dec])

    y_prompt = rms_norm_rows(xp, final_norm, 1024).reshape(bp, seq, D_MODEL)
    y_sample = rms_norm_rows(xs, final_norm, bs * st).reshape(bs, st, D_MODEL)[:, :dec]
    return (y_prompt, y_sample) + tuple(jnp.stack(o) for o in outs)
```

```python
import functools
import math

import numpy as np
import jax
import jax.numpy as jnp
from jax import lax
from jax.experimental import pallas as pl
from jax.experimental.pallas import tpu as pltpu

F32 = jnp.float32
BF16 = jnp.bfloat16

D_MODEL = 2048
DEPTH = 4
PAGE_SIZE = 128
H_A = 16
HD_A = 128
KV_A = 2
HPG_A = H_A // KV_A
CMP_BLK = 32
SLC_BLK = 64
N_SEL = 16
WINDOW = 512
FORCE_BONUS = 1.0e4
NEG_INF = -1.0e30
N_BUCKETS = 32
MAX_DIST = 128
D_INNER_B = 2048
HD_B = 64
H_B = D_INNER_B // HD_B
G_B = 4
N_B = 128
CONV_B = 4
SSD_CHUNK = 128
XBC_DIM = D_INNER_B + 2 * G_B * N_B
D_C = 2048
G_C = 8
CHUNK_C = 128
D_FF = 5632
CONV_F = 3
NORM_EPS = 1e-6

C_Q = 0
C_Z = 2048
C_U = 4096
C_V = 6144
C_GM = 8192
C_XBC = 14336
C_KVC = 17408
C_KVS = 17920
C_KVW = 18432
C_GN = 18944
C_DT = 19072
IN_PAD = 19200
_SEGS = ((H_A * HD_A, C_Q), (512, C_KVC), (512, C_KVS), (512, C_KVW), (3 * H_A, C_GN), (D_INNER_B, C_Z),
         (XBC_DIM, C_XBC), (H_B, C_DT), (D_C, C_U), (D_C, C_V), (3 * D_MODEL, C_GM))

VMEM_LIMIT = 56 * 1024 * 1024
SAMPLE_T = 16


def _cp(*sem):
    return pltpu.CompilerParams(dimension_semantics=sem, vmem_limit_bytes=VMEM_LIMIT)


def _gelu(x):
    return jax.nn.gelu(x)


def _t5_thresholds():
    n = np.arange(0, 4 * MAX_DIST)
    max_exact = N_BUCKETS // 2
    nf = np.maximum(n, 1).astype(np.float64)
    large = max_exact + (np.log(nf / max_exact) / math.log(MAX_DIST / max_exact) * (N_BUCKETS - max_exact)).astype(np.int64)
    b = np.where(n < max_exact, n, np.minimum(large, N_BUCKETS - 1))
    return [int(np.argmax(b >= k)) for k in range(1, N_BUCKETS)]


_T5_THR = _t5_thresholds()


def _t5_bias(dist, rb_ref, head):
    out = jnp.full(dist.shape, rb_ref[0, head], F32)
    for k, thr in enumerate(_T5_THR):
        out = jnp.where(dist >= thr, rb_ref[k + 1, head], out)
    return out


def _rmsmm_kernel(x_ref, g_ref, w_ref, o_ref, xn_ref):
    @pl.when(pl.program_id(1) == 0)
    def _():
        x = x_ref[...]
        y = x * lax.rsqrt(jnp.mean(x * x, axis=-1, keepdims=True) + NORM_EPS)
        xn_ref[...] = (y * g_ref[...]).astype(BF16)

    o_ref[...] = jnp.dot(xn_ref[...], w_ref[...], preferred_element_type=F32)


def rms_matmul(x, g, w, tm, tn):
    r, d = x.shape
    n = w.shape[1]
    return pl.pallas_call(
        _rmsmm_kernel,
        grid=(r // tm, n // tn),
        in_specs=[pl.BlockSpec((tm, d), lambda i, j: (i, 0)),
                  pl.BlockSpec((1, d), lambda i, j: (0, 0)),
                  pl.BlockSpec((d, tn), lambda i, j: (0, j))],
        out_specs=pl.BlockSpec((tm, tn), lambda i, j: (i, j)),
        out_shape=jax.ShapeDtypeStruct((r, n), F32),
        scratch_shapes=[pltpu.VMEM((tm, d), BF16)],
        compiler_params=_cp("arbitrary", "arbitrary"),
        name="rms_matmul",
    )(x, g.reshape(1, d), w)


def _merge_kernel(oa_ref, ob_ref, oc_ref, wa_ref, wb_ref, wc_ref, ga_ref, gb_ref, gc_ref, o_ref):
    acc = jax.nn.sigmoid(ga_ref[...]) * jnp.dot(oa_ref[...], wa_ref[...], preferred_element_type=F32)
    acc = acc + jax.nn.sigmoid(gb_ref[...]) * jnp.dot(ob_ref[...], wb_ref[...], preferred_element_type=F32)
    acc = acc + jax.nn.sigmoid(gc_ref[...]) * jnp.dot(oc_ref[...], wc_ref[...], preferred_element_type=F32)
    o_ref[...] = acc.astype(BF16)


def gated_merge(o_a, o_b, o_c, w_a, w_b, w_c, proj, tm, tn):
    r, d = o_a.shape
    n = w_a.shape[1]
    gblk = C_GM // tn
    per = D_MODEL // tn
    lhs = pl.BlockSpec((tm, d), lambda i, j: (i, 0))
    wsp = pl.BlockSpec((d, tn), lambda i, j: (0, j))

    def gate(b):
        return pl.BlockSpec((tm, tn), lambda i, j: (i, gblk + b * per + j))

    return pl.pallas_call(
        _merge_kernel,
        grid=(r // tm, n // tn),
        in_specs=[lhs, lhs, lhs, wsp, wsp, wsp, gate(0), gate(1), gate(2)],
        out_specs=pl.BlockSpec((tm, tn), lambda i, j: (i, j)),
        out_shape=jax.ShapeDtypeStruct((r, n), BF16),
        compiler_params=_cp("arbitrary", "arbitrary"),
        name="gated_merge",
    )(o_a, o_b, o_c, w_a, w_b, w_c, proj, proj, proj)


def _mmres_kernel(a_ref, w_ref, r_ref, o_ref):
    o_ref[...] = r_ref[...] + jnp.dot(a_ref[...], w_ref[...], preferred_element_type=F32)


def matmul_residual(a, w, res, tm, tn):
    r, k = a.shape
    n = w.shape[1]
    return pl.pallas_call(
        _mmres_kernel,
        grid=(r // tm, n // tn),
        in_specs=[pl.BlockSpec((tm, k), lambda i, j: (i, 0)),
                  pl.BlockSpec((k, tn), lambda i, j: (0, j)),
                  pl.BlockSpec((tm, tn), lambda i, j: (i, j))],
        out_specs=pl.BlockSpec((tm, tn), lambda i, j: (i, j)),
        out_shape=jax.ShapeDtypeStruct((r, n), F32),
        compiler_params=_cp("arbitrary", "arbitrary"),
        name="matmul_residual",
    )(a, w, res)


def _ffn_out_kernel(tiles_per_seq, a_ref, b_ref, prev_ref, cw_ref, cb_ref, w_ref, r_ref, o_ref,
                    ext_ref, carry_ref):
    i = pl.program_id(0)
    k = pl.program_id(1)
    tm = a_ref.shape[0]

    @pl.when(i % tiles_per_seq == 0)
    def _():
        ext_ref[0:8, :] = prev_ref[0]

    @pl.when(i % tiles_per_seq != 0)
    def _():
        ext_ref[0:8, :] = carry_ref[k]

    a = a_ref[...]
    ext_ref[8:8 + tm, :] = a
    carry_ref[k] = a_ref[tm - 8:tm, :]
    cw = cw_ref[...]
    y = cb_ref[...] + ext_ref[6:6 + tm, :] * cw[0:1]
    y = y + ext_ref[7:7 + tm, :] * cw[1:2]
    y = y + a * cw[2:3]
    h = (_gelu(y) * b_ref[...]).astype(BF16)
    part = jnp.dot(h, w_ref[...], preferred_element_type=F32)

    @pl.when(k == 0)
    def _():
        o_ref[...] = r_ref[...] + part

    @pl.when(k != 0)
    def _():
        o_ref[...] += part


def ffn_out(h, prev, conv_w, conv_b, w, res, seq_len, tm, tk):
    r = h.shape[0]
    n = w.shape[1]
    nk = D_FF // tk
    kernel = functools.partial(_ffn_out_kernel, seq_len // tm)
    return pl.pallas_call(
        kernel,
        grid=(r // tm, nk),
        in_specs=[pl.BlockSpec((tm, tk), lambda i, k: (i, k)),
                  pl.BlockSpec((tm, tk), lambda i, k: (i, nk + k)),
                  pl.BlockSpec((1, 8, tk), lambda i, k: (i * tm // seq_len, 0, k)),
                  pl.BlockSpec((8, tk), lambda i, k: (0, k)),
                  pl.BlockSpec((1, tk), lambda i, k: (0, k)),
                  pl.BlockSpec((tk, n), lambda i, k: (k, 0)),
                  pl.BlockSpec((tm, n), lambda i, k: (i, 0))],
        out_specs=pl.BlockSpec((tm, n), lambda i, k: (i, 0)),
        out_shape=jax.ShapeDtypeStruct((r, n), F32),
        scratch_shapes=[pltpu.VMEM((tm + 8, tk), F32), pltpu.VMEM((nk, 8, tk), F32)],
        compiler_params=_cp("arbitrary", "arbitrary"),
        name="ffn_out",
    )(h, h, prev, conv_w, conv_b.reshape(1, D_FF), w, res)


def _rms_kernel(x_ref, g_ref, o_ref):
    x = x_ref[...]
    o_ref[...] = x * lax.rsqrt(jnp.mean(x * x, axis=-1, keepdims=True) + NORM_EPS) * g_ref[...]


def rms_norm_rows(x, g, tm):
    r, d = x.shape
    return pl.pallas_call(
        _rms_kernel,
        grid=(r // tm,),
        in_specs=[pl.BlockSpec((tm, d), lambda i: (i, 0)), pl.BlockSpec((1, d), lambda i: (0, 0))],
        out_specs=pl.BlockSpec((tm, d), lambda i: (i, 0)),
        out_shape=jax.ShapeDtypeStruct((r, d), F32),
        compiler_params=_cp("arbitrary"),
        name="final_rms_norm",
    )(x, g.reshape(1, d))


def _ssd_kernel(rows_in, valid_len, x_ref, b_ref, c_ref, z_ref, dt_ref, prev_ref, cw_ref, cb_ref, dtb_ref, alog_ref,
                dskip_ref, norm_ref, h0_ref, tril_ref, o_ref, ht_ref, ext_ref, st_ref, tt_ref, y_ref, xw_ref):
    L = SSD_CHUNK
    c = pl.program_id(1)

    @pl.when(c == 0)
    def _():
        ext_ref[0:8, :] = prev_ref[0]
        ht_ref[...] = h0_ref[...]

    @pl.when(c != 0)
    def _():
        ext_ref[0:8, :] = ext_ref[L:L + 8, :]

    if rows_in < L:
        ext_ref[8 + rows_in:8 + L, :] = jnp.zeros((L - rows_in, XBC_DIM), F32)
    ext_ref[8:8 + rows_in, 0:D_INNER_B] = x_ref[...]
    ext_ref[8:8 + rows_in, D_INNER_B:D_INNER_B + G_B * N_B] = b_ref[...]
    ext_ref[8:8 + rows_in, D_INNER_B + G_B * N_B:XBC_DIM] = c_ref[...]

    cw = cw_ref[...]
    y = cb_ref[...] + ext_ref[5:5 + L, :] * cw[0:1]
    y = y + ext_ref[6:6 + L, :] * cw[1:2]
    y = y + ext_ref[7:7 + L, :] * cw[2:3]
    y = y + ext_ref[8:8 + L, :] * cw[3:4]
    xbc = y * jax.nn.sigmoid(y)

    if rows_in < L:
        dt_raw = jnp.concatenate([dt_ref[...], jnp.zeros((L - rows_in, 128), F32)], axis=0)
    else:
        dt_raw = dt_ref[...]
    d_all = jax.nn.softplus(dt_raw + dtb_ref[...])
    if valid_len < L:
        row = lax.broadcasted_iota(jnp.int32, (L, 128), 0)
        d_all = jnp.where(row < valid_len, d_all, 0.0)
    a_row = -jnp.exp(alog_ref[...])
    tril = tril_ref[...]
    cs = jnp.dot(tril, d_all * a_row, preferred_element_type=F32, precision=lax.Precision.HIGHEST)
    st_ref[0] = cs
    st_ref[1] = jnp.exp(cs)
    st_ref[2] = jnp.exp(cs[L - 1:L, :] - cs) * d_all
    tt_ref[0] = cs.T
    tt_ref[1] = d_all.T
    lower = tril > 0.5

    for g in range(G_B):
        b_g = xbc[:, D_INNER_B + g * N_B:D_INNER_B + (g + 1) * N_B]
        c_g = xbc[:, D_INNER_B + G_B * N_B + g * N_B:D_INNER_B + G_B * N_B + (g + 1) * N_B]
        cgb = c_g.astype(BF16)
        cb = lax.dot_general(cgb, b_g.astype(BF16), (((1,), (1,)), ((), ())), preferred_element_type=F32)
        bt = b_g.T.astype(BF16)
        gs = slice(g * 512, (g + 1) * 512)
        hg = ht_ref[0, :, gs]
        yoff = jnp.dot(cgb, hg.astype(BF16), preferred_element_type=F32)
        for hh in range(8):
            h = g * 8 + hh
            hs = slice(h * HD_B, (h + 1) * HD_B)
            x_h = xbc[:, hs]
            seg = st_ref[0, :, h:h + 1] - tt_ref[0, h:h + 1, :]
            decay = jnp.exp(jnp.where(lower, seg, NEG_INF))
            m = (cb * decay * tt_ref[1, h:h + 1, :]).astype(BF16)
            y_h = jnp.dot(m, x_h.astype(BF16), preferred_element_type=F32)
            y_h = y_h + yoff[:, hh * HD_B:(hh + 1) * HD_B] * st_ref[1, :, h:h + 1]
            y_ref[:, hs] = y_h + dskip_ref[:, hs] * x_h
            xw_ref[:, hh * HD_B:(hh + 1) * HD_B] = (x_h * st_ref[2, :, h:h + 1]).astype(BF16)
        upd = jnp.dot(bt, xw_ref[...], preferred_element_type=F32)
        for hh in range(8):
            h = g * 8 + hh
            hs = slice(h * HD_B, (h + 1) * HD_B)
            ht_ref[0, :, hs] = ht_ref[0, :, hs] * st_ref[1, L - 1:L, h:h + 1] + upd[:, hh * HD_B:(hh + 1) * HD_B]

    if rows_in < L:
        z = jnp.concatenate([z_ref[...], jnp.zeros((L - rows_in, D_INNER_B), F32)], axis=0)
    else:
        z = z_ref[...]
    yz = y_ref[...] * (z * jax.nn.sigmoid(z))
    for g in range(G_B):
        gs = slice(g * 512, (g + 1) * 512)
        v = yz[:, gs]
        out = v * lax.rsqrt(jnp.mean(v * v, axis=-1, keepdims=True) + NORM_EPS) * norm_ref[:, gs]
        o_ref[:, gs] = out[0:rows_in].astype(BF16)


def ssd_mixer(proj, prev, conv_w, conv_b, dt_bias, a_log, d_skip, ssm_norm, h0t, nseq, seq_rows, valid_len):
    rows_in = min(seq_rows, SSD_CHUNK)
    nchunk = seq_rows // rows_in
    r = nseq * seq_rows
    tril = jnp.asarray(np.tril(np.ones((SSD_CHUNK, SSD_CHUNK), np.float32)))
    pad128 = lambda v: jnp.pad(v.astype(F32), (0, 128 - v.shape[0])).reshape(1, 128)
    kernel = functools.partial(_ssd_kernel, rows_in, min(valid_len, SSD_CHUNK))
    rowblk = lambda n, c: n * nchunk + c
    const = lambda n, c: (0, 0)
    return pl.pallas_call(
        kernel,
        grid=(nseq, nchunk),
        in_specs=[pl.BlockSpec((rows_in, D_INNER_B), lambda n, c: (rowblk(n, c), C_XBC // 2048)),
                  pl.BlockSpec((rows_in, 512), lambda n, c: (rowblk(n, c), (C_XBC + 2048) // 512)),
                  pl.BlockSpec((rows_in, 512), lambda n, c: (rowblk(n, c), (C_XBC + 2560) // 512)),
                  pl.BlockSpec((rows_in, D_INNER_B), lambda n, c: (rowblk(n, c), C_Z // 2048)),
                  pl.BlockSpec((rows_in, 128), lambda n, c: (rowblk(n, c), C_DT // 128)),
                  pl.BlockSpec((1, 8, XBC_DIM), lambda n, c: (n, 0, 0)),
                  pl.BlockSpec((8, XBC_DIM), const),
                  pl.BlockSpec((1, XBC_DIM), const),
                  pl.BlockSpec((1, 128), const),
                  pl.BlockSpec((1, 128), const),
                  pl.BlockSpec((1, D_INNER_B), const),
                  pl.BlockSpec((1, D_INNER_B), const),
                  pl.BlockSpec((1, N_B, D_INNER_B), lambda n, c: (n, 0, 0)),
                  pl.BlockSpec((SSD_CHUNK, SSD_CHUNK), const)],
        out_specs=[pl.BlockSpec((rows_in, D_INNER_B), lambda n, c: (rowblk(n, c), 0)),
                   pl.BlockSpec((1, N_B, D_INNER_B), lambda n, c: (n, 0, 0))],
        out_shape=[jax.ShapeDtypeStruct((r, D_INNER_B), BF16),
                   jax.ShapeDtypeStruct((nseq, N_B, D_INNER_B), F32)],
        scratch_shapes=[pltpu.VMEM((SSD_CHUNK + 8, XBC_DIM), F32),
                        pltpu.VMEM((3, SSD_CHUNK, 128), F32),
                        pltpu.VMEM((2, 128, SSD_CHUNK), F32),
                        pltpu.VMEM((SSD_CHUNK, D_INNER_B), F32),
                        pltpu.VMEM((SSD_CHUNK, 512), BF16)],
        compiler_params=_cp("arbitrary", "arbitrary"),
        name="ssd_mixer",
    )(proj, proj, proj, proj, proj, prev, conv_w, conv_b.reshape(1, XBC_DIM), pad128(dt_bias), pad128(a_log),
      jnp.repeat(d_skip.astype(F32), HD_B).reshape(1, D_INNER_B), ssm_norm.reshape(1, D_INNER_B), h0t, tril)


def _cmlp_kernel(rows_in, u_ref, v_ref, lg_ref, lb_ref, ws_ref, bst_ref, o_ref, vn_ref):
    L = CHUNK_C
    v = _gelu(v_ref[...])
    mu = jnp.mean(v, axis=-1, keepdims=True)
    vc = v - mu
    var = jnp.mean(vc * vc, axis=-1, keepdims=True)
    vn = vc * lax.rsqrt(var + NORM_EPS) * lg_ref[...] + lb_ref[...]
    vn_ref[...] = vn
    u = _gelu(u_ref[...])
    if rows_in < L:
        vn = jnp.concatenate([vn, jnp.zeros((L - rows_in, D_C), F32)], axis=0)
    vnb = vn.astype(BF16)
    row = lax.broadcasted_iota(jnp.int32, (L, L), 0)
    col = lax.broadcasted_iota(jnp.int32, (L, L), 1)
    gw = D_C // G_C
    for g in range(G_C):
        w = jnp.where(col <= row, ws_ref[g], 0.0).astype(BF16)
        s = jnp.dot(w, vnb[:, g * gw:(g + 1) * gw], preferred_element_type=F32) + bst_ref[:, g:g + 1]
        o_ref[:, g * gw:(g + 1) * gw] = (u[:, g * gw:(g + 1) * gw] * s[0:rows_in]).astype(BF16)


def chunk_mlp_mixer(proj, ln_g, ln_b, w_s, b_s, seq_rows):
    r = proj.shape[0]
    rows_in = min(seq_rows, CHUNK_C)
    kernel = functools.partial(_cmlp_kernel, rows_in)
    bst = jnp.pad(b_s.T.astype(F32), ((0, 0), (0, 128 - G_C)))
    return pl.pallas_call(
        kernel,
        grid=(r // rows_in,),
        in_specs=[pl.BlockSpec((rows_in, D_C), lambda i: (i, C_U // 2048)),
                  pl.BlockSpec((rows_in, D_C), lambda i: (i, C_V // 2048)),
                  pl.BlockSpec((1, D_C), lambda i: (0, 0)),
                  pl.BlockSpec((1, D_C), lambda i: (0, 0)),
                  pl.BlockSpec((G_C, CHUNK_C, CHUNK_C), lambda i: (0, 0, 0)),
                  pl.BlockSpec((CHUNK_C, 128), lambda i: (0, 0))],
        out_specs=[pl.BlockSpec((rows_in, D_C), lambda i: (i, 0)),
                   pl.BlockSpec((rows_in, D_C), lambda i: (i, 0))],
        out_shape=[jax.ShapeDtypeStruct((r, D_C), BF16), jax.ShapeDtypeStruct((r, D_C), F32)],
        compiler_params=_cp("arbitrary"),
        name="chunk_mlp_mixer",
    )(proj, proj, ln_g.reshape(1, D_C), ln_b.reshape(1, D_C), w_s, bst)


ROW_W = 2 * KV_A * HD_A
SCALE = HD_A ** -0.5
MASK_BIG = 1.0e30
M_INIT = -2.0e30
EXP2_SCALE = SCALE * math.log2(math.e)


CMP_ROWS = 4096
CHUNK_PAGES = CMP_ROWS // PAGE_SIZE
BLK_PER_PAGE = PAGE_SIZE // CMP_BLK


def _paged_chunk_pipeline(pt_ref, copies_for, n_chunks_per_seq):
    n = pl.program_id(0)
    k = pl.program_id(1)
    step = n * n_chunks_per_seq + k
    total = pl.num_programs(0) * n_chunks_per_seq
    slot = step % 2

    @pl.when(step == 0)
    def _():
        for cp in copies_for(n, k, slot):
            cp.start()

    @pl.when(step + 1 < total)
    def _():
        wrap = k + 1 == n_chunks_per_seq
        for cp in copies_for(jnp.where(wrap, n + 1, n), jnp.where(wrap, 0, k + 1), 1 - slot):
            cp.start()

    for cp in copies_for(n, k, slot):
        cp.wait()
    return slot


def _cmpmlp_paged_kernel(layer, pt_ref, cache_ref, pe_ref, w1_ref, w2_ref, kc_ref, vc_ref, buf_ref, sem):
    nkc = pl.num_programs(1)

    def copies_for(n, k, slot):
        return [pltpu.make_async_copy(cache_ref.at[layer, pt_ref[n, k * CHUNK_PAGES + p]],
                                      buf_ref.at[slot, :, :, pl.ds(p * 8, 8), :], sem.at[slot])
                for p in range(CHUNK_PAGES)]

    slot = _paged_chunk_pipeline(pt_ref, copies_for, nkc)
    for c in range(2):
        xs = [(buf_ref[slot, c, s] + pe_ref[s * 2 + c]).astype(BF16) for s in range(CMP_BLK)]
        acc = jnp.dot(jnp.concatenate(xs, axis=1), w1_ref[c], preferred_element_type=F32)
        out = jnp.dot(_gelu(acc).astype(BF16), w2_ref[c], preferred_element_type=F32)
        dst = kc_ref if c == 0 else vc_ref
        dst[0] = out.reshape(CHUNK_PAGES, KV_A * BLK_PER_PAGE, HD_A)


def compress_kv_paged(page_table, cache_t, layer, pe, w1, w2):
    nseq, npages = page_table.shape
    nkc = npages // CHUNK_PAGES
    rows8 = KV_A * BLK_PER_PAGE
    shape = jax.ShapeDtypeStruct((nseq, npages, rows8, HD_A), F32)
    ospec = pl.BlockSpec((1, CHUNK_PAGES, rows8, HD_A), lambda n, k, pt: (n, k, 0, 0))
    kernel = functools.partial(_cmpmlp_paged_kernel, layer)
    kc, vc = pl.pallas_call(
        kernel,
        grid_spec=pltpu.PrefetchScalarGridSpec(
            num_scalar_prefetch=1, grid=(nseq, nkc),
            in_specs=[pl.BlockSpec(memory_space=pl.ANY),
                      pl.BlockSpec((CMP_BLK * 2, 1, HD_A), lambda n, k, pt: (0, 0, 0)),
                      pl.BlockSpec((2, CMP_BLK * HD_A, HD_A), lambda n, k, pt: (0, 0, 0)),
                      pl.BlockSpec((2, HD_A, HD_A), lambda n, k, pt: (0, 0, 0))],
            out_specs=[ospec, ospec],
            scratch_shapes=[pltpu.VMEM((2, 2, CMP_BLK, CHUNK_PAGES * rows8, HD_A), F32),
                            pltpu.SemaphoreType.DMA((2,))]),
        out_shape=[shape, shape],
        compiler_params=_cp("arbitrary", "arbitrary"),
        name="compress_kv_paged",
    )(page_table, cache_t, pe.reshape(CMP_BLK * 2, 1, HD_A), w1.reshape(2, CMP_BLK * HD_A, HD_A), w2)

    def regroup(x):
        x = x.reshape(nseq, npages, KV_A, BLK_PER_PAGE, HD_A).transpose(0, 2, 1, 3, 4)
        return x.reshape(nseq, KV_A, npages * BLK_PER_PAGE, HD_A).astype(BF16)

    return regroup(kc), regroup(vc)


def _cmpmlp_kernel(x00_ref, x01_ref, x10_ref, x11_ref, pe_ref, w1_ref, w2_ref, kc_ref, vc_ref):
    x_refs = ((x00_ref, x01_ref), (x10_ref, x11_ref))
    nblk = x00_ref.shape[1] // CMP_BLK
    for c in range(2):
        def body(s, acc):
            xs = [x_refs[c][g][0, pl.ds(s, nblk, stride=CMP_BLK), :] for g in range(KV_A)]
            x = jnp.concatenate(xs, axis=0) + pe_ref[s * 2 + c]
            return acc + jnp.dot(x.astype(BF16), w1_ref[c, s], preferred_element_type=F32)

        acc = lax.fori_loop(0, CMP_BLK, body, jnp.zeros((KV_A * nblk, HD_A), F32))
        out = jnp.dot(_gelu(acc).astype(BF16), w2_ref[c], preferred_element_type=F32)
        dst = kc_ref if c == 0 else vc_ref
        for g in range(KV_A):
            dst[0, g] = out[g * nblk:(g + 1) * nblk].astype(BF16)


def compress_kv(src, lead, col_blk, nseq, seq_rows, pe, w1, w2):
    rows = min(CMP_ROWS, seq_rows)
    nchunk = seq_rows // rows
    nb = seq_rows // CMP_BLK
    shape = jax.ShapeDtypeStruct((nseq, KV_A, nb, HD_A), BF16)
    ospec = pl.BlockSpec((1, KV_A, rows // CMP_BLK, HD_A), lambda n, c: (n, 0, c, 0))

    def xspec(part):
        return pl.BlockSpec((1, rows, HD_A), lambda n, c: (lead, n * nchunk + c, col_blk * (ROW_W // HD_A) + part))

    return pl.pallas_call(
        _cmpmlp_kernel,
        grid=(nseq, nchunk),
        in_specs=[xspec(0), xspec(1), xspec(2), xspec(3),
                  pl.BlockSpec((CMP_BLK * 2, 1, HD_A), lambda n, c: (0, 0, 0)),
                  pl.BlockSpec((2, CMP_BLK, HD_A, HD_A), lambda n, c: (0, 0, 0, 0)),
                  pl.BlockSpec((2, HD_A, HD_A), lambda n, c: (0, 0, 0))],
        out_specs=[ospec, ospec],
        out_shape=[shape, shape],
        compiler_params=_cp("arbitrary", "arbitrary"),
        name="compress_kv",
    )(src, src, src, src, pe.reshape(CMP_BLK * 2, 1, HD_A), w1, w2)


def _select_blocks(imp, q0, ns):
    nsp, tqp = imp.shape
    j = lax.broadcasted_iota(jnp.int32, (nsp, tqp), 0)
    cur = lax.shift_right_logical(q0 + lax.broadcasted_iota(jnp.int32, (nsp, tqp), 1), 6)
    forced = (j == 0) | (j == cur) | (j == cur - 1)
    score = jnp.where(j <= cur, imp + jnp.where(forced, FORCE_BONUS, 0.0), -1.0)
    score = jnp.where(j < ns, score, -2.0)
    sel = jnp.zeros((nsp, tqp), F32)
    for _ in range(N_SEL):
        top = jnp.max(score, axis=0, keepdims=True)
        idx = jnp.min(jnp.where(score == top, j, 1 << 30), axis=0, keepdims=True)
        hit = j == idx
        sel = jnp.where(hit, 1.0, sel)
        score = jnp.where(hit, -3.0, score)
    return sel


def _cmp_bias_table(tb_ref, rb_ref, g, tq, nb):
    near = (MAX_DIST + CMP_BLK - 1) // CMP_BLK
    qi = lax.broadcasted_iota(jnp.int32, (tq, nb), 0)
    c = lax.broadcasted_iota(jnp.int32, (tq, nb), 1)
    d = jnp.where(c < near + tq // CMP_BLK, qi + near * CMP_BLK - c * CMP_BLK - (CMP_BLK - 1), 4 * MAX_DIST)
    for h in range(HPG_A):
        tb_ref[h] = _t5_bias(d, rb_ref, g * HPG_A + h)


def _cmp_bias_shift(q0, nb):
    near = (MAX_DIST + CMP_BLK - 1) // CMP_BLK
    return lax.rem(q0 // CMP_BLK - near + nb, nb)


def _cmpattn_kernel(q_pos0, ns, rb_ref, q_ref, kc_ref, vc_ref, pool_ref, o_ref, sel_ref):
    g = pl.program_id(1)
    qt = pl.program_id(2)
    tq = q_ref.shape[0]
    nb = kc_ref.shape[2]
    q0 = q_pos0 + qt * tq
    qpos = q0 + lax.broadcasted_iota(jnp.int32, (tq, nb), 0)
    dist = qpos - (lax.broadcasted_iota(jnp.int32, (tq, nb), 1) * CMP_BLK + (CMP_BLK - 1))
    vis = dist >= 0
    anyvis = ((q0 + lax.broadcasted_iota(jnp.int32, (tq, 1), 0)) >= CMP_BLK - 1).astype(F32)
    kc = kc_ref[0, 0]
    vc = vc_ref[0, 0]
    psum = jnp.zeros((tq, nb), F32)
    for h in range(HPG_A):
        hs = slice(h * HD_A, (h + 1) * HD_A)
        lc = lax.dot_general(q_ref[:, hs].astype(BF16), kc, (((1,), (1,)), ((), ())), preferred_element_type=F32)
        lc = lc * SCALE + _t5_bias(dist, rb_ref, g * HPG_A + h)
        lc = jnp.where(vis, lc, NEG_INF)
        e = jnp.exp(lc - jnp.max(lc, axis=-1, keepdims=True))
        p = e / jnp.sum(e, axis=-1, keepdims=True) * anyvis
        o_ref[:, hs] = jnp.dot(p.astype(BF16), vc, preferred_element_type=F32)
        psum = psum + p

    tqp = max(tq, 128)
    if tq < tqp:
        psum = jnp.concatenate([psum, jnp.zeros((tqp - tq, nb), F32)], axis=0)
    imp = lax.dot_general(pool_ref[...], psum, (((1,), (1,)), ((), ())), preferred_element_type=F32,
                          precision=lax.Precision.HIGHEST)
    sel = _select_blocks(imp, q0, ns)
    sel_ref[0] = sel.T[0:tq].astype(BF16)


def compressed_attention(proj, kc, vc, rel_bias, nseq, seq_rows, tq, q_pos0, ns):
    r = nseq * seq_rows
    nb = kc.shape[2]
    nsp = -(-ns // 128) * 128
    nqt = seq_rows // tq
    pool = np.zeros((nsp, nb), np.float32)
    pool[np.arange(nb) // (SLC_BLK // CMP_BLK), np.arange(nb)] = 1.0
    kernel = functools.partial(_cmpattn_kernel, q_pos0, ns)
    kvspec = pl.BlockSpec((1, 1, nb, HD_A), lambda n, g, t: (n, g, 0, 0))
    sel_spec = pl.BlockSpec((1, tq, nsp), lambda n, g, t: (g, n * nqt + t, 0))
    sel_shape = jax.ShapeDtypeStruct((KV_A, r, nsp), BF16)
    return pl.pallas_call(
        kernel,
        grid=(nseq, KV_A, nqt),
        in_specs=[pl.BlockSpec(memory_space=pltpu.SMEM),
                  pl.BlockSpec((tq, HPG_A * HD_A), lambda n, g, t: (n * nqt + t, g)),
                  kvspec, kvspec,
                  pl.BlockSpec((nsp, nb), lambda n, g, t: (0, 0))],
        out_specs=[pl.BlockSpec((tq, HPG_A * HD_A), lambda n, g, t: (n * nqt + t, g)), sel_spec],
        out_shape=[jax.ShapeDtypeStruct((r, H_A * HD_A), F32), sel_shape],
        compiler_params=_cp("arbitrary", "arbitrary", "arbitrary"),
        name="compressed_attention",
    )(rel_bias, proj, kc, vc, jnp.asarray(pool))


def _softmax_init(m_ref, l_ref, acc_ref):
    m_ref[...] = jnp.full(m_ref.shape, M_INIT, F32)
    l_ref[...] = jnp.zeros(l_ref.shape, F32)
    acc_ref[...] = jnp.zeros(acc_ref.shape, F32)


PTQ = 128


def _attn_sweep_t(k_ref, q_t, vt_ref, bias_ref, tab_of, first, last, s_ref, p_ref, m_ref, l_ref, acc_ref,
                  blockmask_ref=None):
    tq = PTQ
    rows = q_t.shape[1]

    def scores(kc):
        k0 = pl.multiple_of(kc * tq, tq)
        return jnp.dot(k_ref[pl.ds(k0, tq), :], q_t, preferred_element_type=F32)

    _softmax_init(m_ref, l_ref, acc_ref)
    s_ref[0] = scores(first)
    p_ref[...] = jnp.zeros(p_ref.shape, BF16)

    def body(i, carry):
        kc = first + i
        slot = i % 2
        s = s_ref[slot] + bias_ref[tab_of(kc)]
        if blockmask_ref is not None:
            per = tq // SLC_BLK
            s = s + jnp.concatenate([jnp.broadcast_to(blockmask_ref[kc * per + j], (SLC_BLK, rows))
                                     for j in range(per)], axis=0)
        s_ref[1 - slot] = scores(jnp.minimum(kc + 1, last))
        pv = jnp.dot(vt_ref[jnp.maximum(kc - 1, first)], p_ref[...], preferred_element_type=F32)
        m_prev = m_ref[...]
        m_new = jnp.maximum(m_prev, jnp.max(s, axis=0, keepdims=True))
        alpha = jnp.exp2((m_prev - m_new) * EXP2_SCALE)
        p = jnp.exp2((s - m_new) * EXP2_SCALE)
        l_ref[...] = alpha * l_ref[...] + jnp.sum(p, axis=0, keepdims=True)
        acc_ref[...] = alpha * (acc_ref[...] + pv)
        p_ref[...] = p.astype(BF16)
        m_ref[...] = m_new
        return carry

    lax.fori_loop(0, last - first + 1, body, 0)
    acc_ref[...] += jnp.dot(vt_ref[last], p_ref[...], preferred_element_type=F32)


def _prompt_attn_kernel(seq, ns, rb_ref, q_ref, kc_ref, vc_ref, pool_ref, ks_ref, vs_ref, kw_ref, vw_ref, gn_ref,
                        o_ref, ksb_ref, vst_ref, kwb_ref, vwt_ref, btab_ref, tb_ref, s_ref, p_ref, m_ref, l_ref,
                        acc_ref, os_ref, oc_ref, selrow_ref):
    g = pl.program_id(1)
    qt = pl.program_id(2)
    tq = PTQ
    nb = kc_ref.shape[2]

    @pl.when(qt == 0)
    def _():
        _cmp_bias_table(tb_ref, rb_ref, g, tq, nb)
        ksb_ref[...] = ks_ref[...].astype(BF16)
        kwb_ref[...] = kw_ref[...].astype(BF16)

        def transpose_chunk(c, carry):
            r0 = pl.multiple_of(c * tq, tq)
            vst_ref[c] = vs_ref[pl.ds(r0, tq), :].T.astype(BF16)
            vwt_ref[c] = vw_ref[pl.ds(r0, tq), :].T.astype(BF16)
            return carry

        lax.fori_loop(0, seq // tq, transpose_chunk, 0)
        kj = lax.broadcasted_iota(jnp.int32, (tq, tq), 0)
        qi = lax.broadcasted_iota(jnp.int32, (tq, tq), 1)
        for h in range(HPG_A):
            head = g * HPG_A + h
            cols = slice(h * tq, (h + 1) * tq)
            inv = 1.0 / SCALE
            far = jnp.full((tq, tq), rb_ref[N_BUCKETS - 1, head], F32) * inv
            btab_ref[0, :, cols] = _t5_bias(qi - kj + tq, rb_ref, head) * inv
            btab_ref[1, :, cols] = jnp.where(kj <= qi, _t5_bias(qi - kj, rb_ref, head) * inv, NEG_INF)
            btab_ref[2, :, cols] = jnp.where(kj > qi, far, NEG_INF)
            btab_ref[3, :, cols] = far

    q_t = jnp.concatenate([q_ref[:, h * HD_A:(h + 1) * HD_A].T.astype(BF16) for h in range(HPG_A)], axis=1)

    q0 = qt * tq
    shift = _cmp_bias_shift(q0, nb)
    bias_c = jnp.concatenate([pltpu.roll(tb_ref[h], shift, 1).T for h in range(HPG_A)], axis=1)
    blk = lax.broadcasted_iota(jnp.int32, (nb, tq), 0)
    qpos = q0 + lax.broadcasted_iota(jnp.int32, (nb, tq), 1)
    vis = jnp.concatenate([qpos >= blk * CMP_BLK + (CMP_BLK - 1)] * HPG_A, axis=1)
    anyvis = jnp.concatenate([(qpos[0:1] >= CMP_BLK - 1).astype(F32)] * HPG_A, axis=1)
    lc = jnp.dot(kc_ref[0, 0], q_t, preferred_element_type=F32) * SCALE + bias_c
    lc = jnp.where(vis, lc, NEG_INF)
    e = jnp.exp(lc - jnp.max(lc, axis=0, keepdims=True))
    pc = e / jnp.sum(e, axis=0, keepdims=True) * anyvis
    vc_t = vc_ref[0, 0].astype(F32).T.astype(BF16)
    oc_ref[...] = jnp.dot(vc_t, pc.astype(BF16), preferred_element_type=F32)
    psum = pc[:, 0:tq]
    for h in range(1, HPG_A):
        psum = psum + pc[:, h * tq:(h + 1) * tq]
    imp = jnp.dot(pool_ref[...], psum, preferred_element_type=F32, precision=lax.Precision.HIGHEST)
    sel = _select_blocks(imp, q0, ns)
    selterm = jnp.concatenate([(sel - 1.0) * MASK_BIG] * HPG_A, axis=1)
    for j in range(ns):
        selrow_ref[j] = selterm[j:j + 1, :]

    back = WINDOW // tq

    def slc_tab(kc):
        return jnp.where(kc == qt, 1, jnp.where(kc == qt - 1, 0, 3))

    def win_tab(kc):
        return jnp.where(kc == qt - back, 2, slc_tab(kc))

    _attn_sweep_t(ksb_ref, q_t, vst_ref, btab_ref, slc_tab, 0, qt, s_ref, p_ref, m_ref, l_ref, acc_ref, selrow_ref)
    os_ref[...] = acc_ref[...] / l_ref[...]
    _attn_sweep_t(kwb_ref, q_t, vwt_ref, btab_ref, win_tab, jnp.maximum(qt - back, 0), qt, s_ref, p_ref, m_ref,
                  l_ref, acc_ref)
    ow = acc_ref[...] / l_ref[...]

    gates = jax.nn.sigmoid(gn_ref[...])
    gates_t = gates.T
    for h in range(HPG_A):
        lo, hi = 3 * h, 3 * (HPG_A + h)
        g0, g1, g2 = (jnp.where(g == 0, gates_t[lo + b:lo + b + 1, :], gates_t[hi + b:hi + b + 1, :])
                      for b in range(3))
        cols = slice(h * tq, (h + 1) * tq)
        mix = g0 * oc_ref[:, cols] + g1 * os_ref[:, cols] + g2 * ow[:, cols]
        o_ref[:, h * HD_A:(h + 1) * HD_A] = mix.T.astype(BF16)


def prompt_attention(proj, kc, vc, rel_bias, nseq, seq):
    r = nseq * seq
    nqt = seq // PTQ
    nb = kc.shape[2]
    ns = -(-seq // SLC_BLK)
    nsp = 128
    assert ns <= nsp and nb == 128 and seq % PTQ == 0
    pool = np.zeros((nsp, nb), np.float32)
    pool[np.arange(nb) // (SLC_BLK // CMP_BLK), np.arange(nb)] = 1.0
    kernel = functools.partial(_prompt_attn_kernel, seq, ns)
    kvspec = pl.BlockSpec((1, 1, nb, HD_A), lambda n, g, t: (n, g, 0, 0))

    def kv(col):
        return pl.BlockSpec((seq, HD_A), lambda n, g, t: (n, col // HD_A + g))

    qspec = pl.BlockSpec((PTQ, HPG_A * HD_A), lambda n, g, t: (n * nqt + t, g))
    rows8 = HPG_A * PTQ
    return pl.pallas_call(
        kernel,
        grid=(nseq, KV_A, nqt),
        in_specs=[pl.BlockSpec(memory_space=pltpu.SMEM),
                  qspec,
                  kvspec, kvspec,
                  pl.BlockSpec((nsp, nb), lambda n, g, t: (0, 0)),
                  kv(C_KVS), kv(C_KVS + KV_A * HD_A), kv(C_KVW), kv(C_KVW + KV_A * HD_A),
                  pl.BlockSpec((PTQ, 128), lambda n, g, t: (n * nqt + t, C_GN // 128))],
        out_specs=qspec,
        out_shape=jax.ShapeDtypeStruct((r, H_A * HD_A), BF16),
        scratch_shapes=[pltpu.VMEM((seq, HD_A), BF16), pltpu.VMEM((nqt, HD_A, PTQ), BF16),
                        pltpu.VMEM((seq, HD_A), BF16), pltpu.VMEM((nqt, HD_A, PTQ), BF16),
                        pltpu.VMEM((4, PTQ, rows8), F32), pltpu.VMEM((HPG_A, PTQ, nb), F32),
                        pltpu.VMEM((2, PTQ, rows8), F32), pltpu.VMEM((PTQ, rows8), BF16),
                        pltpu.VMEM((1, rows8), F32), pltpu.VMEM((1, rows8), F32), pltpu.VMEM((HD_A, rows8), F32),
                        pltpu.VMEM((HD_A, rows8), F32), pltpu.VMEM((HD_A, rows8), F32),
                        pltpu.VMEM((ns, 1, rows8), F32)],
        compiler_params=_cp("arbitrary", "arbitrary", "arbitrary"),
        name="prompt_attention",
    )(rel_bias, proj, kc, vc, jnp.asarray(pool), proj, proj, proj, proj, proj)


SUB = 512


def _softmax_step_t(s, vt, m_ref, l_ref, acc_ref):
    m_prev = m_ref[...]
    m_new = jnp.maximum(m_prev, jnp.max(s, axis=0, keepdims=True))
    alpha = jnp.exp(m_prev - m_new)
    p = jnp.exp(s - m_new)
    l_ref[...] = alpha * l_ref[...] + jnp.sum(p, axis=0, keepdims=True)
    acc_ref[...] = alpha * acc_ref[...] + jnp.dot(vt, p.astype(BF16), preferred_element_type=F32)
    m_ref[...] = m_new


def _sample_attn_kernel(past, layer, pt_ref, rb_ref, q_ref, sel_ref, cache_ref, kn_ref, vn_ref, kwc_ref, vwc_ref,
                        kwn_ref, vwn_ref, oc_ref, gn_ref, o_ref, buf_ref, sem, selt_ref, m_ref, l_ref, acc_ref):
    kc = pl.program_id(1)
    nkc = pl.num_programs(1)
    tq = q_ref.shape[0]
    rows8 = HPG_A * tq
    chunk = CHUNK_PAGES * PAGE_SIZE
    ns_last = past // SLC_BLK

    def copies_for(n, k, slot):
        return [pltpu.make_async_copy(cache_ref.at[layer, pt_ref[n, k * CHUNK_PAGES + p]],
                                      buf_ref.at[slot, :, :, pl.ds(p * PAGE_SIZE, PAGE_SIZE), :], sem.at[slot])
                for p in range(CHUNK_PAGES)]

    slot = _paged_chunk_pipeline(pt_ref, copies_for, nkc)
    laneh = lax.shift_right_logical(lax.broadcasted_iota(jnp.int32, (1, rows8), 1), int(math.log2(tq)))
    gates = jax.nn.sigmoid(gn_ref[...])
    per_sub = SUB // SLC_BLK

    for g in range(KV_A):
        stats = (m_ref.at[g], l_ref.at[g], acc_ref.at[g])
        q_rows = jnp.concatenate([q_ref[:, (g * HPG_A + h) * HD_A:(g * HPG_A + h + 1) * HD_A] for h in range(HPG_A)],
                                 axis=0)
        q_t = q_rows.T.astype(BF16)
        rb_rows = []
        for k in range(N_BUCKETS):
            row = jnp.zeros((1, rows8), F32)
            for h in range(HPG_A):
                row = jnp.where(laneh == h, rb_ref[k, g * HPG_A + h], row)
            rb_rows.append(row)
        far = rb_rows[N_BUCKETS - 1]

        def lane_bias(dist):
            out = jnp.broadcast_to(rb_rows[0], dist.shape)
            for k, thr in enumerate(_T5_THR):
                out = jnp.where(dist >= thr, rb_rows[k + 1], out)
            return out

        @pl.when(kc == 0)
        def _():
            _softmax_init(*stats)
            selterm = (sel_ref[g].astype(F32) - 1.0) * MASK_BIG
            selt_ref[g] = jnp.concatenate([selterm] * HPG_A, axis=0).T

        for sub in range(chunk // SUB):
            k = buf_ref[slot, 0, g, sub * SUB:(sub + 1) * SUB, :]
            vt = buf_ref[slot, 1, g, sub * SUB:(sub + 1) * SUB, :].astype(F32).T.astype(BF16)
            j0 = pl.multiple_of(kc * (chunk // SLC_BLK) + sub * per_sub, per_sub)
            blkrows = selt_ref[g, pl.ds(j0, per_sub), :]
            mask = jnp.concatenate([jnp.broadcast_to(blkrows[j:j + 1], (SLC_BLK, rows8)) for j in range(per_sub)],
                                   axis=0)
            s = jnp.dot(k, q_t, preferred_element_type=F32) * SCALE + mask
            if sub < chunk // SUB - 1:
                _softmax_step_t(s + far, vt, *stats)
            else:
                @pl.when(kc < nkc - 1)
                def _():
                    _softmax_step_t(s + far, vt, *stats)

                @pl.when(kc == nkc - 1)
                def _():
                    kj = lax.broadcasted_iota(jnp.int32, (SUB, rows8), 0)
                    qi = lax.broadcasted_iota(jnp.int32, (SUB, rows8), 1) & (tq - 1)
                    _softmax_step_t(s + lane_bias(qi + SUB - kj), vt, *stats)

        @pl.when(kc == nkc - 1)
        def _():
            m_g, l_g, acc_g = stats
            gk = slice(g * HD_A, (g + 1) * HD_A)
            kj = lax.broadcasted_iota(jnp.int32, (128, rows8), 0)
            qi = lax.broadcasted_iota(jnp.int32, (128, rows8), 1) & (tq - 1)
            pad = jnp.zeros((128 - tq, HD_A), F32)
            b_new = jnp.where(kj <= qi, lane_bias(qi - kj), NEG_INF)
            k_new = jnp.concatenate([kn_ref[:, gk], pad], axis=0).astype(BF16)
            v_new_t = jnp.concatenate([vn_ref[:, gk], pad], axis=0).T.astype(BF16)
            s = jnp.dot(k_new, q_t, preferred_element_type=F32) * SCALE + b_new + selt_ref[g, ns_last:ns_last + 1, :]
            _softmax_step_t(s, v_new_t, *stats)
            o_s = (acc_g[...] / l_g[...]).T
            _softmax_init(*stats)
            wrows = kwc_ref.shape[2]
            kj_w = lax.broadcasted_iota(jnp.int32, (wrows, rows8), 0)
            qi_w = lax.broadcasted_iota(jnp.int32, (wrows, rows8), 1) & (tq - 1)
            dw = qi_w + wrows - kj_w
            s = jnp.dot(kwc_ref[0, 0, :, gk].astype(BF16), q_t, preferred_element_type=F32) * SCALE
            s = s + jnp.where(dw < WINDOW, lane_bias(dw), NEG_INF)
            _softmax_step_t(s, vwc_ref[0, 0, :, gk].T.astype(BF16), *stats)
            k_new = jnp.concatenate([kwn_ref[:, gk], pad], axis=0).astype(BF16)
            v_new_t = jnp.concatenate([vwn_ref[:, gk], pad], axis=0).T.astype(BF16)
            _softmax_step_t(jnp.dot(k_new, q_t, preferred_element_type=F32) * SCALE + b_new, v_new_t, *stats)
            o_w = (acc_g[...] / l_g[...]).T
            for h in range(HPG_A):
                lane = 3 * (g * HPG_A + h)
                rows = slice(h * tq, (h + 1) * tq)
                hs = slice((g * HPG_A + h) * HD_A, (g * HPG_A + h + 1) * HD_A)
                o_ref[:, hs] = (gates[:, lane:lane + 1] * oc_ref[:, hs] + gates[:, lane + 1:lane + 2] * o_s[rows, :]
                                + gates[:, lane + 2:lane + 3] * o_w[rows, :]).astype(BF16)


def sample_attention(proj, o_cmp, sel, page_table, cache_t, layer, cache_win, rel_bias, nseq, tq, past):
    r = nseq * tq
    nsp = sel.shape[2]
    chunk = CHUNK_PAGES * PAGE_SIZE
    nkc = past // chunk
    kernel = functools.partial(_sample_attn_kernel, past, layer)
    wrows = cache_win.shape[2]
    assert wrows == WINDOW
    gw = KV_A * HD_A

    def new_rows(col):
        return pl.BlockSpec((tq, gw), lambda n, k, pt: (n, col // gw))

    def win_rows(c):
        return pl.BlockSpec((1, 1, wrows, gw), lambda n, k, pt: (layer, n, 0, c))

    qspec = pl.BlockSpec((tq, H_A * HD_A), lambda n, k, pt: (n, 0))
    rows8 = HPG_A * tq
    return pl.pallas_call(
        kernel,
        grid_spec=pltpu.PrefetchScalarGridSpec(
            num_scalar_prefetch=1, grid=(nseq, nkc),
            in_specs=[pl.BlockSpec(memory_space=pltpu.SMEM),
                      qspec,
                      pl.BlockSpec((KV_A, tq, nsp), lambda n, k, pt: (0, n, 0)),
                      pl.BlockSpec(memory_space=pl.ANY),
                      new_rows(C_KVS), new_rows(C_KVS + gw),
                      win_rows(0), win_rows(1),
                      new_rows(C_KVW), new_rows(C_KVW + gw),
                      qspec,
                      pl.BlockSpec((tq, 128), lambda n, k, pt: (n, C_GN // 128))],
            out_specs=qspec,
            scratch_shapes=[pltpu.VMEM((2, 2, KV_A, chunk, HD_A), BF16),
                            pltpu.SemaphoreType.DMA((2,)), pltpu.VMEM((KV_A, nsp, rows8), F32),
                            pltpu.VMEM((KV_A, 1, rows8), F32), pltpu.VMEM((KV_A, 1, rows8), F32),
                            pltpu.VMEM((KV_A, HD_A, rows8), F32)]),
        out_shape=jax.ShapeDtypeStruct((r, H_A * HD_A), BF16),
        compiler_params=_cp("arbitrary", "arbitrary"),
        name="sample_attention",
    )(page_table, rel_bias, proj, sel, cache_t, proj, proj, cache_win, cache_win, proj, proj, o_cmp, proj)


def _prep_w_in(w):
    src = 0
    pieces = []
    for width, dst in _SEGS:
        pieces.append((dst, w[:, src:src + width]))
        src += width
    pieces.sort(key=lambda p: p[0])
    cols, pos = [], 0
    for dst, piece in pieces:
        assert dst >= pos
        if dst > pos:
            cols.append(jnp.zeros((w.shape[0], dst - pos), w.dtype))
        cols.append(piece)
        pos = dst + piece.shape[1]
    if pos < IN_PAD:
        cols.append(jnp.zeros((w.shape[0], IN_PAD - pos), w.dtype))
    return jnp.concatenate(cols, axis=1).astype(BF16)


def _state_rows(state, nseq, width):
    k = state.shape[1]
    return jnp.concatenate([jnp.zeros((nseq, 8 - k, width), F32), state.astype(F32)], axis=1)


def _trunk_layer(x, p, rel_bias, nseq, seq_rows, valid, tiles, attn_fn, conv_b_prev, conv_f_prev, h0t):
    tm, tm_merge, tm_ffn = tiles
    proj = rms_matmul(x, p['norm1'], p['w_in'], tm, 1280)
    o_a = attn_fn(proj)
    o_b, h_t = ssd_mixer(proj, conv_b_prev, p['conv_b_w'], p['conv_b_bias'], p['dt_bias'], p['a_log'], p['d_skip'],
                         p['ssm_norm'], h0t, nseq, seq_rows, valid)
    o_c, v_rows = chunk_mlp_mixer(proj, p['v_ln_g'], p['v_ln_b'], p['w_spatial'], p['b_spatial'], seq_rows)
    merged = gated_merge(o_a, o_b, o_c, p['w_br_a'], p['w_br_b'], p['w_br_c'], proj, tm_merge, 512)
    x = matmul_residual(merged, p['w_out'], x, tm, 1024)
    h = rms_matmul(x, p['norm2'], p['w_ff_in'], tm, 1024)
    x = ffn_out(h, conv_f_prev, p['ff_conv_w'], p['ff_conv_b'], p['w_ff_out'], x, seq_rows, tm_ffn, 512)
    return x, proj, h, h_t, v_rows


def kernel(x_prompt, x_sample, cache_cmp_kv, cache_slc_kv, cache_win_kv, state_ssm, state_conv_ssm, state_conv_ffn,
           page_table, rel_bias, norm1, w_in, cmp_pe, cmp_w1, cmp_w2, conv_b_w, conv_b_bias, dt_bias, a_log, d_skip,
           ssm_norm, v_ln_g, v_ln_b, w_spatial, b_spatial, w_br_a, w_br_b, w_br_c, w_out, norm2, w_ff_in,
           ff_conv_w, ff_conv_b, w_ff_out, final_norm):
    bp, seq = x_prompt.shape[:2]
    bs, dec = x_sample.shape[:2]
    depth = w_in.shape[0]
    n_pool = cache_cmp_kv.shape[1]
    past = page_table.shape[1] * PAGE_SIZE
    st = SAMPLE_T
    kv_shape = (2, KV_A, HD_A)

    page_table = page_table.astype(jnp.int32)
    cmp_t = cache_cmp_kv.reshape(depth, n_pool, BLK_PER_PAGE, CMP_BLK, 2, KV_A, HD_A).transpose(0, 1, 4, 3, 5, 2, 6)
    cmp_t = cmp_t.reshape(depth, n_pool, 2, CMP_BLK, KV_A * BLK_PER_PAGE, HD_A)
    slc_t = cache_slc_kv.transpose(0, 1, 3, 4, 2, 5).astype(BF16)
    cache_win = cache_win_kv.reshape(depth, bs, WINDOW, ROW_W)

    xp = x_prompt.reshape(bp * seq, D_MODEL)
    xs = jnp.pad(x_sample, ((0, 0), (0, st - dec), (0, 0))).reshape(bs * st, D_MODEL)
    zeros_b = jnp.zeros((bp, 8, XBC_DIM), F32)
    zeros_f = jnp.zeros((bp, 8, D_FF), F32)
    zeros_h = jnp.zeros((bp, N_B, D_INNER_B), F32)
    ns_s = -(-(past + dec) // SLC_BLK)

    outs = [[] for _ in range(13)]
    for l in range(depth):
        p = {'norm1': norm1[l], 'w_in': _prep_w_in(w_in[l]), 'conv_b_w': jnp.pad(conv_b_w[l], ((0, 8 - CONV_B), (0, 0))),
             'conv_b_bias': conv_b_bias[l], 'dt_bias': dt_bias[l], 'a_log': a_log[l], 'd_skip': d_skip[l],
             'ssm_norm': ssm_norm[l], 'v_ln_g': v_ln_g[l], 'v_ln_b': v_ln_b[l], 'w_spatial': w_spatial[l],
             'b_spatial': b_spatial[l], 'w_br_a': w_br_a[l].astype(BF16), 'w_br_b': w_br_b[l].astype(BF16),
             'w_br_c': w_br_c[l].astype(BF16), 'w_out': w_out[l].astype(BF16), 'norm2': norm2[l],
             'w_ff_in': w_ff_in[l].astype(BF16), 'ff_conv_w': jnp.pad(ff_conv_w[l], ((0, 8 - CONV_F), (0, 0))),
             'ff_conv_b': ff_conv_b[l], 'w_ff_out': w_ff_out[l].astype(BF16)}
        pe, w1, w2 = cmp_pe[l], cmp_w1[l].astype(BF16), cmp_w2[l].astype(BF16)

        def attn_prompt(proj):
            kc, vc = compress_kv(proj.reshape(1, bp * seq, IN_PAD), 0, C_KVC // ROW_W, bp, seq, pe, w1, w2)
            return prompt_attention(proj, kc, vc, rel_bias, bp, seq)

        def attn_sample(proj):
            kc, vc = compress_kv_paged(page_table, cmp_t, l, pe, w1, w2)
            o_cmp, sel = compressed_attention(proj, kc, vc, rel_bias, bs, st, st, past, ns_s)
            return sample_attention(proj, o_cmp, sel, page_table, slc_t, l, cache_win, rel_bias, bs, st, past)

        xp, proj, h, h_t, _ = _trunk_layer(xp, p, rel_bias, bp, seq, seq, (1024, 512, 1024), attn_prompt,
                                           zeros_b, zeros_f, zeros_h)
        pr = proj.reshape(bp, seq, IN_PAD)
        outs[0].append(pr[:, :, C_KVC:C_KVC + ROW_W].reshape((bp, seq) + kv_shape))
        outs[1].append(pr[:, :, C_KVS:C_KVS + ROW_W].reshape((bp, seq) + kv_shape))
        outs[2].append(pr[:, seq - WINDOW:, C_KVW:C_KVW + ROW_W].reshape((bp, WINDOW) + kv_shape))
        outs[3].append(h_t.reshape(bp, N_B, H_B, HD_B).transpose(0, 2, 3, 1))
        outs[4].append(pr[:, seq - (CONV_B - 1):, C_XBC:C_XBC + XBC_DIM])
        outs[5].append(h.reshape(bp, seq, 2 * D_FF)[:, seq - (CONV_F - 1):, :D_FF])

        h0t = state_ssm[l].astype(F32).transpose(0, 3, 1, 2).reshape(bs, N_B, D_INNER_B)
        xs, proj, h, h_t, v_rows = _trunk_layer(xs, p, rel_bias, bs, st, dec, (bs * st, bs * st, st), attn_sample,
                                                _state_rows(state_conv_ssm[l], bs, XBC_DIM),
                                                _state_rows(state_conv_ffn[l], bs, D_FF), h0t)
        pr = proj.reshape(bs, st, IN_PAD)
        kvw_new = pr[:, :dec, C_KVW:C_KVW + ROW_W].reshape((bs, dec) + kv_shape)
        outs[6].append(pr[:, :dec, C_KVC:C_KVC + ROW_W].reshape((bs, dec) + kv_shape))
        outs[7].append(pr[:, :dec, C_KVS:C_KVS + ROW_W].reshape((bs, dec) + kv_shape))
        outs[8].append(jnp.concatenate([cache_win_kv[l][:, dec:], kvw_new], axis=1))
        outs[9].append(h_t.reshape(bs, N_B, H_B, HD_B).transpose(0, 2, 3, 1))
        xbc_all = jnp.concatenate([state_conv_ssm[l], pr[:, :dec, C_XBC:C_XBC + XBC_DIM]], axis=1)
        outs[10].append(xbc_all[:, dec:])
        a_all = jnp.concatenate([state_conv_ffn[l], h.reshape(bs, st, 2 * D_FF)[:, :dec, :D_FF]], axis=1)
        outs[11].append(a_all[:, dec:])
        outs[12].append(v_rows.reshape(bs, st, D_C)[:, :dec])

    y_prompt = rms_norm_rows(xp, final_norm, 1024).reshape(bp, seq, D_MODEL)
    y_sample = rms_norm_rows(xs, final_norm, bs * st).reshape(bs, st, D_MODEL)[:, :dec]
    return (y_prompt, y_sample) + tuple(jnp.stack(o) for o in outs)
```

```python
import functools
import math

import numpy as np
import jax
import jax.numpy as jnp
from jax import lax
from jax.experimental import pallas as pl
from jax.experimental.pallas import tpu as pltpu

F32 = jnp.float32
BF16 = jnp.bfloat16

D_MODEL = 2048
DEPTH = 4
PAGE_SIZE = 128
H_A = 16
HD_A = 128
KV_A = 2
HPG_A = H_A // KV_A
CMP_BLK = 32
SLC_BLK = 64
N_SEL = 16
WINDOW = 512
FORCE_BONUS = 1.0e4
NEG_INF = -1.0e30
N_BUCKETS = 32
MAX_DIST = 128
D_INNER_B = 2048
HD_B = 64
H_B = D_INNER_B // HD_B
G_B = 4
N_B = 128
CONV_B = 4
SSD_CHUNK = 128
XBC_DIM = D_INNER_B + 2 * G_B * N_B
D_C = 2048
G_C = 8
CHUNK_C = 128
D_FF = 5632
CONV_F = 3
NORM_EPS = 1e-6

C_Q = 0
C_Z = 2048
C_U = 4096
C_V = 6144
C_GM = 8192
C_XBC = 14336
C_KVC = 17408
C_KVS = 17920
C_KVW = 18432
C_GN = 18944
C_DT = 19072
IN_PAD = 19200
_SEGS = ((H_A * HD_A, C_Q), (512, C_KVC), (512, C_KVS), (512, C_KVW), (3 * H_A, C_GN), (D_INNER_B, C_Z),
         (XBC_DIM, C_XBC), (H_B, C_DT), (D_C, C_U), (D_C, C_V), (3 * D_MODEL, C_GM))

VMEM_LIMIT = 56 * 1024 * 1024
SAMPLE_T = 16


def _cp(*sem):
    return pltpu.CompilerParams(dimension_semantics=sem, vmem_limit_bytes=VMEM_LIMIT)


def _gelu(x):
    return jax.nn.gelu(x)


def _t5_thresholds():
    n = np.arange(0, 4 * MAX_DIST)
    max_exact = N_BUCKETS // 2
    nf = np.maximum(n, 1).astype(np.float64)
    large = max_exact + (np.log(nf / max_exact) / math.log(MAX_DIST / max_exact) * (N_BUCKETS - max_exact)).astype(np.int64)
    b = np.where(n < max_exact, n, np.minimum(large, N_BUCKETS - 1))
    return [int(np.argmax(b >= k)) for k in range(1, N_BUCKETS)]


_T5_THR = _t5_thresholds()


def _t5_bias(dist, rb_ref, head):
    out = jnp.full(dist.shape, rb_ref[0, head], F32)
    for k, thr in enumerate(_T5_THR):
        out = jnp.where(dist >= thr, rb_ref[k + 1, head], out)
    return out


def _rmsmm_kernel(x_ref, g_ref, w_ref, o_ref, xn_ref):
    @pl.when(pl.program_id(1) == 0)
    def _():
        x = x_ref[...]
        y = x * lax.rsqrt(jnp.mean(x * x, axis=-1, keepdims=True) + NORM_EPS)
        xn_ref[...] = (y * g_ref[...]).astype(BF16)

    o_ref[...] = jnp.dot(xn_ref[...], w_ref[...], preferred_element_type=F32)


def rms_matmul(x, g, w, tm, tn):
    r, d = x.shape
    n = w.shape[1]
    return pl.pallas_call(
        _rmsmm_kernel,
        grid=(r // tm, n // tn),
        in_specs=[pl.BlockSpec((tm, d), lambda i, j: (i, 0)),
                  pl.BlockSpec((1, d), lambda i, j: (0, 0)),
                  pl.BlockSpec((d, tn), lambda i, j: (0, j))],
        out_specs=pl.BlockSpec((tm, tn), lambda i, j: (i, j)),
        out_shape=jax.ShapeDtypeStruct((r, n), F32),
        scratch_shapes=[pltpu.VMEM((tm, d), BF16)],
        compiler_params=_cp("arbitrary", "arbitrary"),
        name="rms_matmul",
    )(x, g.reshape(1, d), w)


def _merge_kernel(oa_ref, ob_ref, oc_ref, wa_ref, wb_ref, wc_ref, ga_ref, gb_ref, gc_ref, o_ref):
    acc = jax.nn.sigmoid(ga_ref[...]) * jnp.dot(oa_ref[...], wa_ref[...], preferred_element_type=F32)
    acc = acc + jax.nn.sigmoid(gb_ref[...]) * jnp.dot(ob_ref[...], wb_ref[...], preferred_element_type=F32)
    acc = acc + jax.nn.sigmoid(gc_ref[...]) * jnp.dot(oc_ref[...], wc_ref[...], preferred_element_type=F32)
    o_ref[...] = acc.astype(BF16)


def gated_merge(o_a, o_b, o_c, w_a, w_b, w_c, proj, tm, tn):
    r, d = o_a.shape
    n = w_a.shape[1]
    gblk = C_GM // tn
    per = D_MODEL // tn
    lhs = pl.BlockSpec((tm, d), lambda i, j: (i, 0))
    wsp = pl.BlockSpec((d, tn), lambda i, j: (0, j))

    def gate(b):
        return pl.BlockSpec((tm, tn), lambda i, j: (i, gblk + b * per + j))

    return pl.pallas_call(
        _merge_kernel,
        grid=(r // tm, n // tn),
        in_specs=[lhs, lhs, lhs, wsp, wsp, wsp, gate(0), gate(1), gate(2)],
        out_specs=pl.BlockSpec((tm, tn), lambda i, j: (i, j)),
        out_shape=jax.ShapeDtypeStruct((r, n), BF16),
        compiler_params=_cp("arbitrary", "arbitrary"),
        name="gated_merge",
    )(o_a, o_b, o_c, w_a, w_b, w_c, proj, proj, proj)


def _mmres_kernel(a_ref, w_ref, r_ref, o_ref):
    o_ref[...] = r_ref[...] + jnp.dot(a_ref[...], w_ref[...], preferred_element_type=F32)


def matmul_residual(a, w, res, tm, tn):
    r, k = a.shape
    n = w.shape[1]
    return pl.pallas_call(
        _mmres_kernel,
        grid=(r // tm, n // tn),
        in_specs=[pl.BlockSpec((tm, k), lambda i, j: (i, 0)),
                  pl.BlockSpec((k, tn), lambda i, j: (0, j)),
                  pl.BlockSpec((tm, tn), lambda i, j: (i, j))],
        out_specs=pl.BlockSpec((tm, tn), lambda i, j: (i, j)),
        out_shape=jax.ShapeDtypeStruct((r, n), F32),
        compiler_params=_cp("arbitrary", "arbitrary"),
        name="matmul_residual",
    )(a, w, res)


def _ffn_out_kernel(tiles_per_seq, a_ref, b_ref, prev_ref, cw_ref, cb_ref, w_ref, r_ref, o_ref,
                    ext_ref, carry_ref):
    i = pl.program_id(0)
    k = pl.program_id(1)
    tm = a_ref.shape[0]

    @pl.when(i % tiles_per_seq == 0)
    def _():
        ext_ref[0:8, :] = prev_ref[0]

    @pl.when(i % tiles_per_seq != 0)
    def _():
        ext_ref[0:8, :] = carry_ref[k]

    a = a_ref[...]
    ext_ref[8:8 + tm, :] = a
    carry_ref[k] = a_ref[tm - 8:tm, :]
    cw = cw_ref[...]
    y = cb_ref[...] + ext_ref[6:6 + tm, :] * cw[0:1]
    y = y + ext_ref[7:7 + tm, :] * cw[1:2]
    y = y + a * cw[2:3]
    h = (_gelu(y) * b_ref[...]).astype(BF16)
    part = jnp.dot(h, w_ref[...], preferred_element_type=F32)

    @pl.when(k == 0)
    def _():
        o_ref[...] = r_ref[...] + part

    @pl.when(k != 0)
    def _():
        o_ref[...] += part


def _ffn_out_short_kernel(a_ref, b_ref, prev_ref, cw_ref, cb_ref, w_ref, r_ref, o_ref, ext_ref):
    k = pl.program_id(0)
    i = pl.program_id(1)
    tm = a_ref.shape[0]
    a = a_ref[...]
    ext_ref[0:8, :] = prev_ref[0]
    ext_ref[8:8 + tm, :] = a
    cw = cw_ref[...]
    y = cb_ref[...] + ext_ref[6:6 + tm, :] * cw[0:1]
    y = y + ext_ref[7:7 + tm, :] * cw[1:2]
    y = y + a * cw[2:3]
    h = (_gelu(y) * b_ref[...]).astype(BF16)
    part = jnp.dot(h, w_ref[...], preferred_element_type=F32)
    rows = pl.ds(pl.multiple_of(i * tm, tm), tm)

    @pl.when(k == 0)
    def _():
        o_ref[rows, :] = r_ref[rows, :] + part

    @pl.when(k != 0)
    def _():
        o_ref[rows, :] += part


def ffn_out(h, prev, conv_w, conv_b, w, res, seq_len, tm, tk):
    r = h.shape[0]
    n = w.shape[1]
    nk = D_FF // tk
    if seq_len == tm and r > tm:
        return pl.pallas_call(
            _ffn_out_short_kernel,
            grid=(nk, r // tm),
            in_specs=[pl.BlockSpec((tm, tk), lambda k, i: (i, k)),
                      pl.BlockSpec((tm, tk), lambda k, i: (i, nk + k)),
                      pl.BlockSpec((1, 8, tk), lambda k, i: (i, 0, k)),
                      pl.BlockSpec((8, tk), lambda k, i: (0, k)),
                      pl.BlockSpec((1, tk), lambda k, i: (0, k)),
                      pl.BlockSpec((tk, n), lambda k, i: (k, 0)),
                      pl.BlockSpec((r, n), lambda k, i: (0, 0))],
            out_specs=pl.BlockSpec((r, n), lambda k, i: (0, 0)),
            out_shape=jax.ShapeDtypeStruct((r, n), F32),
            scratch_shapes=[pltpu.VMEM((tm + 8, tk), F32)],
            compiler_params=_cp("arbitrary", "arbitrary"),
            name="ffn_out_short",
        )(h, h, prev, conv_w, conv_b.reshape(1, D_FF), w, res)
    kernel = functools.partial(_ffn_out_kernel, seq_len // tm)
    return pl.pallas_call(
        kernel,
        grid=(r // tm, nk),
        in_specs=[pl.BlockSpec((tm, tk), lambda i, k: (i, k)),
                  pl.BlockSpec((tm, tk), lambda i, k: (i, nk + k)),
                  pl.BlockSpec((1, 8, tk), lambda i, k: (i * tm // seq_len, 0, k)),
                  pl.BlockSpec((8, tk), lambda i, k: (0, k)),
                  pl.BlockSpec((1, tk), lambda i, k: (0, k)),
                  pl.BlockSpec((tk, n), lambda i, k: (k, 0)),
                  pl.BlockSpec((tm, n), lambda i, k: (i, 0))],
        out_specs=pl.BlockSpec((tm, n), lambda i, k: (i, 0)),
        out_shape=jax.ShapeDtypeStruct((r, n), F32),
        scratch_shapes=[pltpu.VMEM((tm + 8, tk), F32), pltpu.VMEM((nk, 8, tk), F32)],
        compiler_params=_cp("arbitrary", "arbitrary"),
        name="ffn_out",
    )(h, h, prev, conv_w, conv_b.reshape(1, D_FF), w, res)


def _rms_kernel(x_ref, g_ref, o_ref):
    x = x_ref[...]
    o_ref[...] = x * lax.rsqrt(jnp.mean(x * x, axis=-1, keepdims=True) + NORM_EPS) * g_ref[...]


def rms_norm_rows(x, g, tm):
    r, d = x.shape
    return pl.pallas_call(
        _rms_kernel,
        grid=(r // tm,),
        in_specs=[pl.BlockSpec((tm, d), lambda i: (i, 0)), pl.BlockSpec((1, d), lambda i: (0, 0))],
        out_specs=pl.BlockSpec((tm, d), lambda i: (i, 0)),
        out_shape=jax.ShapeDtypeStruct((r, d), F32),
        compiler_params=_cp("arbitrary"),
        name="final_rms_norm",
    )(x, g.reshape(1, d))


def _ssd_kernel(rows_in, valid_len, x_ref, b_ref, c_ref, z_ref, dt_ref, prev_ref, cw_ref, cb_ref, dtb_ref, alog_ref,
                dskip_ref, norm_ref, h0_ref, tril_ref, o_ref, ht_ref, ext_ref, st_ref, tt_ref, y_ref, xw_ref):
    L = SSD_CHUNK
    c = pl.program_id(1)

    @pl.when(c == 0)
    def _():
        ext_ref[0:8, :] = prev_ref[0]
        ht_ref[...] = h0_ref[...]

    @pl.when(c != 0)
    def _():
        ext_ref[0:8, :] = ext_ref[L:L + 8, :]

    if rows_in < L:
        ext_ref[8 + rows_in:8 + L, :] = jnp.zeros((L - rows_in, XBC_DIM), F32)
    ext_ref[8:8 + rows_in, 0:D_INNER_B] = x_ref[...]
    ext_ref[8:8 + rows_in, D_INNER_B:D_INNER_B + G_B * N_B] = b_ref[...]
    ext_ref[8:8 + rows_in, D_INNER_B + G_B * N_B:XBC_DIM] = c_ref[...]

    cw = cw_ref[...]
    y = cb_ref[...] + ext_ref[5:5 + L, :] * cw[0:1]
    y = y + ext_ref[6:6 + L, :] * cw[1:2]
    y = y + ext_ref[7:7 + L, :] * cw[2:3]
    y = y + ext_ref[8:8 + L, :] * cw[3:4]
    xbc = y * jax.nn.sigmoid(y)

    if rows_in < L:
        dt_raw = jnp.concatenate([dt_ref[...], jnp.zeros((L - rows_in, 128), F32)], axis=0)
    else:
        dt_raw = dt_ref[...]
    d_all = jax.nn.softplus(dt_raw + dtb_ref[...])
    if valid_len < L:
        row = lax.broadcasted_iota(jnp.int32, (L, 128), 0)
        d_all = jnp.where(row < valid_len, d_all, 0.0)
    a_row = -jnp.exp(alog_ref[...])
    tril = tril_ref[...]
    cs = jnp.dot(tril, d_all * a_row, preferred_element_type=F32, precision=lax.Precision.HIGHEST)
    st_ref[0] = cs
    st_ref[1] = jnp.exp(cs)
    st_ref[2] = jnp.exp(cs[L - 1:L, :] - cs) * d_all
    tt_ref[0] = cs.T
    tt_ref[1] = d_all.T
    lower = tril > 0.5

    for g in range(G_B):
        b_g = xbc[:, D_INNER_B + g * N_B:D_INNER_B + (g + 1) * N_B]
        c_g = xbc[:, D_INNER_B + G_B * N_B + g * N_B:D_INNER_B + G_B * N_B + (g + 1) * N_B]
        cgb = c_g.astype(BF16)
        cb = lax.dot_general(cgb, b_g.astype(BF16), (((1,), (1,)), ((), ())), preferred_element_type=F32)
        bt = b_g.T.astype(BF16)
        gs = slice(g * 512, (g + 1) * 512)
        hg = ht_ref[0, :, gs]
        yoff = jnp.dot(cgb, hg.astype(BF16), preferred_element_type=F32)
        for hh in range(8):
            h = g * 8 + hh
            hs = slice(h * HD_B, (h + 1) * HD_B)
            x_h = xbc[:, hs]
            seg = st_ref[0, :, h:h + 1] - tt_ref[0, h:h + 1, :]
            decay = jnp.exp(jnp.where(lower, seg, NEG_INF))
            m = (cb * decay * tt_ref[1, h:h + 1, :]).astype(BF16)
            y_h = jnp.dot(m, x_h.astype(BF16), preferred_element_type=F32)
            y_h = y_h + yoff[:, hh * HD_B:(hh + 1) * HD_B] * st_ref[1, :, h:h + 1]
            y_ref[:, hs] = y_h + dskip_ref[:, hs] * x_h
            xw_ref[:, hh * HD_B:(hh + 1) * HD_B] = (x_h * st_ref[2, :, h:h + 1]).astype(BF16)
        upd = jnp.dot(bt, xw_ref[...], preferred_element_type=F32)
        for hh in range(8):
            h = g * 8 + hh
            hs = slice(h * HD_B, (h + 1) * HD_B)
            ht_ref[0, :, hs] = ht_ref[0, :, hs] * st_ref[1, L - 1:L, h:h + 1] + upd[:, hh * HD_B:(hh + 1) * HD_B]

    if rows_in < L:
        z = jnp.concatenate([z_ref[...], jnp.zeros((L - rows_in, D_INNER_B), F32)], axis=0)
    else:
        z = z_ref[...]
    yz = y_ref[...] * (z * jax.nn.sigmoid(z))
    for g in range(G_B):
        gs = slice(g * 512, (g + 1) * 512)
        v = yz[:, gs]
        out = v * lax.rsqrt(jnp.mean(v * v, axis=-1, keepdims=True) + NORM_EPS) * norm_ref[:, gs]
        o_ref[:, gs] = out[0:rows_in].astype(BF16)


def ssd_mixer(proj, prev, conv_w, conv_b, dt_bias, a_log, d_skip, ssm_norm, h0t, nseq, seq_rows, valid_len):
    rows_in = min(seq_rows, SSD_CHUNK)
    nchunk = seq_rows // rows_in
    r = nseq * seq_rows
    tril = jnp.asarray(np.tril(np.ones((SSD_CHUNK, SSD_CHUNK), np.float32)))
    pad128 = lambda v: jnp.pad(v.astype(F32), (0, 128 - v.shape[0])).reshape(1, 128)
    kernel = functools.partial(_ssd_kernel, rows_in, min(valid_len, SSD_CHUNK))
    rowblk = lambda n, c: n * nchunk + c
    const = lambda n, c: (0, 0)
    return pl.pallas_call(
        kernel,
        grid=(nseq, nchunk),
        in_specs=[pl.BlockSpec((rows_in, D_INNER_B), lambda n, c: (rowblk(n, c), C_XBC // 2048)),
                  pl.BlockSpec((rows_in, 512), lambda n, c: (rowblk(n, c), (C_XBC + 2048) // 512)),
                  pl.BlockSpec((rows_in, 512), lambda n, c: (rowblk(n, c), (C_XBC + 2560) // 512)),
                  pl.BlockSpec((rows_in, D_INNER_B), lambda n, c: (rowblk(n, c), C_Z // 2048)),
                  pl.BlockSpec((rows_in, 128), lambda n, c: (rowblk(n, c), C_DT // 128)),
                  pl.BlockSpec((1, 8, XBC_DIM), lambda n, c: (n, 0, 0)),
                  pl.BlockSpec((8, XBC_DIM), const),
                  pl.BlockSpec((1, XBC_DIM), const),
                  pl.BlockSpec((1, 128), const),
                  pl.BlockSpec((1, 128), const),
                  pl.BlockSpec((1, D_INNER_B), const),
                  pl.BlockSpec((1, D_INNER_B), const),
                  pl.BlockSpec((1, N_B, D_INNER_B), lambda n, c: (n, 0, 0)),
                  pl.BlockSpec((SSD_CHUNK, SSD_CHUNK), const)],
        out_specs=[pl.BlockSpec((rows_in, D_INNER_B), lambda n, c: (rowblk(n, c), 0)),
                   pl.BlockSpec((1, N_B, D_INNER_B), lambda n, c: (n, 0, 0))],
        out_shape=[jax.ShapeDtypeStruct((r, D_INNER_B), BF16),
                   jax.ShapeDtypeStruct((nseq, N_B, D_INNER_B), F32)],
        scratch_shapes=[pltpu.VMEM((SSD_CHUNK + 8, XBC_DIM), F32),
                        pltpu.VMEM((3, SSD_CHUNK, 128), F32),
                        pltpu.VMEM((2, 128, SSD_CHUNK), F32),
                        pltpu.VMEM((SSD_CHUNK, D_INNER_B), F32),
                        pltpu.VMEM((SSD_CHUNK, 512), BF16)],
        compiler_params=_cp("arbitrary", "arbitrary"),
        name="ssd_mixer",
    )(proj, proj, proj, proj, proj, prev, conv_w, conv_b.reshape(1, XBC_DIM), pad128(dt_bias), pad128(a_log),
      jnp.repeat(d_skip.astype(F32), HD_B).reshape(1, D_INNER_B), ssm_norm.reshape(1, D_INNER_B), h0t, tril)


def _cmlp_kernel(rows_in, u_ref, v_ref, lg_ref, lb_ref, ws_ref, bst_ref, o_ref, vn_ref):
    L = CHUNK_C
    v = _gelu(v_ref[...])
    mu = jnp.mean(v, axis=-1, keepdims=True)
    vc = v - mu
    var = jnp.mean(vc * vc, axis=-1, keepdims=True)
    vn = vc * lax.rsqrt(var + NORM_EPS) * lg_ref[...] + lb_ref[...]
    vn_ref[...] = vn
    u = _gelu(u_ref[...])
    if rows_in < L:
        vn = jnp.concatenate([vn, jnp.zeros((L - rows_in, D_C), F32)], axis=0)
    vnb = vn.astype(BF16)
    row = lax.broadcasted_iota(jnp.int32, (L, L), 0)
    col = lax.broadcasted_iota(jnp.int32, (L, L), 1)
    gw = D_C // G_C
    for g in range(G_C):
        w = jnp.where(col <= row, ws_ref[g], 0.0).astype(BF16)
        s = jnp.dot(w, vnb[:, g * gw:(g + 1) * gw], preferred_element_type=F32) + bst_ref[:, g:g + 1]
        o_ref[:, g * gw:(g + 1) * gw] = (u[:, g * gw:(g + 1) * gw] * s[0:rows_in]).astype(BF16)


def chunk_mlp_mixer(proj, ln_g, ln_b, w_s, b_s, seq_rows):
    r = proj.shape[0]
    rows_in = min(seq_rows, CHUNK_C)
    kernel = functools.partial(_cmlp_kernel, rows_in)
    bst = jnp.pad(b_s.T.astype(F32), ((0, 0), (0, 128 - G_C)))
    return pl.pallas_call(
        kernel,
        grid=(r // rows_in,),
        in_specs=[pl.BlockSpec((rows_in, D_C), lambda i: (i, C_U // 2048)),
                  pl.BlockSpec((rows_in, D_C), lambda i: (i, C_V // 2048)),
                  pl.BlockSpec((1, D_C), lambda i: (0, 0)),
                  pl.BlockSpec((1, D_C), lambda i: (0, 0)),
                  pl.BlockSpec((G_C, CHUNK_C, CHUNK_C), lambda i: (0, 0, 0)),
                  pl.BlockSpec((CHUNK_C, 128), lambda i: (0, 0))],
        out_specs=[pl.BlockSpec((rows_in, D_C), lambda i: (i, 0)),
                   pl.BlockSpec((rows_in, D_C), lambda i: (i, 0))],
        out_shape=[jax.ShapeDtypeStruct((r, D_C), BF16), jax.ShapeDtypeStruct((r, D_C), F32)],
        compiler_params=_cp("arbitrary"),
        name="chunk_mlp_mixer",
    )(proj, proj, ln_g.reshape(1, D_C), ln_b.reshape(1, D_C), w_s, bst)


ROW_W = 2 * KV_A * HD_A
SCALE = HD_A ** -0.5
MASK_BIG = 1.0e30
M_INIT = -2.0e30
EXP2_SCALE = SCALE * math.log2(math.e)


CMP_ROWS = 4096
CHUNK_PAGES = CMP_ROWS // PAGE_SIZE
BLK_PER_PAGE = PAGE_SIZE // CMP_BLK


def _paged_chunk_pipeline(pt_ref, copies_for, n_chunks_per_seq):
    n = pl.program_id(0)
    k = pl.program_id(1)
    step = n * n_chunks_per_seq + k
    total = pl.num_programs(0) * n_chunks_per_seq
    slot = step % 2

    @pl.when(step == 0)
    def _():
        for cp in copies_for(n, k, slot):
            cp.start()

    @pl.when(step + 1 < total)
    def _():
        wrap = k + 1 == n_chunks_per_seq
        for cp in copies_for(jnp.where(wrap, n + 1, n), jnp.where(wrap, 0, k + 1), 1 - slot):
            cp.start()

    for cp in copies_for(n, k, slot):
        cp.wait()
    return slot


def _cmpmlp_paged_kernel(layer, pt_ref, cache_ref, pe_ref, w1_ref, w2_ref, kc_ref, vc_ref, buf_ref, sem):
    nkc = pl.num_programs(1)

    def copies_for(n, k, slot):
        return [pltpu.make_async_copy(cache_ref.at[layer, pt_ref[n, k * CHUNK_PAGES + p]],
                                      buf_ref.at[slot, :, :, pl.ds(p * 8, 8), :], sem.at[slot])
                for p in range(CHUNK_PAGES)]

    slot = _paged_chunk_pipeline(pt_ref, copies_for, nkc)
    for c in range(2):
        xs = [(buf_ref[slot, c, s] + pe_ref[s * 2 + c]).astype(BF16) for s in range(CMP_BLK)]
        acc = jnp.dot(jnp.concatenate(xs, axis=1), w1_ref[c], preferred_element_type=F32)
        out = jnp.dot(_gelu(acc).astype(BF16), w2_ref[c], preferred_element_type=F32)
        dst = kc_ref if c == 0 else vc_ref
        dst[0] = out.reshape(CHUNK_PAGES, KV_A * BLK_PER_PAGE, HD_A)


def compress_kv_paged(page_table, cache_t, layer, pe, w1, w2):
    nseq, npages = page_table.shape
    nkc = npages // CHUNK_PAGES
    rows8 = KV_A * BLK_PER_PAGE
    shape = jax.ShapeDtypeStruct((nseq, npages, rows8, HD_A), F32)
    ospec = pl.BlockSpec((1, CHUNK_PAGES, rows8, HD_A), lambda n, k, pt: (n, k, 0, 0))
    kernel = functools.partial(_cmpmlp_paged_kernel, layer)
    kc, vc = pl.pallas_call(
        kernel,
        grid_spec=pltpu.PrefetchScalarGridSpec(
            num_scalar_prefetch=1, grid=(nseq, nkc),
            in_specs=[pl.BlockSpec(memory_space=pl.ANY),
                      pl.BlockSpec((CMP_BLK * 2, 1, HD_A), lambda n, k, pt: (0, 0, 0)),
                      pl.BlockSpec((2, CMP_BLK * HD_A, HD_A), lambda n, k, pt: (0, 0, 0)),
                      pl.BlockSpec((2, HD_A, HD_A), lambda n, k, pt: (0, 0, 0))],
            out_specs=[ospec, ospec],
            scratch_shapes=[pltpu.VMEM((2, 2, CMP_BLK, CHUNK_PAGES * rows8, HD_A), F32),
                            pltpu.SemaphoreType.DMA((2,))]),
        out_shape=[shape, shape],
        compiler_params=_cp("arbitrary", "arbitrary"),
        name="compress_kv_paged",
    )(page_table, cache_t, pe.reshape(CMP_BLK * 2, 1, HD_A), w1.reshape(2, CMP_BLK * HD_A, HD_A), w2)

    def regroup(x):
        x = x.reshape(nseq, npages, KV_A, BLK_PER_PAGE, HD_A).transpose(0, 2, 1, 3, 4)
        return x.reshape(nseq, KV_A, npages * BLK_PER_PAGE, HD_A).astype(BF16)

    return regroup(kc), regroup(vc)


def _cmpmlp_kernel(x00_ref, x01_ref, x10_ref, x11_ref, pe_ref, w1_ref, w2_ref, kc_ref, vc_ref):
    x_refs = ((x00_ref, x01_ref), (x10_ref, x11_ref))
    nblk = x00_ref.shape[1] // CMP_BLK
    for c in range(2):
        def body(s, acc):
            xs = [x_refs[c][g][0, pl.ds(s, nblk, stride=CMP_BLK), :] for g in range(KV_A)]
            x = jnp.concatenate(xs, axis=0) + pe_ref[s * 2 + c]
            return acc + jnp.dot(x.astype(BF16), w1_ref[c, s], preferred_element_type=F32)

        acc = lax.fori_loop(0, CMP_BLK, body, jnp.zeros((KV_A * nblk, HD_A), F32))
        out = jnp.dot(_gelu(acc).astype(BF16), w2_ref[c], preferred_element_type=F32)
        dst = kc_ref if c == 0 else vc_ref
        for g in range(KV_A):
            dst[0, g] = out[g * nblk:(g + 1) * nblk].astype(BF16)


def compress_kv(src, lead, col_blk, nseq, seq_rows, pe, w1, w2):
    rows = min(CMP_ROWS, seq_rows)
    nchunk = seq_rows // rows
    nb = seq_rows // CMP_BLK
    shape = jax.ShapeDtypeStruct((nseq, KV_A, nb, HD_A), BF16)
    ospec = pl.BlockSpec((1, KV_A, rows // CMP_BLK, HD_A), lambda n, c: (n, 0, c, 0))

    def xspec(part):
        return pl.BlockSpec((1, rows, HD_A), lambda n, c: (lead, n * nchunk + c, col_blk * (ROW_W // HD_A) + part))

    return pl.pallas_call(
        _cmpmlp_kernel,
        grid=(nseq, nchunk),
        in_specs=[xspec(0), xspec(1), xspec(2), xspec(3),
                  pl.BlockSpec((CMP_BLK * 2, 1, HD_A), lambda n, c: (0, 0, 0)),
                  pl.BlockSpec((2, CMP_BLK, HD_A, HD_A), lambda n, c: (0, 0, 0, 0)),
                  pl.BlockSpec((2, HD_A, HD_A), lambda n, c: (0, 0, 0))],
        out_specs=[ospec, ospec],
        out_shape=[shape, shape],
        compiler_params=_cp("arbitrary", "arbitrary"),
        name="compress_kv",
    )(src, src, src, src, pe.reshape(CMP_BLK * 2, 1, HD_A), w1, w2)


def _select_blocks(imp, q0, ns):
    nsp, tqp = imp.shape
    j = lax.broadcasted_iota(jnp.int32, (nsp, tqp), 0)
    cur = lax.shift_right_logical(q0 + lax.broadcasted_iota(jnp.int32, (nsp, tqp), 1), 6)
    forced = (j == 0) | (j == cur) | (j == cur - 1)
    score = jnp.where(j <= cur, imp + jnp.where(forced, FORCE_BONUS, 0.0), -1.0)
    score = jnp.where(j < ns, score, -2.0)
    sel = jnp.zeros((nsp, tqp), F32)
    for _ in range(N_SEL):
        top = jnp.max(score, axis=0, keepdims=True)
        idx = jnp.min(jnp.where(score == top, j, 1 << 30), axis=0, keepdims=True)
        hit = j == idx
        sel = jnp.where(hit, 1.0, sel)
        score = jnp.where(hit, -3.0, score)
    return sel


def _cmp_bias_table(tb_ref, rb_ref, g, tq, nb):
    near = (MAX_DIST + CMP_BLK - 1) // CMP_BLK
    qi = lax.broadcasted_iota(jnp.int32, (tq, nb), 0)
    c = lax.broadcasted_iota(jnp.int32, (tq, nb), 1)
    d = jnp.where(c < near + tq // CMP_BLK, qi + near * CMP_BLK - c * CMP_BLK - (CMP_BLK - 1), 4 * MAX_DIST)
    for h in range(HPG_A):
        tb_ref[h] = _t5_bias(d, rb_ref, g * HPG_A + h)


def _cmp_bias_shift(q0, nb):
    near = (MAX_DIST + CMP_BLK - 1) // CMP_BLK
    return lax.rem(q0 // CMP_BLK - near + nb, nb)


def _cmpattn_kernel(q_pos0, ns, rb_ref, q_ref, kc_ref, vc_ref, pool_ref, o_ref, sel_ref):
    g = pl.program_id(1)
    qt = pl.program_id(2)
    tq = q_ref.shape[0]
    nb = kc_ref.shape[2]
    q0 = q_pos0 + qt * tq
    qpos = q0 + lax.broadcasted_iota(jnp.int32, (tq, nb), 0)
    dist = qpos - (lax.broadcasted_iota(jnp.int32, (tq, nb), 1) * CMP_BLK + (CMP_BLK - 1))
    vis = dist >= 0
    anyvis = ((q0 + lax.broadcasted_iota(jnp.int32, (tq, 1), 0)) >= CMP_BLK - 1).astype(F32)
    kc = kc_ref[0, 0]
    vc = vc_ref[0, 0]
    psum = jnp.zeros((tq, nb), F32)
    for h in range(HPG_A):
        hs = slice(h * HD_A, (h + 1) * HD_A)
        lc = lax.dot_general(q_ref[:, hs].astype(BF16), kc, (((1,), (1,)), ((), ())), preferred_element_type=F32)
        lc = lc * SCALE + _t5_bias(dist, rb_ref, g * HPG_A + h)
        lc = jnp.where(vis, lc, NEG_INF)
        e = jnp.exp(lc - jnp.max(lc, axis=-1, keepdims=True))
        p = e / jnp.sum(e, axis=-1, keepdims=True) * anyvis
        o_ref[:, hs] = jnp.dot(p.astype(BF16), vc, preferred_element_type=F32)
        psum = psum + p

    tqp = max(tq, 128)
    if tq < tqp:
        psum = jnp.concatenate([psum, jnp.zeros((tqp - tq, nb), F32)], axis=0)
    imp = lax.dot_general(pool_ref[...], psum, (((1,), (1,)), ((), ())), preferred_element_type=F32,
                          precision=lax.Precision.HIGHEST)
    sel = _select_blocks(imp, q0, ns)
    sel_ref[0] = sel.T[0:tq].astype(BF16)


def compressed_attention(proj, kc, vc, rel_bias, nseq, seq_rows, tq, q_pos0, ns):
    r = nseq * seq_rows
    nb = kc.shape[2]
    nsp = -(-ns // 128) * 128
    nqt = seq_rows // tq
    pool = np.zeros((nsp, nb), np.float32)
    pool[np.arange(nb) // (SLC_BLK // CMP_BLK), np.arange(nb)] = 1.0
    kernel = functools.partial(_cmpattn_kernel, q_pos0, ns)
    kvspec = pl.BlockSpec((1, 1, nb, HD_A), lambda n, g, t: (n, g, 0, 0))
    sel_spec = pl.BlockSpec((1, tq, nsp), lambda n, g, t: (g, n * nqt + t, 0))
    sel_shape = jax.ShapeDtypeStruct((KV_A, r, nsp), BF16)
    return pl.pallas_call(
        kernel,
        grid=(nseq, KV_A, nqt),
        in_specs=[pl.BlockSpec(memory_space=pltpu.SMEM),
                  pl.BlockSpec((tq, HPG_A * HD_A), lambda n, g, t: (n * nqt + t, g)),
                  kvspec, kvspec,
                  pl.BlockSpec((nsp, nb), lambda n, g, t: (0, 0))],
        out_specs=[pl.BlockSpec((tq, HPG_A * HD_A), lambda n, g, t: (n * nqt + t, g)), sel_spec],
        out_shape=[jax.ShapeDtypeStruct((r, H_A * HD_A), F32), sel_shape],
        compiler_params=_cp("arbitrary", "arbitrary", "arbitrary"),
        name="compressed_attention",
    )(rel_bias, proj, kc, vc, jnp.asarray(pool))


def _softmax_init(m_ref, l_ref, acc_ref):
    m_ref[...] = jnp.full(m_ref.shape, M_INIT, F32)
    l_ref[...] = jnp.zeros(l_ref.shape, F32)
    acc_ref[...] = jnp.zeros(acc_ref.shape, F32)


PTQ = 128


def _attn_sweep_t(k_ref, q_t, vt_ref, bias_ref, tab_of, first, last, s_ref, p_ref, m_ref, l_ref, acc_ref,
                  blockmask_ref=None):
    tq = PTQ
    rows = q_t.shape[1]

    def scores(kc):
        k0 = pl.multiple_of(kc * tq, tq)
        return jnp.dot(k_ref[pl.ds(k0, tq), :], q_t, preferred_element_type=F32)

    _softmax_init(m_ref, l_ref, acc_ref)
    s_ref[0] = scores(first)
    p_ref[...] = jnp.zeros(p_ref.shape, BF16)

    def body(i, carry):
        kc = first + i
        slot = i % 2
        s = s_ref[slot] + bias_ref[tab_of(kc)]
        if blockmask_ref is not None:
            per = tq // SLC_BLK
            s = s + jnp.concatenate([jnp.broadcast_to(blockmask_ref[kc * per + j], (SLC_BLK, rows))
                                     for j in range(per)], axis=0)
        s_ref[1 - slot] = scores(jnp.minimum(kc + 1, last))
        pv = jnp.dot(vt_ref[jnp.maximum(kc - 1, first)], p_ref[...], preferred_element_type=F32)
        m_prev = m_ref[...]
        m_new = jnp.maximum(m_prev, jnp.max(s, axis=0, keepdims=True))
        alpha = jnp.exp2((m_prev - m_new) * EXP2_SCALE)
        p = jnp.exp2((s - m_new) * EXP2_SCALE)
        l_ref[...] = alpha * l_ref[...] + jnp.sum(p, axis=0, keepdims=True)
        acc_ref[...] = alpha * (acc_ref[...] + pv)
        p_ref[...] = p.astype(BF16)
        m_ref[...] = m_new
        return carry

    lax.fori_loop(0, last - first + 1, body, 0)
    acc_ref[...] += jnp.dot(vt_ref[last], p_ref[...], preferred_element_type=F32)


def _prompt_attn_kernel(seq, ns, rb_ref, q_ref, kc_ref, vc_ref, pool_ref, ks_ref, vs_ref, kw_ref, vw_ref, gn_ref,
                        o_ref, ksb_ref, vst_ref, kwb_ref, vwt_ref, btab_ref, tb_ref, s_ref, p_ref, m_ref, l_ref,
                        acc_ref, os_ref, oc_ref, selrow_ref):
    g = pl.program_id(1)
    qt = pl.program_id(2)
    tq = PTQ
    nb = kc_ref.shape[2]

    @pl.when(qt == 0)
    def _():
        _cmp_bias_table(tb_ref, rb_ref, g, tq, nb)
        ksb_ref[...] = ks_ref[...].astype(BF16)
        kwb_ref[...] = kw_ref[...].astype(BF16)

        def transpose_chunk(c, carry):
            r0 = pl.multiple_of(c * tq, tq)
            vst_ref[c] = vs_ref[pl.ds(r0, tq), :].T.astype(BF16)
            vwt_ref[c] = vw_ref[pl.ds(r0, tq), :].T.astype(BF16)
            return carry

        lax.fori_loop(0, seq // tq, transpose_chunk, 0)
        kj = lax.broadcasted_iota(jnp.int32, (tq, tq), 0)
        qi = lax.broadcasted_iota(jnp.int32, (tq, tq), 1)
        for h in range(HPG_A):
            head = g * HPG_A + h
            cols = slice(h * tq, (h + 1) * tq)
            inv = 1.0 / SCALE
            far = jnp.full((tq, tq), rb_ref[N_BUCKETS - 1, head], F32) * inv
            btab_ref[0, :, cols] = _t5_bias(qi - kj + tq, rb_ref, head) * inv
            btab_ref[1, :, cols] = jnp.where(kj <= qi, _t5_bias(qi - kj, rb_ref, head) * inv, NEG_INF)
            btab_ref[2, :, cols] = jnp.where(kj > qi, far, NEG_INF)
            btab_ref[3, :, cols] = far

    q_t = jnp.concatenate([q_ref[:, h * HD_A:(h + 1) * HD_A].T.astype(BF16) for h in range(HPG_A)], axis=1)

    q0 = qt * tq
    shift = _cmp_bias_shift(q0, nb)
    bias_c = jnp.concatenate([pltpu.roll(tb_ref[h], shift, 1).T for h in range(HPG_A)], axis=1)
    blk = lax.broadcasted_iota(jnp.int32, (nb, tq), 0)
    qpos = q0 + lax.broadcasted_iota(jnp.int32, (nb, tq), 1)
    vis = jnp.concatenate([qpos >= blk * CMP_BLK + (CMP_BLK - 1)] * HPG_A, axis=1)
    anyvis = jnp.concatenate([(qpos[0:1] >= CMP_BLK - 1).astype(F32)] * HPG_A, axis=1)
    lc = jnp.dot(kc_ref[0, 0], q_t, preferred_element_type=F32) * SCALE + bias_c
    lc = jnp.where(vis, lc, NEG_INF)
    e = jnp.exp(lc - jnp.max(lc, axis=0, keepdims=True))
    pc = e / jnp.sum(e, axis=0, keepdims=True) * anyvis
    vc_t = vc_ref[0, 0].astype(F32).T.astype(BF16)
    oc_ref[...] = jnp.dot(vc_t, pc.astype(BF16), preferred_element_type=F32)
    psum = pc[:, 0:tq]
    for h in range(1, HPG_A):
        psum = psum + pc[:, h * tq:(h + 1) * tq]
    imp = jnp.dot(pool_ref[...], psum, preferred_element_type=F32, precision=lax.Precision.HIGHEST)
    sel = _select_blocks(imp, q0, ns)
    selterm = jnp.concatenate([(sel - 1.0) * MASK_BIG] * HPG_A, axis=1)
    for j in range(ns):
        selrow_ref[j] = selterm[j:j + 1, :]

    back = WINDOW // tq

    def slc_tab(kc):
        return jnp.where(kc == qt, 1, jnp.where(kc == qt - 1, 0, 3))

    def win_tab(kc):
        return jnp.where(kc == qt - back, 2, slc_tab(kc))

    _attn_sweep_t(ksb_ref, q_t, vst_ref, btab_ref, slc_tab, 0, qt, s_ref, p_ref, m_ref, l_ref, acc_ref, selrow_ref)
    os_ref[...] = acc_ref[...] / l_ref[...]
    _attn_sweep_t(kwb_ref, q_t, vwt_ref, btab_ref, win_tab, jnp.maximum(qt - back, 0), qt, s_ref, p_ref, m_ref,
                  l_ref, acc_ref)
    ow = acc_ref[...] / l_ref[...]

    gates = jax.nn.sigmoid(gn_ref[...])
    gates_t = gates.T
    for h in range(HPG_A):
        lo, hi = 3 * h, 3 * (HPG_A + h)
        g0, g1, g2 = (jnp.where(g == 0, gates_t[lo + b:lo + b + 1, :], gates_t[hi + b:hi + b + 1, :])
                      for b in range(3))
        cols = slice(h * tq, (h + 1) * tq)
        mix = g0 * oc_ref[:, cols] + g1 * os_ref[:, cols] + g2 * ow[:, cols]
        o_ref[:, h * HD_A:(h + 1) * HD_A] = mix.T.astype(BF16)


def prompt_attention(proj, kc, vc, rel_bias, nseq, seq):
    r = nseq * seq
    nqt = seq // PTQ
    nb = kc.shape[2]
    ns = -(-seq // SLC_BLK)
    nsp = 128
    assert ns <= nsp and nb == 128 and seq % PTQ == 0
    pool = np.zeros((nsp, nb), np.float32)
    pool[np.arange(nb) // (SLC_BLK // CMP_BLK), np.arange(nb)] = 1.0
    kernel = functools.partial(_prompt_attn_kernel, seq, ns)
    kvspec = pl.BlockSpec((1, 1, nb, HD_A), lambda n, g, t: (n, g, 0, 0))

    def kv(col):
        return pl.BlockSpec((seq, HD_A), lambda n, g, t: (n, col // HD_A + g))

    qspec = pl.BlockSpec((PTQ, HPG_A * HD_A), lambda n, g, t: (n * nqt + t, g))
    rows8 = HPG_A * PTQ
    return pl.pallas_call(
        kernel,
        grid=(nseq, KV_A, nqt),
        in_specs=[pl.BlockSpec(memory_space=pltpu.SMEM),
                  qspec,
                  kvspec, kvspec,
                  pl.BlockSpec((nsp, nb), lambda n, g, t: (0, 0)),
                  kv(C_KVS), kv(C_KVS + KV_A * HD_A), kv(C_KVW), kv(C_KVW + KV_A * HD_A),
                  pl.BlockSpec((PTQ, 128), lambda n, g, t: (n * nqt + t, C_GN // 128))],
        out_specs=qspec,
        out_shape=jax.ShapeDtypeStruct((r, H_A * HD_A), BF16),
        scratch_shapes=[pltpu.VMEM((seq, HD_A), BF16), pltpu.VMEM((nqt, HD_A, PTQ), BF16),
                        pltpu.VMEM((seq, HD_A), BF16), pltpu.VMEM((nqt, HD_A, PTQ), BF16),
                        pltpu.VMEM((4, PTQ, rows8), F32), pltpu.VMEM((HPG_A, PTQ, nb), F32),
                        pltpu.VMEM((2, PTQ, rows8), F32), pltpu.VMEM((PTQ, rows8), BF16),
                        pltpu.VMEM((1, rows8), F32), pltpu.VMEM((1, rows8), F32), pltpu.VMEM((HD_A, rows8), F32),
                        pltpu.VMEM((HD_A, rows8), F32), pltpu.VMEM((HD_A, rows8), F32),
                        pltpu.VMEM((ns, 1, rows8), F32)],
        compiler_params=_cp("arbitrary", "arbitrary", "arbitrary"),
        name="prompt_attention",
    )(rel_bias, proj, kc, vc, jnp.asarray(pool), proj, proj, proj, proj, proj)


SUB = 512


def _softmax_step_t(s, vt, m_ref, l_ref, acc_ref):
    m_prev = m_ref[...]
    m_new = jnp.maximum(m_prev, jnp.max(s, axis=0, keepdims=True))
    alpha = jnp.exp(m_prev - m_new)
    p = jnp.exp(s - m_new)
    l_ref[...] = alpha * l_ref[...] + jnp.sum(p, axis=0, keepdims=True)
    acc_ref[...] = alpha * acc_ref[...] + jnp.dot(vt, p.astype(BF16), preferred_element_type=F32)
    m_ref[...] = m_new


def _sample_attn_kernel(past, layer, pt_ref, rb_ref, q_ref, sel_ref, cache_ref, kn_ref, vn_ref, kwc_ref, vwc_ref,
                        kwn_ref, vwn_ref, oc_ref, gn_ref, o_ref, buf_ref, sem, selt_ref, m_ref, l_ref, acc_ref):
    kc = pl.program_id(1)
    nkc = pl.num_programs(1)
    tq = q_ref.shape[0]
    rows8 = HPG_A * tq
    chunk = CHUNK_PAGES * PAGE_SIZE
    ns_last = past // SLC_BLK

    def copies_for(n, k, slot):
        return [pltpu.make_async_copy(cache_ref.at[layer, pt_ref[n, k * CHUNK_PAGES + p]],
                                      buf_ref.at[slot, :, :, pl.ds(p * PAGE_SIZE, PAGE_SIZE), :], sem.at[slot])
                for p in range(CHUNK_PAGES)]

    slot = _paged_chunk_pipeline(pt_ref, copies_for, nkc)
    laneh = lax.shift_right_logical(lax.broadcasted_iota(jnp.int32, (1, rows8), 1), int(math.log2(tq)))
    gates = jax.nn.sigmoid(gn_ref[...])
    per_sub = SUB // SLC_BLK

    for g in range(KV_A):
        stats = (m_ref.at[g], l_ref.at[g], acc_ref.at[g])
        q_rows = jnp.concatenate([q_ref[:, (g * HPG_A + h) * HD_A:(g * HPG_A + h + 1) * HD_A] for h in range(HPG_A)],
                                 axis=0)
        q_t = q_rows.T.astype(BF16)
        rb_rows = []
        for k in range(N_BUCKETS):
            row = jnp.zeros((1, rows8), F32)
            for h in range(HPG_A):
                row = jnp.where(laneh == h, rb_ref[k, g * HPG_A + h], row)
            rb_rows.append(row)
        far = rb_rows[N_BUCKETS - 1]

        def lane_bias(dist):
            out = jnp.broadcast_to(rb_rows[0], dist.shape)
            for k, thr in enumerate(_T5_THR):
                out = jnp.where(dist >= thr, rb_rows[k + 1], out)
            return out

        @pl.when(kc == 0)
        def _():
            _softmax_init(*stats)
            selterm = (sel_ref[g].astype(F32) - 1.0) * MASK_BIG
            selt_ref[g] = jnp.concatenate([selterm] * HPG_A, axis=0).T

        for sub in range(chunk // SUB):
            k = buf_ref[slot, 0, g, sub * SUB:(sub + 1) * SUB, :]
            vt = buf_ref[slot, 1, g, sub * SUB:(sub + 1) * SUB, :].astype(F32).T.astype(BF16)
            j0 = pl.multiple_of(kc * (chunk // SLC_BLK) + sub * per_sub, per_sub)
            blkrows = selt_ref[g, pl.ds(j0, per_sub), :]
            mask = jnp.concatenate([jnp.broadcast_to(blkrows[j:j + 1], (SLC_BLK, rows8)) for j in range(per_sub)],
                                   axis=0)
            s = jnp.dot(k, q_t, preferred_element_type=F32) * SCALE + mask
            if sub < chunk // SUB - 1:
                _softmax_step_t(s + far, vt, *stats)
            else:
                @pl.when(kc < nkc - 1)
                def _():
                    _softmax_step_t(s + far, vt, *stats)

                @pl.when(kc == nkc - 1)
                def _():
                    kj = lax.broadcasted_iota(jnp.int32, (SUB, rows8), 0)
                    qi = lax.broadcasted_iota(jnp.int32, (SUB, rows8), 1) & (tq - 1)
                    _softmax_step_t(s + lane_bias(qi + SUB - kj), vt, *stats)

        @pl.when(kc == nkc - 1)
        def _():
            m_g, l_g, acc_g = stats
            gk = slice(g * HD_A, (g + 1) * HD_A)
            kj = lax.broadcasted_iota(jnp.int32, (128, rows8), 0)
            qi = lax.broadcasted_iota(jnp.int32, (128, rows8), 1) & (tq - 1)
            pad = jnp.zeros((128 - tq, HD_A), F32)
            b_new = jnp.where(kj <= qi, lane_bias(qi - kj), NEG_INF)
            k_new = jnp.concatenate([kn_ref[:, gk], pad], axis=0).astype(BF16)
            v_new_t = jnp.concatenate([vn_ref[:, gk], pad], axis=0).T.astype(BF16)
            s = jnp.dot(k_new, q_t, preferred_element_type=F32) * SCALE + b_new + selt_ref[g, ns_last:ns_last + 1, :]
            _softmax_step_t(s, v_new_t, *stats)
            o_s = (acc_g[...] / l_g[...]).T
            _softmax_init(*stats)
            wrows = kwc_ref.shape[2]
            kj_w = lax.broadcasted_iota(jnp.int32, (wrows, rows8), 0)
            qi_w = lax.broadcasted_iota(jnp.int32, (wrows, rows8), 1) & (tq - 1)
            dw = qi_w + wrows - kj_w
            s = jnp.dot(kwc_ref[0, 0, :, gk].astype(BF16), q_t, preferred_element_type=F32) * SCALE
            s = s + jnp.where(dw < WINDOW, lane_bias(dw), NEG_INF)
            _softmax_step_t(s, vwc_ref[0, 0, :, gk].T.astype(BF16), *stats)
            k_new = jnp.concatenate([kwn_ref[:, gk], pad], axis=0).astype(BF16)
            v_new_t = jnp.concatenate([vwn_ref[:, gk], pad], axis=0).T.astype(BF16)
            _softmax_step_t(jnp.dot(k_new, q_t, preferred_element_type=F32) * SCALE + b_new, v_new_t, *stats)
            o_w = (acc_g[...] / l_g[...]).T
            for h in range(HPG_A):
                lane = 3 * (g * HPG_A + h)
                rows = slice(h * tq, (h + 1) * tq)
                hs = slice((g * HPG_A + h) * HD_A, (g * HPG_A + h + 1) * HD_A)
                o_ref[:, hs] = (gates[:, lane:lane + 1] * oc_ref[:, hs] + gates[:, lane + 1:lane + 2] * o_s[rows, :]
                                + gates[:, lane + 2:lane + 3] * o_w[rows, :]).astype(BF16)


def sample_attention(proj, o_cmp, sel, page_table, cache_t, layer, cache_win, rel_bias, nseq, tq, past):
    r = nseq * tq
    nsp = sel.shape[2]
    chunk = CHUNK_PAGES * PAGE_SIZE
    nkc = past // chunk
    kernel = functools.partial(_sample_attn_kernel, past, layer)
    wrows = cache_win.shape[2]
    assert wrows == WINDOW
    gw = KV_A * HD_A

    def new_rows(col):
        return pl.BlockSpec((tq, gw), lambda n, k, pt: (n, col // gw))

    def win_rows(c):
        return pl.BlockSpec((1, 1, wrows, gw), lambda n, k, pt: (layer, n, 0, c))

    qspec = pl.BlockSpec((tq, H_A * HD_A), lambda n, k, pt: (n, 0))
    rows8 = HPG_A * tq
    return pl.pallas_call(
        kernel,
        grid_spec=pltpu.PrefetchScalarGridSpec(
            num_scalar_prefetch=1, grid=(nseq, nkc),
            in_specs=[pl.BlockSpec(memory_space=pltpu.SMEM),
                      qspec,
                      pl.BlockSpec((KV_A, tq, nsp), lambda n, k, pt: (0, n, 0)),
                      pl.BlockSpec(memory_space=pl.ANY),
                      new_rows(C_KVS), new_rows(C_KVS + gw),
                      win_rows(0), win_rows(1),
                      new_rows(C_KVW), new_rows(C_KVW + gw),
                      qspec,
                      pl.BlockSpec((tq, 128), lambda n, k, pt: (n, C_GN // 128))],
            out_specs=qspec,
            scratch_shapes=[pltpu.VMEM((2, 2, KV_A, chunk, HD_A), BF16),
                            pltpu.SemaphoreType.DMA((2,)), pltpu.VMEM((KV_A, nsp, rows8), F32),
                            pltpu.VMEM((KV_A, 1, rows8), F32), pltpu.VMEM((KV_A, 1, rows8), F32),
                            pltpu.VMEM((KV_A, HD_A, rows8), F32)]),
        out_shape=jax.ShapeDtypeStruct((r, H_A * HD_A), BF16),
        compiler_params=_cp("arbitrary", "arbitrary"),
        name="sample_attention",
    )(page_table, rel_bias, proj, sel, cache_t, proj, proj, cache_win, cache_win, proj, proj, o_cmp, proj)


def _prep_w_in(w):
    src = 0
    pieces = []
    for width, dst in _SEGS:
        pieces.append((dst, w[:, src:src + width]))
        src += width
    pieces.sort(key=lambda p: p[0])
    cols, pos = [], 0
    for dst, piece in pieces:
        assert dst >= pos
        if dst > pos:
            cols.append(jnp.zeros((w.shape[0], dst - pos), w.dtype))
        cols.append(piece)
        pos = dst + piece.shape[1]
    if pos < IN_PAD:
        cols.append(jnp.zeros((w.shape[0], IN_PAD - pos), w.dtype))
    return jnp.concatenate(cols, axis=1).astype(BF16)


def _state_rows(state, nseq, width):
    k = state.shape[1]
    return jnp.concatenate([jnp.zeros((nseq, 8 - k, width), F32), state.astype(F32)], axis=1)


def _trunk_layer(x, p, rel_bias, nseq, seq_rows, valid, tiles, attn_fn, conv_b_prev, conv_f_prev, h0t):
    tm, tm_merge, tm_ffn = tiles
    proj = rms_matmul(x, p['norm1'], p['w_in'], tm, 1280)
    o_a = attn_fn(proj)
    o_b, h_t = ssd_mixer(proj, conv_b_prev, p['conv_b_w'], p['conv_b_bias'], p['dt_bias'], p['a_log'], p['d_skip'],
                         p['ssm_norm'], h0t, nseq, seq_rows, valid)
    o_c, v_rows = chunk_mlp_mixer(proj, p['v_ln_g'], p['v_ln_b'], p['w_spatial'], p['b_spatial'], seq_rows)
    merged = gated_merge(o_a, o_b, o_c, p['w_br_a'], p['w_br_b'], p['w_br_c'], proj, tm_merge, 512)
    x = matmul_residual(merged, p['w_out'], x, tm, 1024)
    h = rms_matmul(x, p['norm2'], p['w_ff_in'], tm, 1024)
    x = ffn_out(h, conv_f_prev, p['ff_conv_w'], p['ff_conv_b'], p['w_ff_out'], x, seq_rows, tm_ffn, 512)
    return x, proj, h, h_t, v_rows


def kernel(x_prompt, x_sample, cache_cmp_kv, cache_slc_kv, cache_win_kv, state_ssm, state_conv_ssm, state_conv_ffn,
           page_table, rel_bias, norm1, w_in, cmp_pe, cmp_w1, cmp_w2, conv_b_w, conv_b_bias, dt_bias, a_log, d_skip,
           ssm_norm, v_ln_g, v_ln_b, w_spatial, b_spatial, w_br_a, w_br_b, w_br_c, w_out, norm2, w_ff_in,
           ff_conv_w, ff_conv_b, w_ff_out, final_norm):
    bp, seq = x_prompt.shape[:2]
    bs, dec = x_sample.shape[:2]
    depth = w_in.shape[0]
    n_pool = cache_cmp_kv.shape[1]
    past = page_table.shape[1] * PAGE_SIZE
    st = SAMPLE_T
    kv_shape = (2, KV_A, HD_A)

    page_table = page_table.astype(jnp.int32)
    cmp_t = cache_cmp_kv.reshape(depth, n_pool, BLK_PER_PAGE, CMP_BLK, 2, KV_A, HD_A).transpose(0, 1, 4, 3, 5, 2, 6)
    cmp_t = cmp_t.reshape(depth, n_pool, 2, CMP_BLK, KV_A * BLK_PER_PAGE, HD_A)
    slc_t = cache_slc_kv.transpose(0, 1, 3, 4, 2, 5).astype(BF16)
    cache_win = cache_win_kv.reshape(depth, bs, WINDOW, ROW_W)

    xp = x_prompt.reshape(bp * seq, D_MODEL)
    xs = jnp.pad(x_sample, ((0, 0), (0, st - dec), (0, 0))).reshape(bs * st, D_MODEL)
    zeros_b = jnp.zeros((bp, 8, XBC_DIM), F32)
    zeros_f = jnp.zeros((bp, 8, D_FF), F32)
    zeros_h = jnp.zeros((bp, N_B, D_INNER_B), F32)
    ns_s = -(-(past + dec) // SLC_BLK)

    outs = [[] for _ in range(13)]
    for l in range(depth):
        p = {'norm1': norm1[l], 'w_in': _prep_w_in(w_in[l]), 'conv_b_w': jnp.pad(conv_b_w[l], ((0, 8 - CONV_B), (0, 0))),
             'conv_b_bias': conv_b_bias[l], 'dt_bias': dt_bias[l], 'a_log': a_log[l], 'd_skip': d_skip[l],
             'ssm_norm': ssm_norm[l], 'v_ln_g': v_ln_g[l], 'v_ln_b': v_ln_b[l], 'w_spatial': w_spatial[l],
             'b_spatial': b_spatial[l], 'w_br_a': w_br_a[l].astype(BF16), 'w_br_b': w_br_b[l].astype(BF16),
             'w_br_c': w_br_c[l].astype(BF16), 'w_out': w_out[l].astype(BF16), 'norm2': norm2[l],
             'w_ff_in': w_ff_in[l].astype(BF16), 'ff_conv_w': jnp.pad(ff_conv_w[l], ((0, 8 - CONV_F), (0, 0))),
             'ff_conv_b': ff_conv_b[l], 'w_ff_out': w_ff_out[l].astype(BF16)}
        pe, w1, w2 = cmp_pe[l], cmp_w1[l].astype(BF16), cmp_w2[l].astype(BF16)

        def attn_prompt(proj):
            kc, vc = compress_kv(proj.reshape(1, bp * seq, IN_PAD), 0, C_KVC // ROW_W, bp, seq, pe, w1, w2)
            return prompt_attention(proj, kc, vc, rel_bias, bp, seq)

        def attn_sample(proj):
            kc, vc = compress_kv_paged(page_table, cmp_t, l, pe, w1, w2)
            o_cmp, sel = compressed_attention(proj, kc, vc, rel_bias, bs, st, st, past, ns_s)
            return sample_attention(proj, o_cmp, sel, page_table, slc_t, l, cache_win, rel_bias, bs, st, past)

        xp, proj, h, h_t, _ = _trunk_layer(xp, p, rel_bias, bp, seq, seq, (1024, 512, 1024), attn_prompt,
                                           zeros_b, zeros_f, zeros_h)
        pr = proj.reshape(bp, seq, IN_PAD)
        outs[0].append(pr[:, :, C_KVC:C_KVC + ROW_W].reshape((bp, seq) + kv_shape))
        outs[1].append(pr[:, :, C_KVS:C_KVS + ROW_W].reshape((bp, seq) + kv_shape))
        outs[2].append(pr[:, seq - WINDOW:, C_KVW:C_KVW + ROW_W].reshape((bp, WINDOW) + kv_shape))
        outs[3].append(h_t.reshape(bp, N_B, H_B, HD_B).transpose(0, 2, 3, 1))
        outs[4].append(pr[:, seq - (CONV_B - 1):, C_XBC:C_XBC + XBC_DIM])
        outs[5].append(h.reshape(bp, seq, 2 * D_FF)[:, seq - (CONV_F - 1):, :D_FF])

        h0t = state_ssm[l].astype(F32).transpose(0, 3, 1, 2).reshape(bs, N_B, D_INNER_B)
        xs, proj, h, h_t, v_rows = _trunk_layer(xs, p, rel_bias, bs, st, dec, (bs * st, bs * st, st), attn_sample,
                                                _state_rows(state_conv_ssm[l], bs, XBC_DIM),
                                                _state_rows(state_conv_ffn[l], bs, D_FF), h0t)
        pr = proj.reshape(bs, st, IN_PAD)
        kvw_new = pr[:, :dec, C_KVW:C_KVW + ROW_W].reshape((bs, dec) + kv_shape)
        outs[6].append(pr[:, :dec, C_KVC:C_KVC + ROW_W].reshape((bs, dec) + kv_shape))
        outs[7].append(pr[:, :dec, C_KVS:C_KVS + ROW_W].reshape((bs, dec) + kv_shape))
        outs[8].append(jnp.concatenate([cache_win_kv[l][:, dec:], kvw_new], axis=1))
        outs[9].append(h_t.reshape(bs, N_B, H_B, HD_B).transpose(0, 2, 3, 1))
        xbc_all = jnp.concatenate([state_conv_ssm[l], pr[:, :dec, C_XBC:C_XBC + XBC_DIM]], axis=1)
        outs[10].append(xbc_all[:, dec:])
        a_all = jnp.concatenate([state_conv_ffn[l], h.reshape(bs, st, 2 * D_FF)[:, :dec, :D_FF]], axis=1)
        outs[11].append(a_all[:, dec:])
        outs[12].append(v_rows.reshape(bs, st, D_C)[:, :dec])

    y_prompt = rms_norm_rows(xp, final_norm, 1024).reshape(bp, seq, D_MODEL)
    y_sample = rms_norm_rows(xs, final_norm, bs * st).reshape(bs, st, D_MODEL)[:, :dec]
    return (y_prompt, y_sample) + tuple(jnp.stack(o) for o in outs)
```
